```python
import math
import jax, jax.numpy as jnp
from jax import lax
import numpy as np

D_MODEL = 2048
BATCH = 2
SEQ = 8192
DEPTH = 1
DEC_BATCH = 2
DEC_SEQ = 4096
PAST_LEN = 128

N_FNET_GROUPS = 4
FNET_GROUP_DIM = 256
D_FNET = N_FNET_GROUPS * FNET_GROUP_DIM
D_HYENA = D_MODEL // 2
HYENA_ORDER = 2
D_IN = D_FNET + (HYENA_ORDER + 1) * D_HYENA
SHORT_CONV = 3
FILTER_EMB_DIM = 33
FILTER_BANDS = (FILTER_EMB_DIM - 1) // 2
FILTER_HIDDEN = 64
DECAY_TARGET = 1e-2
FAST_DECAY_PCT = 0.3
SLOW_DECAY_PCT = 1.5
D_FF = 4 * D_MODEL
N_BRANCHES = 2
EPS = 1e-6

kernel_name = "hybrid_fnet_hyena_encoder"


def _rmsnorm(x, g):
    x32 = x.astype(jnp.float32)
    y = x32 * lax.rsqrt(jnp.mean(x32 * x32, axis=-1, keepdims=True) + EPS)
    return y.astype(x.dtype) * g


def _fnet_branch(a, w_map):
    b, l, _ = a.shape
    a4 = a.reshape(b, l, N_FNET_GROUPS, FNET_GROUP_DIM).astype(jnp.float32)
    mixed = jnp.fft.fft2(a4, axes=(1, 3), norm="ortho").real
    mixed = mixed.reshape(b, l, D_FNET).astype(a.dtype)
    return mixed @ w_map


def _hyena_filter(l, w1, b1, w2, b2, w3, b3, w4, freq):
    f32 = jnp.float32
    t = jnp.linspace(0.0, 1.0, l, dtype=f32)[:, None]
    pos = jnp.arange(l, dtype=f32)[:, None]
    bands = jnp.linspace(1e-4, FILTER_BANDS - 1, FILTER_BANDS, dtype=f32)[None, :]
    ang = (2.0 * math.pi / l) * pos * bands
    z = jnp.concatenate([t, jnp.cos(ang), -jnp.sin(ang)], axis=-1)
    fr = freq.astype(f32)
    h = jnp.sin(fr * (z @ w1.astype(f32) + b1.astype(f32)))
    h = jnp.sin(fr * (h @ w2.astype(f32) + b2.astype(f32)))
    h = jnp.sin(fr * (h @ w3.astype(f32) + b3.astype(f32)))
    h = h @ w4.astype(f32)
    deltas = jnp.abs(jnp.linspace(math.log(DECAY_TARGET) / FAST_DECAY_PCT,
                                  math.log(DECAY_TARGET) / SLOW_DECAY_PCT, D_HYENA, dtype=f32))
    decay = jnp.exp(-t * deltas[None, :])
    h_fwd = h[:, :D_HYENA] * decay
    h_bwd = h[:, D_HYENA:] * decay
    k = jnp.concatenate([h_fwd, jnp.zeros((1, D_HYENA), f32), h_bwd[:0:-1]], axis=0)
    return k * lax.rsqrt(jnp.sum(k * k, axis=0, keepdims=True) + EPS)


def _hyena_branch(u, conv_w, conv_b, w1, b1, w2, b2, w3, b3, w4, freq, skip, w_out):
    l = u.shape[1]
    up = jnp.pad(u, ((0, 0), (1, 1), (0, 0)))
    u = up[:, :-2] * conv_w[0] + up[:, 1:-1] * conv_w[1] + up[:, 2:] * conv_w[2] + conv_b
    x1, x2, v = jnp.split(u, 3, axis=-1)
    v32 = (v * x2).astype(jnp.float32)
    k = _hyena_filter(l, w1, b1, w2, b2, w3, b3, w4, freq)
    v_f = jnp.fft.rfft(v32, n=2 * l, axis=1)
    k_f = jnp.fft.rfft(k, n=2 * l, axis=0)
    y = jnp.fft.irfft(v_f * k_f[None], n=2 * l, axis=1)[:, :l]
    y = y + v32 * skip.astype(jnp.float32)
    y = y.astype(u.dtype) * x1
    return y @ w_out


def _layer(x, norm1_g, w_in, conv_w, conv_b, filt_w1, filt_b1, filt_w2, filt_b2, filt_w3, filt_b3,
           filt_w4, filt_freq, hyena_skip, w_fnet_map, w_hyena_out, w_gate, b_gate, w_out,
           norm2_g, w_mlp1, w_mlp2):
    h = _rmsnorm(x, norm1_g)
    proj = h @ w_in
    y_a = _fnet_branch(proj[..., :D_FNET], w_fnet_map)
    y_b = _hyena_branch(proj[..., D_FNET:], conv_w, conv_b, filt_w1, filt_b1, filt_w2, filt_b2,
                        filt_w3, filt_b3, filt_w4, filt_freq, hyena_skip, w_hyena_out)
    gates = jax.nn.sigmoid(h @ w_gate + b_gate)
    g_a, g_b = jnp.split(gates, N_BRANCHES, axis=-1)
    x = x + (g_a * y_a + g_b * y_b) @ w_out
    h2 = _rmsnorm(x, norm2_g)
    x = x + jnp.square(jax.nn.relu(h2 @ w_mlp1)) @ w_mlp2
    return x


def setup_inputs(seed: int = 0) -> dict:
    key = jax.random.key(seed)
    ks = jax.random.split(key, 32)

    def nrm(k, shape, scale):
        return jax.random.normal(k, shape, jnp.float32) * scale

    L = DEPTH
    return {
        "x_prompt": nrm(ks[0], (BATCH, SEQ, D_MODEL), 1.0),
        "x_sample": nrm(ks[1], (DEC_BATCH, DEC_SEQ, D_MODEL), 1.0),
        "norm1_g": 1.0 + nrm(ks[2], (L, D_MODEL), 0.02),
        "w_in": nrm(ks[3], (L, D_MODEL, D_IN), D_MODEL ** -0.5),
        "conv_w": nrm(ks[4], (L, SHORT_CONV, 3 * D_HYENA), SHORT_CONV ** -0.5),
        "conv_b": nrm(ks[5], (L, 3 * D_HYENA), 0.02),
        "filt_w1": nrm(ks[6], (L, FILTER_EMB_DIM, FILTER_HIDDEN), FILTER_EMB_DIM ** -0.5),
        "filt_b1": nrm(ks[7], (L, FILTER_HIDDEN), 0.02),
        "filt_w2": nrm(ks[8], (L, FILTER_HIDDEN, FILTER_HIDDEN), FILTER_HIDDEN ** -0.5),
        "filt_b2": nrm(ks[9], (L, FILTER_HIDDEN), 0.02),
        "filt_w3": nrm(ks[10], (L, FILTER_HIDDEN, FILTER_HIDDEN), FILTER_HIDDEN ** -0.5),
        "filt_b3": nrm(ks[11], (L, FILTER_HIDDEN), 0.02),
        "filt_w4": nrm(ks[12], (L, FILTER_HIDDEN, 2 * D_HYENA), FILTER_HIDDEN ** -0.5),
        "filt_freq": 1.0 + nrm(ks[13], (L, FILTER_HIDDEN), 0.02),
        "hyena_skip": nrm(ks[14], (L, D_HYENA), 0.1),
        "w_fnet_map": nrm(ks[15], (L, D_FNET, D_MODEL), D_FNET ** -0.5),
        "w_hyena_out": nrm(ks[16], (L, D_HYENA, D_MODEL), D_HYENA ** -0.5),
        "w_gate": nrm(ks[17], (L, D_MODEL, N_BRANCHES * D_MODEL), D_MODEL ** -0.5),
        "b_gate": nrm(ks[18], (L, N_BRANCHES * D_MODEL), 0.02),
        "w_out": nrm(ks[19], (L, D_MODEL, D_MODEL), D_MODEL ** -0.5),
        "norm2_g": 1.0 + nrm(ks[20], (L, D_MODEL), 0.02),
        "w_mlp1": nrm(ks[21], (L, D_MODEL, D_FF), D_MODEL ** -0.5),
        "w_mlp2": nrm(ks[22], (L, D_FF, D_MODEL), D_FF ** -0.5),
        "norm_f_g": 1.0 + nrm(ks[23], (D_MODEL,), 0.02),
    }


def reference(x_prompt, x_sample, norm1_g, w_in, conv_w, conv_b, filt_w1, filt_b1, filt_w2, filt_b2,
              filt_w3, filt_b3, filt_w4, filt_freq, hyena_skip, w_fnet_map, w_hyena_out, w_gate,
              b_gate, w_out, norm2_g, w_mlp1, w_mlp2, norm_f_g):
    xp = x_prompt
    xs = x_sample
    for l in range(DEPTH):
        params = (norm1_g[l], w_in[l], conv_w[l], conv_b[l], filt_w1[l], filt_b1[l], filt_w2[l],
                  filt_b2[l], filt_w3[l], filt_b3[l], filt_w4[l], filt_freq[l], hyena_skip[l],
                  w_fnet_map[l], w_hyena_out[l], w_gate[l], b_gate[l], w_out[l], norm2_g[l],
                  w_mlp1[l], w_mlp2[l])
        xp = _layer(xp, *params)
        xs = _layer(xs, *params)
    y_prompt = _rmsnorm(xp, norm_f_g)
    y_sample = _rmsnorm(xs, norm_f_g)
    return (y_prompt, y_sample)
```

```python
import functools
import math

import jax
import jax.numpy as jnp
import numpy as np
from jax import lax
from jax.experimental import pallas as pl
from jax.experimental.pallas import tpu as pltpu

D_MODEL = 2048
N_FNET_GROUPS = 4
FNET_GROUP_DIM = 256
D_FNET = N_FNET_GROUPS * FNET_GROUP_DIM
D_HYENA = 1024
FILTER_BANDS = 16
FILTER_HIDDEN = 64
DECAY_TARGET = 1e-2
FAST_DECAY_PCT = 0.3
SLOW_DECAY_PCT = 1.5
D_FF = 4 * D_MODEL
EPS = 1e-6

F32 = jnp.float32
BF16 = jnp.bfloat16

_VMEM_LIMIT_BYTES = 56 * 1024 * 1024
_DFT_N2 = 128
_BF16_SUBLANES = 16

_dot = functools.partial(jnp.dot, preferred_element_type=F32)
_dot_hi = functools.partial(jnp.dot, preferred_element_type=F32,
                            precision=lax.Precision.HIGHEST)


def _params(*sem):
    return pltpu.CompilerParams(dimension_semantics=sem,
                                vmem_limit_bytes=_VMEM_LIMIT_BYTES)


def _resident(shape):
    zeros = (0,) * len(shape)
    return pl.BlockSpec(shape, lambda *_: zeros, pipeline_mode=pl.Buffered(1))


def _cos_sin(n):
    jk = (np.arange(n)[:, None] * np.arange(n)[None, :]) % n
    ang = 2.0 * np.pi * jk / n
    return np.cos(ang), np.sin(ang)


@functools.lru_cache(maxsize=None)
def _dft_tables(n1, n2):
    n = n1 * n2
    c1, s1 = _cos_sin(n1)
    c2, s2 = _cos_sin(n2)
    fr1, fi1 = c1, -s1
    fr2, fi2 = c2, -s2
    ang = 2.0 * np.pi * ((np.arange(n1)[:, None] * np.arange(n2)[None, :]) % n) / n
    twr, twi = np.cos(ang), -np.sin(ang)
    t = {}
    t["rows_fwd"] = np.concatenate([fr1, fi1], axis=0)
    t["p"] = np.block([[fr2, -fi2], [fi2, fr2]])
    t["q"] = np.block([[-fi2, -fr2], [fr2, -fi2]])
    t["tw_k1"] = np.stack([np.concatenate([twr, twr], axis=1),
                           np.concatenate([twi, twi], axis=1)], axis=1)
    t["inv2"] = np.block([[fr2, fi2], [-fi2, fr2]])
    t["pc"] = np.concatenate([fr1, fi1], axis=1)
    t["qc"] = np.concatenate([-fi1, fr1], axis=1)
    t["tw_n2"] = np.stack([np.concatenate([twr.T, twr.T], axis=1),
                           np.concatenate([twi.T, twi.T], axis=1)], axis=1)
    return t


def _const(a, dtype=F32):
    return jnp.asarray(np.asarray(a, np.float32), dtype)


def _project_kernel(x_ref, g_ref, w_ref, b_ref, a_ref, u_ref, gate_ref, h_ref):
    j = pl.program_id(1)

    @pl.when(j == 0)
    def _():
        x = x_ref[...]
        ms = jnp.mean(x * x, axis=-1, keepdims=True)
        h_ref[...] = (x * lax.rsqrt(ms + EPS) * g_ref[...]).astype(BF16)

    acc = _dot(h_ref[...], w_ref[...])

    @pl.when(j == 0)
    def _():
        a_ref[...] = acc.astype(BF16)

    @pl.when(jnp.logical_and(j >= 1, j < 4))
    def _():
        u_ref[...] = acc.astype(BF16)

    @pl.when(j >= 4)
    def _():
        gate_ref[...] = jax.nn.sigmoid(acc + b_ref[...]).astype(BF16)


def _project(x2, g, w_cat, b_gate):
    rows = x2.shape[0]
    tm, tn = 1024, 1024
    grid = (rows // tm, w_cat.shape[1] // tn)
    return pl.pallas_call(
        _project_kernel,
        grid=grid,
        in_specs=[
            pl.BlockSpec((tm, D_MODEL), lambda i, j: (i, 0)),
            pl.BlockSpec((1, D_MODEL), lambda i, j: (0, 0)),
            pl.BlockSpec((D_MODEL, tn), lambda i, j: (0, j)),
            pl.BlockSpec((1, tn), lambda i, j: (0, jnp.maximum(j - 4, 0))),
        ],
        out_specs=[
            pl.BlockSpec((tm, tn), lambda i, j: (i, 0)),
            pl.BlockSpec((tm, tn), lambda i, j: (i, jnp.clip(j - 1, 0, 2))),
            pl.BlockSpec((tm, tn), lambda i, j: (i, jnp.maximum(j - 4, 0))),
        ],
        out_shape=[
            jax.ShapeDtypeStruct((rows, D_FNET), BF16),
            jax.ShapeDtypeStruct((rows, 3 * D_HYENA), BF16),
            jax.ShapeDtypeStruct((rows, 2 * D_MODEL), BF16),
        ],
        scratch_shapes=[pltpu.VMEM((tm, D_MODEL), BF16)],
        compiler_params=_params("parallel", "arbitrary"),
        name="project",
    )(x2, g, w_cat, b_gate)


def _rowdft_kernel(lhs_ref, x_ref, o_ref):
    o_ref[0] = _dot(lhs_ref[...].astype(BF16), x_ref[0].astype(BF16)).astype(o_ref.dtype)


def _rowdft(lhs, x3, name):
    b, k, cols = x3.shape
    m = lhs.shape[0]
    w = 2048
    return pl.pallas_call(
        _rowdft_kernel,
        grid=(b, cols // w),
        in_specs=[_resident((m, k)),
                  pl.BlockSpec((1, k, w), lambda bi, c: (bi, 0, c))],
        out_specs=pl.BlockSpec((1, m, w), lambda bi, c: (bi, 0, c)),
        out_shape=jax.ShapeDtypeStruct((b, m, cols), BF16),
        compiler_params=_params("parallel", "parallel"),
        name=name,
    )(lhs, x3)


def _stage2_fwd(p_ref, q_ref, tw_ref, y_ref, i):
    g = (p_ref[...] * tw_ref[i, 0:1, :] + q_ref[...] * tw_ref[i, 1:2, :]).astype(BF16)
    ys = jnp.concatenate([y_ref[0, 0, i], y_ref[0, 1, i]], axis=0)
    return _dot(g, ys)


def _conv_mid_kernel(p_ref, q_ref, tw_ref, inv_ref, y_ref, kf_ref, o_ref, *, k1b, n2):
    for i in range(k1b):
        s = _stage2_fwd(p_ref, q_ref, tw_ref, y_ref, i)
        xr, xi = s[:n2], s[n2:]
        kr, ki = kf_ref[0, i], kf_ref[1, i]
        prod = jnp.concatenate([xr * kr - xi * ki, xr * ki + xi * kr], axis=0)
        z = _dot(inv_ref[...].astype(BF16), prod.astype(BF16))
        o_ref[0, 0, i] = z[:n2].astype(BF16)
        o_ref[0, 1, i] = z[n2:].astype(BF16)


def _filter_mid_kernel(p_ref, q_ref, tw_ref, y_ref, ss_ref, o_ref, *, k1b, n2):
    scale = lax.rsqrt(ss_ref[...] + EPS)
    for i in range(k1b):
        s = _stage2_fwd(p_ref, q_ref, tw_ref, y_ref, i)
        o_ref[0, i] = s[:n2] * scale
        o_ref[1, i] = s[n2:] * scale


def _fnet_mid_kernel(p_ref, q_ref, tw_ref, y_ref, cc_ref, sc_ref, o_ref, *, k1b, n2):
    for i in range(k1b):
        s = _stage2_fwd(p_ref, q_ref, tw_ref, y_ref, i)
        ur, ui = s[:n2].astype(BF16), s[n2:].astype(BF16)
        for g in range(N_FNET_GROUPS):
            lo = g * FNET_GROUP_DIM
            m = (_dot(ur[:, lo:lo + FNET_GROUP_DIM], cc_ref[...].astype(BF16))
                 + _dot(ui[:, lo:lo + FNET_GROUP_DIM], sc_ref[...].astype(BF16)))
            o_ref[0, :, i * D_FNET + lo:i * D_FNET + lo + FNET_GROUP_DIM] = m.astype(BF16)


def _stage2_specs(n1, n2, c, k1b):
    return [
        _resident((2 * n2, 2 * n2)),
        _resident((2 * n2, 2 * n2)),
        pl.BlockSpec((k1b, 2, 2 * n2), lambda k, b: (k, 0, 0)),
    ], pl.BlockSpec((1, 2, k1b, n2, c), lambda k, b: (b, 0, k, 0, 0))


def _fnet(a, b, l):
    n2 = _DFT_N2
    n1 = l // n2
    c = D_FNET
    t = _dft_tables(n1, n2)
    lhs = _const(t["rows_fwd"] / math.sqrt(l))
    y = _rowdft(lhs, a.reshape(b, n1, n2 * c), "fnet_rows")
    cc, sc = _cos_sin(FNET_GROUP_DIM)
    norm = 1.0 / math.sqrt(FNET_GROUP_DIM)
    k1b = 4
    head, y_spec = _stage2_specs(n1, n2, c, k1b)
    out = pl.pallas_call(
        functools.partial(_fnet_mid_kernel, k1b=k1b, n2=n2),
        grid=(n1 // k1b, b),
        in_specs=head + [y_spec,
                         _resident((FNET_GROUP_DIM, FNET_GROUP_DIM)),
                         _resident((FNET_GROUP_DIM, FNET_GROUP_DIM))],
        out_specs=pl.BlockSpec((1, n2, k1b * c), lambda k, bi: (bi, 0, k)),
        out_shape=jax.ShapeDtypeStruct((b, n2, n1 * c), BF16),
        compiler_params=_params("parallel", "parallel"),
        name="fnet_mid",
    )(_const(t["p"]), _const(t["q"]), _const(t["tw_k1"]),
      y.reshape(b, 2, n1, n2, c), _const(cc * norm), _const(sc * norm))
    return out.reshape(b * l, c)


def _shortconv_kernel(u_ref, hp_ref, hn_ref, w_ref, b_ref, vx_ref, x1_ref, *, r, nt):
    i = pl.program_id(1)
    row = lax.broadcasted_iota(jnp.int32, (r, 1), 0)
    hp = hp_ref[0].astype(F32)
    hn = hn_ref[0].astype(F32)
    outs = []
    for part in range(3):
        cols = slice(part * D_HYENA, (part + 1) * D_HYENA)
        u = u_ref[0, :, cols].astype(F32)
        prev_row = jnp.where(i == 0, 0.0, hp[_BF16_SUBLANES - 1:_BF16_SUBLANES, cols])
        next_row = jnp.where(i == nt - 1, 0.0, hn[0:1, cols])
        u_prev = jnp.where(row == 0, prev_row, pltpu.roll(u, 1, axis=0))
        u_next = jnp.where(row == r - 1, next_row, pltpu.roll(u, r - 1, axis=0))
        outs.append(u_prev * w_ref[0:1, cols] + u * w_ref[1:2, cols]
                    + u_next * w_ref[2:3, cols] + b_ref[:, cols])
    x1, x2, v = outs
    vx_ref[0] = (v * x2).astype(BF16)
    x1_ref[0] = x1.astype(BF16)


def _shortconv(u3, conv_w, conv_b):
    b, l, c3 = u3.shape
    r = 256
    nt = l // r
    hb = _BF16_SUBLANES
    halo = r // hb
    return pl.pallas_call(
        functools.partial(_shortconv_kernel, r=r, nt=nt),
        grid=(b, nt),
        in_specs=[
            pl.BlockSpec((1, r, c3), lambda bi, i: (bi, i, 0)),
            pl.BlockSpec((1, hb, c3), lambda bi, i: (bi, jnp.maximum(i * halo - 1, 0), 0)),
            pl.BlockSpec((1, hb, c3), lambda bi, i: (bi, jnp.minimum((i + 1) * halo, l // hb - 1), 0)),
            _resident((3, c3)),
            _resident((1, c3)),
        ],
        out_specs=[pl.BlockSpec((1, r, D_HYENA), lambda bi, i: (bi, i, 0))] * 2,
        out_shape=[jax.ShapeDtypeStruct((b, l, D_HYENA), BF16)] * 2,
        compiler_params=_params("parallel", "parallel"),
        name="shortconv",
    )(u3, u3, u3, conv_w, conv_b)


def _filter_kernel(bands_ref, w1t_ref, w1c_ref, w1s_ref, b1_ref, w2_ref, b2_ref,
                   w3_ref, b3_ref, w4_ref, fr_ref, deltas_ref, k_ref, ss_ref, *, r, l):
    i = pl.program_id(0)
    row = i * r + lax.broadcasted_iota(jnp.int32, (r, 1), 0)
    pos = jnp.where(row < l, row, 2 * l - row).astype(F32)
    t = pos * (1.0 / (l - 1))
    ang = ((2.0 * math.pi / l) * pos) * bands_ref[...]
    fr = fr_ref[...]
    pre = (t * w1t_ref[...] + _dot_hi(jnp.cos(ang), w1c_ref[...])
           - _dot_hi(jnp.sin(ang), w1s_ref[...]) + b1_ref[...])
    h = jnp.sin(fr * pre)
    h = jnp.sin(fr * (_dot_hi(h, w2_ref[...]) + b2_ref[...]))
    h = jnp.sin(fr * (_dot_hi(h, w3_ref[...]) + b3_ref[...]))
    k = _dot_hi(h, w4_ref[...]) * jnp.exp(-t * deltas_ref[...])
    k = jnp.where(row == l, 0.0, k)
    k_ref[...] = k

    @pl.when(i == 0)
    def _():
        ss_ref[...] = jnp.zeros_like(ss_ref)

    ss_ref[...] += jnp.sum(k * k, axis=0, keepdims=True)


def _filter_spectrum(l, w1, b1, w2, b2, w3, b3, w4, freq):
    n = 2 * l
    n2 = _DFT_N2
    n1 = n // n2
    c = D_HYENA
    r = 512
    nt = n // r
    bands = np.linspace(1e-4, FILTER_BANDS - 1, FILTER_BANDS)[None, :]
    deltas = np.abs(np.linspace(math.log(DECAY_TARGET) / FAST_DECAY_PCT,
                                math.log(DECAY_TARGET) / SLOW_DECAY_PCT, D_HYENA))[None, :]
    hdim = FILTER_HIDDEN
    small = [_const(bands), w1[0:1], w1[1:1 + FILTER_BANDS], w1[1 + FILTER_BANDS:],
             b1[None], w2, b2[None], w3, b3[None]]
    k, ss = pl.pallas_call(
        functools.partial(_filter_kernel, r=r, l=l),
        grid=(nt,),
        in_specs=[_resident(s.shape) for s in small] + [
            pl.BlockSpec((hdim, c), lambda i: (0, i // (nt // 2))),
            _resident((1, hdim)),
            _resident((1, c)),
        ],
        out_specs=[pl.BlockSpec((r, c), lambda i: (i, 0)),
                   pl.BlockSpec((1, c), lambda i: (0, 0))],
        out_shape=[jax.ShapeDtypeStruct((n, c), F32),
                   jax.ShapeDtypeStruct((1, c), F32)],
        compiler_params=_params("arbitrary"),
        name="hyena_filter",
    )(*small, w4, freq[None], _const(deltas))

    t = _dft_tables(n1, n2)
    y = _rowdft(_const(t["rows_fwd"]), k.reshape(1, n1, n2 * c), "filter_rows")
    k1b = 4
    head, y_spec = _stage2_specs(n1, n2, c, k1b)
    return pl.pallas_call(
        functools.partial(_filter_mid_kernel, k1b=k1b, n2=n2),
        grid=(n1 // k1b, 1),
        in_specs=head + [y_spec, _resident((1, c))],
        out_specs=pl.BlockSpec((2, k1b, n2, c), lambda k, bi: (0, k, 0, 0)),
        out_shape=jax.ShapeDtypeStruct((2, n1, n2, c), F32),
        compiler_params=_params("parallel", "parallel"),
        name="filter_mid",
    )(_const(t["p"]), _const(t["q"]), _const(t["tw_k1"]), y.reshape(1, 2, n1, n2, c), ss)


def _conv_out_kernel(pc_ref, qc_ref, tw_ref, z_ref, vx_ref, x1_ref, skip_ref, o_ref, *, nb, c):
    for i in range(nb):
        cols = slice(i * c, (i + 1) * c)
        lhs = (pc_ref[...] * tw_ref[i, 0:1, :] + qc_ref[...] * tw_ref[i, 1:2, :]).astype(BF16)
        y = _dot(lhs, z_ref[0, :, cols])
        y = y + vx_ref[0, :, cols].astype(F32) * skip_ref[...]
        o_ref[0, :, cols] = (y * x1_ref[0, :, cols].astype(F32)).astype(BF16)


def _longconv(vx, x1, kf, skip):
    b, l, c = vx.shape
    n = 2 * l
    n2 = _DFT_N2
    n1 = n // n2
    half = n1 // 2
    t = _dft_tables(n1, n2)
    y = _rowdft(_const(t["rows_fwd"][:, :half]), vx.reshape(b, half, n2 * c), "conv_rows")
    k1b = 4
    head, y_spec = _stage2_specs(n1, n2, c, k1b)
    z = pl.pallas_call(
        functools.partial(_conv_mid_kernel, k1b=k1b, n2=n2),
        grid=(n1 // k1b, b),
        in_specs=head + [_resident((2 * n2, 2 * n2)), y_spec,
                         pl.BlockSpec((2, k1b, n2, c), lambda k, bi: (0, k, 0, 0))],
        out_specs=y_spec,
        out_shape=jax.ShapeDtypeStruct((b, 2, n1, n2, c), BF16),
        compiler_params=_params("parallel", "arbitrary"),
        name="conv_mid",
    )(_const(t["p"]), _const(t["q"]), _const(t["tw_k1"]),
      _const(t["inv2"] / n), y.reshape(b, 2, n1, n2, c), kf)

    nb = 4
    seq_spec = pl.BlockSpec((1, half, nb * c), lambda bi, j: (bi, 0, j))
    out = pl.pallas_call(
        functools.partial(_conv_out_kernel, nb=nb, c=c),
        grid=(b, n2 // nb),
        in_specs=[_resident((half, 2 * n1)), _resident((half, 2 * n1)),
                  pl.BlockSpec((nb, 2, 2 * n1), lambda bi, j: (j, 0, 0)),
                  pl.BlockSpec((1, 2 * n1, nb * c), lambda bi, j: (bi, 0, j)),
                  seq_spec, seq_spec, _resident((1, c))],
        out_specs=seq_spec,
        out_shape=jax.ShapeDtypeStruct((b, half, n2 * c), BF16),
        compiler_params=_params("parallel", "parallel"),
        name="conv_out",
    )(_const(t["pc"][:half]), _const(t["qc"][:half]), _const(t["tw_n2"]),
      z.reshape(b, 2 * n1, n2 * c), vx.reshape(b, half, n2 * c),
      x1.reshape(b, half, n2 * c), skip)
    return out.reshape(b * l, c)


def _merge_kernel(mx_ref, yh_ref, ga_ref, gb_ref, x_ref, wf_ref, wh_ref, wo_ref, g2_ref,
                  x1_ref, h2_ref):
    ya = _dot(mx_ref[...], wf_ref[...])
    yb = _dot(yh_ref[...], wh_ref[...])
    m = ga_ref[...].astype(F32) * ya + gb_ref[...].astype(F32) * yb
    x1 = x_ref[...] + _dot(m.astype(BF16), wo_ref[...])
    x1_ref[...] = x1
    ms = jnp.mean(x1 * x1, axis=-1, keepdims=True)
    h2_ref[...] = (x1 * lax.rsqrt(ms + EPS) * g2_ref[...]).astype(BF16)


def _merge(mixed, yh, gates, x2, wf, wh, wo, g2):
    rows = x2.shape[0]
    tm = 256
    d = D_MODEL
    return pl.pallas_call(
        _merge_kernel,
        grid=(rows // tm,),
        in_specs=[
            pl.BlockSpec((tm, D_FNET), lambda i: (i, 0)),
            pl.BlockSpec((tm, D_HYENA), lambda i: (i, 0)),
            pl.BlockSpec((tm, d), lambda i: (i, 0)),
            pl.BlockSpec((tm, d), lambda i: (i, 1)),
            pl.BlockSpec((tm, d), lambda i: (i, 0)),
            _resident((D_FNET, d)), _resident((D_HYENA, d)), _resident((d, d)),
            _resident((1, d)),
        ],
        out_specs=[pl.BlockSpec((tm, d), lambda i: (i, 0))] * 2,
        out_shape=[jax.ShapeDtypeStruct((rows, d), F32),
                   jax.ShapeDtypeStruct((rows, d), BF16)],
        compiler_params=_params("parallel"),
        name="merge",
    )(mixed, yh, gates, gates, x2, wf, wh, wo, g2)


def _mlp_kernel(h2_ref, x1_ref, w1_ref, w2_ref, gf_ref, o_ref, acc_ref, *, nj):
    j = pl.program_id(1)
    t = _dot(h2_ref[...], w1_ref[...])
    t = jnp.square(jnp.maximum(t, 0.0)).astype(BF16)
    part = _dot(t, w2_ref[...])

    @pl.when(j == 0)
    def _():
        acc_ref[...] = x1_ref[...] + part

    @pl.when(j > 0)
    def _():
        acc_ref[...] += part

    @pl.when(j == nj - 1)
    def _():
        x = acc_ref[...]
        ms = jnp.mean(x * x, axis=-1, keepdims=True)
        o_ref[...] = x * lax.rsqrt(ms + EPS) * gf_ref[...]


def _mlp(h2, x1, w1, w2, gf):
    rows = x1.shape[0]
    tm, tf = 512, 1024
    d = D_MODEL
    nj = D_FF // tf
    return pl.pallas_call(
        functools.partial(_mlp_kernel, nj=nj),
        grid=(rows // tm, nj),
        in_specs=[
            pl.BlockSpec((tm, d), lambda i, j: (i, 0)),
            pl.BlockSpec((tm, d), lambda i, j: (i, 0)),
            pl.BlockSpec((d, tf), lambda i, j: (0, j)),
            pl.BlockSpec((tf, d), lambda i, j: (j, 0)),
            pl.BlockSpec((1, d), lambda i, j: (0, 0)),
        ],
        out_specs=pl.BlockSpec((tm, d), lambda i, j: (i, 0)),
        out_shape=jax.ShapeDtypeStruct((rows, d), F32),
        scratch_shapes=[pltpu.VMEM((tm, d), F32)],
        compiler_params=_params("parallel", "arbitrary"),
        name="mlp",
    )(h2, x1, w1, w2, gf)


def _layer_and_final_norm(x, w, filt):
    b, l, d = x.shape
    x2 = x.reshape(b * l, d)
    a, u, gates = _project(x2, w["norm1_g"], w["w_cat"], w["b_gate"])
    mixed = _fnet(a, b, l)
    vx, x1h = _shortconv(u.reshape(b, l, 3 * D_HYENA), w["conv_w"], w["conv_b"])
    kf = _filter_spectrum(l, *filt)
    yh = _longconv(vx, x1h, kf, w["skip"])
    x1, h2 = _merge(mixed, yh, gates, x2, w["w_fnet_map"], w["w_hyena_out"], w["w_out"],
                    w["norm2_g"])
    out = _mlp(h2, x1, w["w_mlp1"], w["w_mlp2"], w["norm_f_g"])
    return out.reshape(b, l, d)


def kernel(x_prompt, x_sample, norm1_g, w_in, conv_w, conv_b, filt_w1, filt_b1, filt_w2, filt_b2, filt_w3, filt_b3, filt_w4, filt_freq, hyena_skip, w_fnet_map, w_hyena_out, w_gate, b_gate, w_out, norm2_g, w_mlp1, w_mlp2, norm_f_g):
    assert norm1_g.shape[0] == 1, "one layer"
    w = {
        "norm1_g": norm1_g[0][None],
        "w_cat": jnp.concatenate([w_in[0], w_gate[0]], axis=1).astype(BF16),
        "b_gate": b_gate[0][None],
        "conv_w": conv_w[0],
        "conv_b": conv_b[0][None],
        "skip": hyena_skip[0][None],
        "w_fnet_map": w_fnet_map[0].astype(BF16),
        "w_hyena_out": w_hyena_out[0].astype(BF16),
        "w_out": w_out[0].astype(BF16),
        "norm2_g": norm2_g[0][None],
        "w_mlp1": w_mlp1[0].astype(BF16),
        "w_mlp2": w_mlp2[0].astype(BF16),
        "norm_f_g": norm_f_g[None],
    }
    filt = (filt_w1[0], filt_b1[0], filt_w2[0], filt_b2[0], filt_w3[0], filt_b3[0],
            filt_w4[0], filt_freq[0])
    return (_layer_and_final_norm(x_prompt, w, filt),
            _layer_and_final_norm(x_sample, w, filt))
```

```python
import functools
import math

import jax
import jax.numpy as jnp
import numpy as np
from jax import lax
from jax.experimental import pallas as pl
from jax.experimental.pallas import tpu as pltpu

D_MODEL = 2048
N_FNET_GROUPS = 4
FNET_GROUP_DIM = 256
D_FNET = N_FNET_GROUPS * FNET_GROUP_DIM
D_HYENA = 1024
FILTER_BANDS = 16
FILTER_HIDDEN = 64
DECAY_TARGET = 1e-2
FAST_DECAY_PCT = 0.3
SLOW_DECAY_PCT = 1.5
D_FF = 4 * D_MODEL
EPS = 1e-6

F32 = jnp.float32
BF16 = jnp.bfloat16

_VMEM_LIMIT_BYTES = 56 * 1024 * 1024
_DFT_N2 = 128
_BF16_SUBLANES = 16
_ROW_GROUP = _BF16_SUBLANES

_dot = functools.partial(jnp.dot, preferred_element_type=F32)
_dot_hi = functools.partial(jnp.dot, preferred_element_type=F32,
                            precision=lax.Precision.HIGHEST)


def _params(*sem):
    return pltpu.CompilerParams(dimension_semantics=sem,
                                vmem_limit_bytes=_VMEM_LIMIT_BYTES)


def _resident(shape):
    zeros = (0,) * len(shape)
    return pl.BlockSpec(shape, lambda *_: zeros, pipeline_mode=pl.Buffered(1))


def _cos_sin(n):
    jk = (np.arange(n)[:, None] * np.arange(n)[None, :]) % n
    ang = 2.0 * np.pi * jk / n
    return np.cos(ang), np.sin(ang)


@functools.lru_cache(maxsize=None)
def _dft_tables(n1, n2):
    n = n1 * n2
    c1, s1 = _cos_sin(n1)
    c2, s2 = _cos_sin(n2)
    fr1, fi1 = c1, -s1
    fr2, fi2 = c2, -s2
    ang = 2.0 * np.pi * ((np.arange(n1)[:, None] * np.arange(n2)[None, :]) % n) / n
    twr, twi = np.cos(ang), -np.sin(ang)
    t = {}
    h = n1 // 2
    t["rows_fwd"] = np.concatenate([fr1, fi1], axis=0)
    t["rows_fwd_c"] = np.block([[fr1[:, :h], -fi1[:, :h]], [fi1[:, :h], fr1[:, :h]]])
    t["p"] = np.block([[fr2, -fi2], [fi2, fr2]])
    t["q"] = np.block([[-fi2, -fr2], [fr2, -fi2]])
    t["tw_k1"] = np.stack([np.concatenate([twr, twr], axis=1),
                           np.concatenate([twi, twi], axis=1)], axis=1)
    t["inv2"] = np.block([[fr2, fi2], [-fi2, fr2]])
    t["pc"] = np.block([[fr1[:h], fi1[:h]], [-fi1[:h], fr1[:h]]])
    t["qc"] = np.block([[-fi1[:h], fr1[:h]], [-fr1[:h], -fi1[:h]]])
    t["tw_n2"] = np.stack([np.concatenate([twr.T, twr.T], axis=1),
                           np.concatenate([twi.T, twi.T], axis=1)], axis=1)
    return t


def _const(a, dtype=F32):
    return jnp.asarray(np.asarray(a, np.float32), dtype)


def _project_kernel(x_ref, g_ref, w_ref, b_ref, a_ref, u_ref, gate_ref, h_ref):
    j = pl.program_id(1)

    @pl.when(j == 0)
    def _():
        x = x_ref[...]
        ms = jnp.mean(x * x, axis=-1, keepdims=True)
        h_ref[...] = (x * lax.rsqrt(ms + EPS) * g_ref[...]).astype(BF16)

    acc = _dot(h_ref[...], w_ref[...])

    @pl.when(j == 0)
    def _():
        a_ref[...] = acc.astype(BF16)

    @pl.when(jnp.logical_and(j >= 1, j < 4))
    def _():
        u_ref[...] = acc.astype(BF16)

    @pl.when(j >= 4)
    def _():
        gate_ref[...] = jax.nn.sigmoid(acc + b_ref[...]).astype(BF16)


def _project(x2, g, w_cat, b_gate):
    rows = x2.shape[0]
    tm, tn = 1024, 1024
    grid = (rows // tm, w_cat.shape[1] // tn)
    return pl.pallas_call(
        _project_kernel,
        grid=grid,
        in_specs=[
            pl.BlockSpec((tm, D_MODEL), lambda i, j: (i, 0)),
            pl.BlockSpec((1, D_MODEL), lambda i, j: (0, 0)),
            pl.BlockSpec((D_MODEL, tn), lambda i, j: (0, j)),
            pl.BlockSpec((1, tn), lambda i, j: (0, jnp.maximum(j - 4, 0))),
        ],
        out_specs=[
            pl.BlockSpec((tm, tn), lambda i, j: (i, 0)),
            pl.BlockSpec((tm, tn), lambda i, j: (i, jnp.clip(j - 1, 0, 2))),
            pl.BlockSpec((tm, tn), lambda i, j: (i, jnp.maximum(j - 4, 0))),
        ],
        out_shape=[
            jax.ShapeDtypeStruct((rows, D_FNET), BF16),
            jax.ShapeDtypeStruct((rows, 3 * D_HYENA), BF16),
            jax.ShapeDtypeStruct((rows, 2 * D_MODEL), BF16),
        ],
        scratch_shapes=[pltpu.VMEM((tm, D_MODEL), BF16)],
        compiler_params=_params("parallel", "arbitrary"),
        name="project",
    )(x2, g, w_cat, b_gate)


def _swap_major(x):
    return jnp.transpose(x, (1, 0, 2))


def _rows_fwd_kernel(lhs_ref, x_ref, o_ref):
    _, parts, k, sb, cb = x_ref.shape
    x = x_ref[0].astype(BF16).reshape(parts * k, sb, cb)
    xt = _swap_major(x)
    lhs = lhs_ref[...].astype(BF16)
    y = jnp.stack([_dot(lhs, xt[s]).astype(BF16) for s in range(sb)], axis=0)
    o_ref[0] = _swap_major(y).reshape(o_ref.shape[1:])


def _rows_fwd(lhs, x5, name):
    b, parts, k, n2, c = x5.shape
    n1 = lhs.shape[0] // 2
    sb, cb = _ROW_GROUP, 512
    return pl.pallas_call(
        _rows_fwd_kernel,
        grid=(b, n2 // sb, c // cb),
        in_specs=[_resident(lhs.shape),
                  pl.BlockSpec((1, parts, k, sb, cb), lambda bi, g, ci: (bi, 0, 0, g, ci))],
        out_specs=pl.BlockSpec((1, 2, n1, sb, cb), lambda bi, g, ci: (bi, 0, 0, g, ci)),
        out_shape=jax.ShapeDtypeStruct((b, 2, n1, n2, c), BF16),
        compiler_params=_params("parallel", "parallel", "parallel"),
        name=name,
    )(lhs, x5)


def _stage2_fwd(p_ref, q_ref, tw_ref, y_ref, i):
    g = (p_ref[...] * tw_ref[i, 0:1, :] + q_ref[...] * tw_ref[i, 1:2, :]).astype(BF16)
    ys = jnp.concatenate([y_ref[0, 0, i], y_ref[0, 1, i]], axis=0)
    return _dot(g, ys)


def _conv_mid_kernel(p_ref, q_ref, tw_ref, inv_ref, y_ref, kf_ref, o_ref, *, k1b, n2):
    for i in range(k1b):
        s = _stage2_fwd(p_ref, q_ref, tw_ref, y_ref, i)
        xr, xi = s[:n2], s[n2:]
        kr, ki = kf_ref[0, i], kf_ref[1, i]
        prod = jnp.concatenate([xr * kr - xi * ki, xr * ki + xi * kr], axis=0)
        z = _dot(inv_ref[...].astype(BF16), prod.astype(BF16))
        o_ref[0, 0, i] = z[:n2].astype(BF16)
        o_ref[0, 1, i] = z[n2:].astype(BF16)


def _filter_mid_kernel(p_ref, q_ref, tw_ref, y_ref, ss_ref, o_ref, *, k1b, n2):
    scale = lax.rsqrt(ss_ref[...] + EPS)
    for i in range(k1b):
        s = _stage2_fwd(p_ref, q_ref, tw_ref, y_ref, i)
        o_ref[0, i] = s[:n2] * scale
        o_ref[1, i] = s[n2:] * scale


def _fnet_mid_kernel(p_ref, q_ref, tw_ref, y_ref, cc_ref, sc_ref, o_ref, *, k1b, n2):
    cb = o_ref.shape[-1]
    cc = cc_ref[...].astype(BF16)
    sc = sc_ref[...].astype(BF16)
    rows = []
    for i in range(k1b):
        s = _stage2_fwd(p_ref, q_ref, tw_ref, y_ref, i)
        ur, ui = s[:n2].astype(BF16), s[n2:].astype(BF16)
        groups = []
        for lo in range(0, cb, FNET_GROUP_DIM):
            hi = lo + FNET_GROUP_DIM
            groups.append(_dot(ur[:, lo:hi], cc) + _dot(ui[:, lo:hi], sc))
        rows.append(jnp.concatenate(groups, axis=1).astype(BF16))
    o_ref[0] = _swap_major(jnp.stack(rows, axis=0))


def _stage2_specs(n2, c, k1b):
    return [
        _resident((2 * n2, 2 * n2)),
        _resident((2 * n2, 2 * n2)),
        pl.BlockSpec((k1b, 2, 2 * n2), lambda k, b, *_: (k, 0, 0)),
    ], pl.BlockSpec((1, 2, k1b, n2, c), lambda k, b, *_: (b, 0, k, 0, 0))


def _fnet(a, b, l):
    n2 = _DFT_N2
    n1 = l // n2
    c = D_FNET
    t = _dft_tables(n1, n2)
    lhs = _const(t["rows_fwd"] / math.sqrt(l))
    y = _rows_fwd(lhs, a.reshape(b, 1, n1, n2, c), "fnet_rows")
    cc, sc = _cos_sin(FNET_GROUP_DIM)
    norm = 1.0 / math.sqrt(FNET_GROUP_DIM)
    k1b, cb = _ROW_GROUP, 2 * FNET_GROUP_DIM
    head, _ = _stage2_specs(n2, c, k1b)
    out = pl.pallas_call(
        functools.partial(_fnet_mid_kernel, k1b=k1b, n2=n2),
        grid=(n1 // k1b, b, c // cb),
        in_specs=head + [pl.BlockSpec((1, 2, k1b, n2, cb), lambda k, bi, ci: (bi, 0, k, 0, ci)),
                         _resident((FNET_GROUP_DIM, FNET_GROUP_DIM)),
                         _resident((FNET_GROUP_DIM, FNET_GROUP_DIM))],
        out_specs=pl.BlockSpec((1, n2, k1b, cb), lambda k, bi, ci: (bi, 0, k, ci)),
        out_shape=jax.ShapeDtypeStruct((b, n2, n1, c), BF16),
        compiler_params=_params("parallel", "parallel", "parallel"),
        name="fnet_mid",
    )(_const(t["p"]), _const(t["q"]), _const(t["tw_k1"]), y,
      _const(cc * norm), _const(sc * norm))
    return out.reshape(b * l, c)


def _shortconv_kernel(u_ref, hp_ref, hn_ref, w_ref, b_ref, vx_ref, x1_ref, *, r, nt):
    i = pl.program_id(1)
    row = lax.broadcasted_iota(jnp.int32, (r, 1), 0)
    hp = hp_ref[0].astype(F32)
    hn = hn_ref[0].astype(F32)
    outs = []
    for part in range(3):
        cols = slice(part * D_HYENA, (part + 1) * D_HYENA)
        u = u_ref[0, :, cols].astype(F32)
        prev_row = jnp.where(i == 0, 0.0, hp[_BF16_SUBLANES - 1:_BF16_SUBLANES, cols])
        next_row = jnp.where(i == nt - 1, 0.0, hn[0:1, cols])
        u_prev = jnp.where(row == 0, prev_row, pltpu.roll(u, 1, axis=0))
        u_next = jnp.where(row == r - 1, next_row, pltpu.roll(u, r - 1, axis=0))
        outs.append(u_prev * w_ref[0:1, cols] + u * w_ref[1:2, cols]
                    + u_next * w_ref[2:3, cols] + b_ref[:, cols])
    x1, x2, v = outs
    vx_ref[0] = (v * x2).astype(BF16)
    x1_ref[0] = x1.astype(BF16)


def _shortconv(u3, conv_w, conv_b):
    b, l, c3 = u3.shape
    r = 256
    nt = l // r
    hb = _BF16_SUBLANES
    halo = r // hb
    return pl.pallas_call(
        functools.partial(_shortconv_kernel, r=r, nt=nt),
        grid=(b, nt),
        in_specs=[
            pl.BlockSpec((1, r, c3), lambda bi, i: (bi, i, 0)),
            pl.BlockSpec((1, hb, c3), lambda bi, i: (bi, jnp.maximum(i * halo - 1, 0), 0)),
            pl.BlockSpec((1, hb, c3), lambda bi, i: (bi, jnp.minimum((i + 1) * halo, l // hb - 1), 0)),
            _resident((3, c3)),
            _resident((1, c3)),
        ],
        out_specs=[pl.BlockSpec((1, r, D_HYENA), lambda bi, i: (bi, i, 0))] * 2,
        out_shape=[jax.ShapeDtypeStruct((b, l, D_HYENA), BF16)] * 2,
        compiler_params=_params("parallel", "parallel"),
        name="shortconv",
    )(u3, u3, u3, conv_w, conv_b)


def _filter_kernel(bands_ref, w1t_ref, w1c_ref, w1s_ref, b1_ref, w2_ref, b2_ref,
                   w3_ref, b3_ref, w4_ref, fr_ref, deltas_ref, k_ref, ss_ref, *, r, l):
    i = pl.program_id(0)
    row = i * r + lax.broadcasted_iota(jnp.int32, (r, 1), 0)
    pos = jnp.where(row < l, row, 2 * l - row).astype(F32)
    t = pos * (1.0 / (l - 1))
    ang = ((2.0 * math.pi / l) * pos) * bands_ref[...]
    fr = fr_ref[...]
    pre = (t * w1t_ref[...] + _dot_hi(jnp.cos(ang), w1c_ref[...])
           - _dot_hi(jnp.sin(ang), w1s_ref[...]) + b1_ref[...])
    h = jnp.sin(fr * pre)
    h = jnp.sin(fr * (_dot_hi(h, w2_ref[...]) + b2_ref[...]))
    h = jnp.sin(fr * (_dot_hi(h, w3_ref[...]) + b3_ref[...]))
    k = _dot_hi(h, w4_ref[...]) * jnp.exp(-t * deltas_ref[...])
    k = jnp.where(row == l, 0.0, k)
    k_ref[...] = k

    @pl.when(i == 0)
    def _():
        ss_ref[...] = jnp.zeros_like(ss_ref)

    ss_ref[...] += jnp.sum(k * k, axis=0, keepdims=True)


def _filter_spectrum(l, w1, b1, w2, b2, w3, b3, w4, freq):
    n = 2 * l
    n2 = _DFT_N2
    n1 = n // n2
    c = D_HYENA
    r = 512
    nt = n // r
    bands = np.linspace(1e-4, FILTER_BANDS - 1, FILTER_BANDS)[None, :]
    deltas = np.abs(np.linspace(math.log(DECAY_TARGET) / FAST_DECAY_PCT,
                                math.log(DECAY_TARGET) / SLOW_DECAY_PCT, D_HYENA))[None, :]
    hdim = FILTER_HIDDEN
    small = [_const(bands), w1[0:1], w1[1:1 + FILTER_BANDS], w1[1 + FILTER_BANDS:],
             b1[None], w2, b2[None], w3, b3[None]]
    k, ss = pl.pallas_call(
        functools.partial(_filter_kernel, r=r, l=l),
        grid=(nt,),
        in_specs=[_resident(s.shape) for s in small] + [
            pl.BlockSpec((hdim, c), lambda i: (0, i // (nt // 2))),
            _resident((1, hdim)),
            _resident((1, c)),
        ],
        out_specs=[pl.BlockSpec((r, c), lambda i: (i, 0)),
                   pl.BlockSpec((1, c), lambda i: (0, 0))],
        out_shape=[jax.ShapeDtypeStruct((n, c), F32),
                   jax.ShapeDtypeStruct((1, c), F32)],
        compiler_params=_params("arbitrary"),
        name="hyena_filter",
    )(*small, w4, freq[None], _const(deltas))

    t = _dft_tables(n1, n2)
    y = _rows_fwd(_const(t["rows_fwd"]), k.reshape(1, 1, n1, n2, c), "filter_rows")
    k1b = 4
    head, y_spec = _stage2_specs(n2, c, k1b)
    return pl.pallas_call(
        functools.partial(_filter_mid_kernel, k1b=k1b, n2=n2),
        grid=(n1 // k1b, 1),
        in_specs=head + [y_spec, _resident((1, c))],
        out_specs=pl.BlockSpec((2, k1b, n2, c), lambda k, bi: (0, k, 0, 0)),
        out_shape=jax.ShapeDtypeStruct((2, n1, n2, c), F32),
        compiler_params=_params("parallel", "parallel"),
        name="filter_mid",
    )(_const(t["p"]), _const(t["q"]), _const(t["tw_k1"]), y, ss)


def _conv_out_kernel(pc_ref, qc_ref, tw_ref, z_ref, vx_ref, x1_ref, skip_ref, o_ref):
    _, _, n1, sb, cb = z_ref.shape
    z = _swap_major(z_ref[0].reshape(2 * n1, sb, cb))
    ys = []
    for s in range(sb):
        lhs = (pc_ref[...] * tw_ref[s, 0:1, :] + qc_ref[...] * tw_ref[s, 1:2, :]).astype(BF16)
        ys.append(_dot(lhs, z[s]))
    y = _swap_major(jnp.stack(ys, axis=0)).reshape(o_ref.shape)
    y = y + vx_ref[...].astype(F32) * skip_ref[...]
    o_ref[...] = (y * x1_ref[...].astype(F32)).astype(BF16)


def _longconv(vx, x1, kf, skip):
    b, l, c = vx.shape
    assert b == 2, "batch pair is packed as (re, im)"
    n = 2 * l
    n2 = _DFT_N2
    n1 = n // n2
    half = n1 // 2
    t = _dft_tables(n1, n2)
    y = _rows_fwd(_const(t["rows_fwd_c"]), vx.reshape(1, 2, half, n2, c), "conv_rows")
    k1b = 4
    head, y_spec = _stage2_specs(n2, c, k1b)
    z = pl.pallas_call(
        functools.partial(_conv_mid_kernel, k1b=k1b, n2=n2),
        grid=(n1 // k1b, 1),
        in_specs=head + [_resident((2 * n2, 2 * n2)), y_spec,
                         pl.BlockSpec((2, k1b, n2, c), lambda k, bi: (0, k, 0, 0))],
        out_specs=y_spec,
        out_shape=jax.ShapeDtypeStruct((1, 2, n1, n2, c), BF16),
        compiler_params=_params("parallel", "parallel"),
        name="conv_mid",
    )(_const(t["p"]), _const(t["q"]), _const(t["tw_k1"]), _const(t["inv2"] / n), y, kf)

    sb, cb = _ROW_GROUP, 512
    seq_spec = pl.BlockSpec((2, half, sb, cb), lambda g, ci: (0, 0, g, ci))
    out = pl.pallas_call(
        _conv_out_kernel,
        grid=(n2 // sb, c // cb),
        in_specs=[_resident((n1, 2 * n1)), _resident((n1, 2 * n1)),
                  pl.BlockSpec((sb, 2, 2 * n1), lambda g, ci: (g, 0, 0)),
                  pl.BlockSpec((1, 2, n1, sb, cb), lambda g, ci: (0, 0, 0, g, ci)),
                  seq_spec, seq_spec,
                  pl.BlockSpec((1, cb), lambda g, ci: (0, ci))],
        out_specs=seq_spec,
        out_shape=jax.ShapeDtypeStruct((2, half, n2, c), BF16),
        compiler_params=_params("parallel", "parallel"),
        name="conv_out",
    )(_const(t["pc"]), _const(t["qc"]), _const(t["tw_n2"]), z,
      vx.reshape(2, half, n2, c), x1.reshape(2, half, n2, c), skip)
    return out.reshape(b * l, c)


def _merge_kernel(mx_ref, yh_ref, ga_ref, gb_ref, x_ref, wf_ref, wh_ref, wo_ref, g2_ref,
                  x1_ref, h2_ref):
    ya = _dot(mx_ref[...], wf_ref[...])
    yb = _dot(yh_ref[...], wh_ref[...])
    m = ga_ref[...].astype(F32) * ya + gb_ref[...].astype(F32) * yb
    x1 = x_ref[...] + _dot(m.astype(BF16), wo_ref[...])
    x1_ref[...] = x1
    ms = jnp.mean(x1 * x1, axis=-1, keepdims=True)
    h2_ref[...] = (x1 * lax.rsqrt(ms + EPS) * g2_ref[...]).astype(BF16)


def _merge(mixed, yh, gates, x2, wf, wh, wo, g2):
    rows = x2.shape[0]
    tm = 256
    d = D_MODEL
    return pl.pallas_call(
        _merge_kernel,
        grid=(rows // tm,),
        in_specs=[
            pl.BlockSpec((tm, D_FNET), lambda i: (i, 0)),
            pl.BlockSpec((tm, D_HYENA), lambda i: (i, 0)),
            pl.BlockSpec((tm, d), lambda i: (i, 0)),
            pl.BlockSpec((tm, d), lambda i: (i, 1)),
            pl.BlockSpec((tm, d), lambda i: (i, 0)),
            _resident((D_FNET, d)), _resident((D_HYENA, d)), _resident((d, d)),
            _resident((1, d)),
        ],
        out_specs=[pl.BlockSpec((tm, d), lambda i: (i, 0))] * 2,
        out_shape=[jax.ShapeDtypeStruct((rows, d), F32),
                   jax.ShapeDtypeStruct((rows, d), BF16)],
        compiler_params=_params("parallel"),
        name="merge",
    )(mixed, yh, gates, gates, x2, wf, wh, wo, g2)


def _mlp_kernel(h2_ref, x1_ref, w1_ref, w2_ref, gf_ref, o_ref, acc_ref, *, nj):
    j = pl.program_id(1)
    t = _dot(h2_ref[...], w1_ref[...])
    t = jnp.square(jnp.maximum(t, 0.0)).astype(BF16)
    part = _dot(t, w2_ref[...])

    @pl.when(j == 0)
    def _():
        acc_ref[...] = x1_ref[...] + part

    @pl.when(j > 0)
    def _():
        acc_ref[...] += part

    @pl.when(j == nj - 1)
    def _():
        x = acc_ref[...]
        ms = jnp.mean(x * x, axis=-1, keepdims=True)
        o_ref[...] = x * lax.rsqrt(ms + EPS) * gf_ref[...]


def _mlp(h2, x1, w1, w2, gf):
    rows = x1.shape[0]
    tm, tf = 512, 1024
    d = D_MODEL
    nj = D_FF // tf
    return pl.pallas_call(
        functools.partial(_mlp_kernel, nj=nj),
        grid=(rows // tm, nj),
        in_specs=[
            pl.BlockSpec((tm, d), lambda i, j: (i, 0)),
            pl.BlockSpec((tm, d), lambda i, j: (i, 0)),
            pl.BlockSpec((d, tf), lambda i, j: (0, j)),
            pl.BlockSpec((tf, d), lambda i, j: (j, 0)),
            pl.BlockSpec((1, d), lambda i, j: (0, 0)),
        ],
        out_specs=pl.BlockSpec((tm, d), lambda i, j: (i, 0)),
        out_shape=jax.ShapeDtypeStruct((rows, d), F32),
        scratch_shapes=[pltpu.VMEM((tm, d), F32)],
        compiler_params=_params("parallel", "arbitrary"),
        name="mlp",
    )(h2, x1, w1, w2, gf)


def _layer_and_final_norm(x, w, filt):
    b, l, d = x.shape
    x2 = x.reshape(b * l, d)
    a, u, gates = _project(x2, w["norm1_g"], w["w_cat"], w["b_gate"])
    mixed = _fnet(a, b, l)
    vx, x1h = _shortconv(u.reshape(b, l, 3 * D_HYENA), w["conv_w"], w["conv_b"])
    kf = _filter_spectrum(l, *filt)
    yh = _longconv(vx, x1h, kf, w["skip"])
    x1, h2 = _merge(mixed, yh, gates, x2, w["w_fnet_map"], w["w_hyena_out"], w["w_out"],
                    w["norm2_g"])
    out = _mlp(h2, x1, w["w_mlp1"], w["w_mlp2"], w["norm_f_g"])
    return out.reshape(b, l, d)


def kernel(x_prompt, x_sample, norm1_g, w_in, conv_w, conv_b, filt_w1, filt_b1, filt_w2, filt_b2, filt_w3, filt_b3, filt_w4, filt_freq, hyena_skip, w_fnet_map, w_hyena_out, w_gate, b_gate, w_out, norm2_g, w_mlp1, w_mlp2, norm_f_g):
    assert norm1_g.shape[0] == 1, "one layer"
    w = {
        "norm1_g": norm1_g[0][None],
        "w_cat": jnp.concatenate([w_in[0], w_gate[0]], axis=1).astype(BF16),
        "b_gate": b_gate[0][None],
        "conv_w": conv_w[0],
        "conv_b": conv_b[0][None],
        "skip": hyena_skip[0][None],
        "w_fnet_map": w_fnet_map[0].astype(BF16),
        "w_hyena_out": w_hyena_out[0].astype(BF16),
        "w_out": w_out[0].astype(BF16),
        "norm2_g": norm2_g[0][None],
        "w_mlp1": w_mlp1[0].astype(BF16),
        "w_mlp2": w_mlp2[0].astype(BF16),
        "norm_f_g": norm_f_g[None],
    }
    filt = (filt_w1[0], filt_b1[0], filt_w2[0], filt_b2[0], filt_w3[0], filt_b3[0],
            filt_w4[0], filt_freq[0])
    return (_layer_and_final_norm(x_prompt, w, filt),
            _layer_and_final_norm(x_sample, w, filt))
```

```python
import functools
import math

import jax
import jax.numpy as jnp
import numpy as np
from jax import lax
from jax.experimental import pallas as pl
from jax.experimental.pallas import tpu as pltpu

D_MODEL = 2048
N_FNET_GROUPS = 4
FNET_GROUP_DIM = 256
D_FNET = N_FNET_GROUPS * FNET_GROUP_DIM
D_HYENA = 1024
FILTER_BANDS = 16
FILTER_HIDDEN = 64
DECAY_TARGET = 1e-2
FAST_DECAY_PCT = 0.3
SLOW_DECAY_PCT = 1.5
D_FF = 4 * D_MODEL
EPS = 1e-6

F32 = jnp.float32
BF16 = jnp.bfloat16

_VMEM_LIMIT_BYTES = 56 * 1024 * 1024
_DFT_N2 = 128
_BF16_SUBLANES = 16
_ROW_GROUP = _BF16_SUBLANES

_dot = functools.partial(jnp.dot, preferred_element_type=F32)
_dot_hi = functools.partial(jnp.dot, preferred_element_type=F32,
                            precision=lax.Precision.HIGHEST)


def _params(*sem):
    return pltpu.CompilerParams(dimension_semantics=sem,
                                vmem_limit_bytes=_VMEM_LIMIT_BYTES)


def _resident(shape):
    zeros = (0,) * len(shape)
    return pl.BlockSpec(shape, lambda *_: zeros, pipeline_mode=pl.Buffered(1))


def _cos_sin(n):
    jk = (np.arange(n)[:, None] * np.arange(n)[None, :]) % n
    ang = 2.0 * np.pi * jk / n
    return np.cos(ang), np.sin(ang)


@functools.lru_cache(maxsize=None)
def _dft_tables(n1, n2):
    n = n1 * n2
    c1, s1 = _cos_sin(n1)
    c2, s2 = _cos_sin(n2)
    fr1, fi1 = c1, -s1
    fr2, fi2 = c2, -s2
    ang = 2.0 * np.pi * ((np.arange(n1)[:, None] * np.arange(n2)[None, :]) % n) / n
    twr, twi = np.cos(ang), -np.sin(ang)
    t = {}
    h = n1 // 2
    t["rows_fwd"] = np.concatenate([fr1, fi1], axis=0)
    t["rows_fwd_c"] = np.block([[fr1[:, :h], -fi1[:, :h]], [fi1[:, :h], fr1[:, :h]]])
    t["p"] = np.block([[fr2, -fi2], [fi2, fr2]])
    t["q"] = np.block([[-fi2, -fr2], [fr2, -fi2]])
    t["tw_k1"] = np.stack([np.concatenate([twr, twr], axis=1),
                           np.concatenate([twi, twi], axis=1)], axis=1)
    t["inv2"] = np.block([[fr2, fi2], [-fi2, fr2]])
    t["pc"] = np.block([[fr1[:h], fi1[:h]], [-fi1[:h], fr1[:h]]])
    t["qc"] = np.block([[-fi1[:h], fr1[:h]], [-fr1[:h], -fi1[:h]]])
    t["tw_n2"] = np.stack([np.concatenate([twr.T, twr.T], axis=1),
                           np.concatenate([twi.T, twi.T], axis=1)], axis=1)
    return t


def _const(a, dtype=F32):
    return jnp.asarray(np.asarray(a, np.float32), dtype)


def _project_kernel(x_ref, g_ref, w_ref, b_ref, o_ref, h_ref, *, gate_tile, chunk):
    j = pl.program_id(1)

    @pl.when(j == 0)
    def _():
        x = x_ref[...]
        ms = jnp.mean(x * x, axis=-1, keepdims=True)
        h_ref[...] = (x * lax.rsqrt(ms + EPS) * g_ref[...]).astype(BF16)

    def emit(gated):
        for lo in range(0, o_ref.shape[1], chunk):
            acc = _dot(h_ref[...], w_ref[:, lo:lo + chunk])
            if gated:
                acc = jax.nn.sigmoid(acc + b_ref[:, lo:lo + chunk])
            o_ref[:, lo:lo + chunk] = acc.astype(BF16)

    @pl.when(j < gate_tile)
    def _():
        emit(False)

    @pl.when(j >= gate_tile)
    def _():
        emit(True)


def _project(x2, g, w_cat, b_gate):
    rows = x2.shape[0]
    n_out = w_cat.shape[1]
    tm, tn = 1024, 2048
    gate_tile = (n_out - b_gate.shape[1]) // tn
    return pl.pallas_call(
        functools.partial(_project_kernel, gate_tile=gate_tile, chunk=512),
        grid=(rows // tm, n_out // tn),
        in_specs=[
            pl.BlockSpec((tm, D_MODEL), lambda i, j: (i, 0)),
            pl.BlockSpec((1, D_MODEL), lambda i, j: (0, 0)),
            pl.BlockSpec((D_MODEL, tn), lambda i, j: (0, j)),
            pl.BlockSpec((1, tn), lambda i, j: (0, jnp.maximum(j - gate_tile, 0))),
        ],
        out_specs=pl.BlockSpec((tm, tn), lambda i, j: (i, j)),
        out_shape=jax.ShapeDtypeStruct((rows, n_out), BF16),
        scratch_shapes=[pltpu.VMEM((tm, D_MODEL), BF16)],
        compiler_params=_params("parallel", "arbitrary"),
        name="project",
    )(x2, g, w_cat, b_gate)


def _swap_major(x):
    return jnp.transpose(x, (1, 0, 2))


def _rows_fwd_kernel(lhs_ref, x_ref, o_ref):
    _, parts, k, sb, cb = x_ref.shape
    x = x_ref[0].astype(BF16).reshape(parts * k, sb, cb)
    xt = _swap_major(x)
    lhs = lhs_ref[...].astype(BF16)
    y = jnp.stack([_dot(lhs, xt[s]).astype(BF16) for s in range(sb)], axis=0)
    o_ref[0] = _swap_major(y).reshape(o_ref.shape[1:])


def _rows_fwd(lhs, x5, name, c=None):
    b, parts, k, n2, _ = x5.shape
    c = c or x5.shape[-1]
    n1 = lhs.shape[0] // 2
    sb, cb = _ROW_GROUP, 512
    return pl.pallas_call(
        _rows_fwd_kernel,
        grid=(b, n2 // sb, c // cb),
        in_specs=[_resident(lhs.shape),
                  pl.BlockSpec((1, parts, k, sb, cb), lambda bi, g, ci: (bi, 0, 0, g, ci))],
        out_specs=pl.BlockSpec((1, 2, n1, sb, cb), lambda bi, g, ci: (bi, 0, 0, g, ci)),
        out_shape=jax.ShapeDtypeStruct((b, 2, n1, n2, c), BF16),
        compiler_params=_params("parallel", "parallel", "parallel"),
        name=name,
    )(lhs, x5)


def _stage2_fwd(p_ref, q_ref, tw_ref, y_ref, i):
    g = (p_ref[...] * tw_ref[i, 0:1, :] + q_ref[...] * tw_ref[i, 1:2, :]).astype(BF16)
    ys = jnp.concatenate([y_ref[0, 0, i], y_ref[0, 1, i]], axis=0)
    return _dot(g, ys)


def _conv_mid_kernel(p_ref, q_ref, tw_ref, inv_ref, y_ref, kf_ref, o_ref, *, k1b, n2):
    for i in range(k1b):
        s = _stage2_fwd(p_ref, q_ref, tw_ref, y_ref, i)
        xr, xi = s[:n2], s[n2:]
        kr, ki = kf_ref[0, i], kf_ref[1, i]
        prod = jnp.concatenate([xr * kr - xi * ki, xr * ki + xi * kr], axis=0)
        z = _dot(inv_ref[...].astype(BF16), prod.astype(BF16))
        o_ref[0, 0, i] = z[:n2].astype(BF16)
        o_ref[0, 1, i] = z[n2:].astype(BF16)


def _filter_mid_kernel(p_ref, q_ref, tw_ref, y_ref, ss_ref, o_ref, *, k1b, n2):
    scale = lax.rsqrt(ss_ref[...] + EPS)
    for i in range(k1b):
        s = _stage2_fwd(p_ref, q_ref, tw_ref, y_ref, i)
        o_ref[0, i] = s[:n2] * scale
        o_ref[1, i] = s[n2:] * scale


def _fnet_mid_kernel(p_ref, q_ref, tw_ref, y_ref, cc_ref, sc_ref, o_ref, *, k1b, n2):
    cb = o_ref.shape[-1]
    cc = cc_ref[...].astype(BF16)
    sc = sc_ref[...].astype(BF16)
    rows = []
    for i in range(k1b):
        s = _stage2_fwd(p_ref, q_ref, tw_ref, y_ref, i)
        ur, ui = s[:n2].astype(BF16), s[n2:].astype(BF16)
        groups = []
        for lo in range(0, cb, FNET_GROUP_DIM):
            hi = lo + FNET_GROUP_DIM
            groups.append(_dot(ur[:, lo:hi], cc) + _dot(ui[:, lo:hi], sc))
        rows.append(jnp.concatenate(groups, axis=1).astype(BF16))
    o_ref[0] = _swap_major(jnp.stack(rows, axis=0))


def _stage2_specs(n2, c, k1b):
    return [
        _resident((2 * n2, 2 * n2)),
        _resident((2 * n2, 2 * n2)),
        pl.BlockSpec((k1b, 2, 2 * n2), lambda k, b, *_: (k, 0, 0)),
    ], pl.BlockSpec((1, 2, k1b, n2, c), lambda k, b, *_: (b, 0, k, 0, 0))


def _fnet(proj, b, l):
    n2 = _DFT_N2
    n1 = l // n2
    c = D_FNET
    t = _dft_tables(n1, n2)
    lhs = _const(t["rows_fwd"] / math.sqrt(l))
    y = _rows_fwd(lhs, proj.reshape(b, 1, n1, n2, proj.shape[-1]), "fnet_rows", c)
    cc, sc = _cos_sin(FNET_GROUP_DIM)
    norm = 1.0 / math.sqrt(FNET_GROUP_DIM)
    k1b, cb = _ROW_GROUP, 2 * FNET_GROUP_DIM
    head, _ = _stage2_specs(n2, c, k1b)
    out = pl.pallas_call(
        functools.partial(_fnet_mid_kernel, k1b=k1b, n2=n2),
        grid=(n1 // k1b, b, c // cb),
        in_specs=head + [pl.BlockSpec((1, 2, k1b, n2, cb), lambda k, bi, ci: (bi, 0, k, 0, ci)),
                         _resident((FNET_GROUP_DIM, FNET_GROUP_DIM)),
                         _resident((FNET_GROUP_DIM, FNET_GROUP_DIM))],
        out_specs=pl.BlockSpec((1, n2, k1b, cb), lambda k, bi, ci: (bi, 0, k, ci)),
        out_shape=jax.ShapeDtypeStruct((b, n2, n1, c), BF16),
        compiler_params=_params("parallel", "parallel", "parallel"),
        name="fnet_mid",
    )(_const(t["p"]), _const(t["q"]), _const(t["tw_k1"]), y,
      _const(cc * norm), _const(sc * norm))
    return out.reshape(b * l, c)


def _shortconv_kernel(*refs, r, nt):
    u_refs, prev_refs, next_refs = refs[0:3], refs[3:6], refs[6:9]
    w_ref, b_ref, vx_ref, x1_ref = refs[9:]
    i = pl.program_id(1)
    row = lax.broadcasted_iota(jnp.int32, (r, 1), 0)
    outs = []
    for part in range(3):
        cols = slice(part * D_HYENA, (part + 1) * D_HYENA)
        u = u_refs[part][0].astype(F32)
        prev_row = prev_refs[part][0].astype(F32)[_BF16_SUBLANES - 1:_BF16_SUBLANES]
        next_row = next_refs[part][0].astype(F32)[0:1]
        prev_row = jnp.where(i == 0, 0.0, prev_row)
        next_row = jnp.where(i == nt - 1, 0.0, next_row)
        u_prev = jnp.where(row == 0, prev_row, pltpu.roll(u, 1, axis=0))
        u_next = jnp.where(row == r - 1, next_row, pltpu.roll(u, r - 1, axis=0))
        outs.append(u_prev * w_ref[0:1, cols] + u * w_ref[1:2, cols]
                    + u_next * w_ref[2:3, cols] + b_ref[:, cols])
    x1, x2, v = outs
    vx_ref[0] = (v * x2).astype(BF16)
    x1_ref[0] = x1.astype(BF16)


def _shortconv(proj3, conv_w, conv_b):
    b, l, _ = proj3.shape
    c = D_HYENA
    r = 256
    nt = l // r
    hb = _BF16_SUBLANES
    halo = r // hb
    first_block = D_FNET // c
    tiles = [pl.BlockSpec((1, r, c), lambda bi, i, p=p: (bi, i, first_block + p))
             for p in range(3)]
    prevs = [pl.BlockSpec((1, hb, c),
                          lambda bi, i, p=p: (bi, jnp.maximum(i * halo - 1, 0), first_block + p))
             for p in range(3)]
    nexts = [pl.BlockSpec((1, hb, c),
                          lambda bi, i, p=p: (bi, jnp.minimum((i + 1) * halo, l // hb - 1),
                                              first_block + p))
             for p in range(3)]
    return pl.pallas_call(
        functools.partial(_shortconv_kernel, r=r, nt=nt),
        grid=(b, nt),
        in_specs=tiles + prevs + nexts + [_resident((3, 3 * c)), _resident((1, 3 * c))],
        out_specs=[pl.BlockSpec((1, r, c), lambda bi, i: (bi, i, 0))] * 2,
        out_shape=[jax.ShapeDtypeStruct((b, l, c), BF16)] * 2,
        compiler_params=_params("parallel", "parallel"),
        name="shortconv",
    )(*([proj3] * 9), conv_w, conv_b)


def _filter_kernel(bands_ref, w1t_ref, w1c_ref, w1s_ref, b1_ref, w2_ref, b2_ref,
                   w3_ref, b3_ref, w4_ref, fr_ref, deltas_ref, k_ref, ss_ref, *, r, l):
    i = pl.program_id(0)
    row = i * r + lax.broadcasted_iota(jnp.int32, (r, 1), 0)
    pos = jnp.where(row < l, row, 2 * l - row).astype(F32)
    t = pos * (1.0 / (l - 1))
    ang = ((2.0 * math.pi / l) * pos) * bands_ref[...]
    fr = fr_ref[...]
    pre = (t * w1t_ref[...] + _dot_hi(jnp.cos(ang), w1c_ref[...])
           - _dot_hi(jnp.sin(ang), w1s_ref[...]) + b1_ref[...])
    h = jnp.sin(fr * pre)
    h = jnp.sin(fr * (_dot_hi(h, w2_ref[...]) + b2_ref[...]))
    h = jnp.sin(fr * (_dot_hi(h, w3_ref[...]) + b3_ref[...]))
    k = _dot_hi(h, w4_ref[...]) * jnp.exp(-t * deltas_ref[...])
    k = jnp.where(row == l, 0.0, k)
    k_ref[...] = k

    @pl.when(i == 0)
    def _():
        ss_ref[...] = jnp.zeros_like(ss_ref)

    ss_ref[...] += jnp.sum(k * k, axis=0, keepdims=True)


def _filter_spectrum(l, w1, b1, w2, b2, w3, b3, w4, freq):
    n = 2 * l
    n2 = _DFT_N2
    n1 = n // n2
    c = D_HYENA
    r = 512
    nt = n // r
    bands = np.linspace(1e-4, FILTER_BANDS - 1, FILTER_BANDS)[None, :]
    deltas = np.abs(np.linspace(math.log(DECAY_TARGET) / FAST_DECAY_PCT,
                                math.log(DECAY_TARGET) / SLOW_DECAY_PCT, D_HYENA))[None, :]
    hdim = FILTER_HIDDEN
    small = [_const(bands), w1[0:1], w1[1:1 + FILTER_BANDS], w1[1 + FILTER_BANDS:],
             b1[None], w2, b2[None], w3, b3[None]]
    k, ss = pl.pallas_call(
        functools.partial(_filter_kernel, r=r, l=l),
        grid=(nt,),
        in_specs=[_resident(s.shape) for s in small] + [
            pl.BlockSpec((hdim, c), lambda i: (0, i // (nt // 2))),
            _resident((1, hdim)),
            _resident((1, c)),
        ],
        out_specs=[pl.BlockSpec((r, c), lambda i: (i, 0)),
                   pl.BlockSpec((1, c), lambda i: (0, 0))],
        out_shape=[jax.ShapeDtypeStruct((n, c), F32),
                   jax.ShapeDtypeStruct((1, c), F32)],
        compiler_params=_params("arbitrary"),
        name="hyena_filter",
    )(*small, w4, freq[None], _const(deltas))

    t = _dft_tables(n1, n2)
    y = _rows_fwd(_const(t["rows_fwd"]), k.reshape(1, 1, n1, n2, c), "filter_rows")
    k1b = 4
    head, y_spec = _stage2_specs(n2, c, k1b)
    return pl.pallas_call(
        functools.partial(_filter_mid_kernel, k1b=k1b, n2=n2),
        grid=(n1 // k1b, 1),
        in_specs=head + [y_spec, _resident((1, c))],
        out_specs=pl.BlockSpec((2, k1b, n2, c), lambda k, bi: (0, k, 0, 0)),
        out_shape=jax.ShapeDtypeStruct((2, n1, n2, c), F32),
        compiler_params=_params("parallel", "parallel"),
        name="filter_mid",
    )(_const(t["p"]), _const(t["q"]), _const(t["tw_k1"]), y, ss)


def _conv_out_kernel(pc_ref, qc_ref, tw_ref, z_ref, vx_ref, x1_ref, skip_ref, o_ref):
    _, _, n1, sb, cb = z_ref.shape
    z = _swap_major(z_ref[0].reshape(2 * n1, sb, cb))
    ys = []
    for s in range(sb):
        lhs = (pc_ref[...] * tw_ref[s, 0:1, :] + qc_ref[...] * tw_ref[s, 1:2, :]).astype(BF16)
        ys.append(_dot(lhs, z[s]))
    y = _swap_major(jnp.stack(ys, axis=0)).reshape(o_ref.shape)
    y = y + vx_ref[...].astype(F32) * skip_ref[...]
    o_ref[...] = (y * x1_ref[...].astype(F32)).astype(BF16)


def _longconv(vx, x1, kf, skip):
    b, l, c = vx.shape
    assert b == 2, "batch pair is packed as (re, im)"
    n = 2 * l
    n2 = _DFT_N2
    n1 = n // n2
    half = n1 // 2
    t = _dft_tables(n1, n2)
    y = _rows_fwd(_const(t["rows_fwd_c"]), vx.reshape(1, 2, half, n2, c), "conv_rows")
    k1b = 4
    head, y_spec = _stage2_specs(n2, c, k1b)
    z = pl.pallas_call(
        functools.partial(_conv_mid_kernel, k1b=k1b, n2=n2),
        grid=(n1 // k1b, 1),
        in_specs=head + [_resident((2 * n2, 2 * n2)), y_spec,
                         pl.BlockSpec((2, k1b, n2, c), lambda k, bi: (0, k, 0, 0))],
        out_specs=y_spec,
        out_shape=jax.ShapeDtypeStruct((1, 2, n1, n2, c), BF16),
        compiler_params=_params("parallel", "parallel"),
        name="conv_mid",
    )(_const(t["p"]), _const(t["q"]), _const(t["tw_k1"]), _const(t["inv2"] / n), y, kf)

    sb, cb = _ROW_GROUP, 512
    seq_spec = pl.BlockSpec((2, half, sb, cb), lambda g, ci: (0, 0, g, ci))
    out = pl.pallas_call(
        _conv_out_kernel,
        grid=(n2 // sb, c // cb),
        in_specs=[_resident((n1, 2 * n1)), _resident((n1, 2 * n1)),
                  pl.BlockSpec((sb, 2, 2 * n1), lambda g, ci: (g, 0, 0)),
                  pl.BlockSpec((1, 2, n1, sb, cb), lambda g, ci: (0, 0, 0, g, ci)),
                  seq_spec, seq_spec,
                  pl.BlockSpec((1, cb), lambda g, ci: (0, ci))],
        out_specs=seq_spec,
        out_shape=jax.ShapeDtypeStruct((2, half, n2, c), BF16),
        compiler_params=_params("parallel", "parallel"),
        name="conv_out",
    )(_const(t["pc"]), _const(t["qc"]), _const(t["tw_n2"]), z,
      vx.reshape(2, half, n2, c), x1.reshape(2, half, n2, c), skip)
    return out.reshape(b * l, c)


def _merge_kernel(mx_ref, yh_ref, ga_ref, gb_ref, x_ref, wf_ref, wh_ref, wo_ref, g2_ref,
                  x1_ref, h2_ref, m_ref, *, chunk):
    d = x_ref.shape[1]
    for lo in range(0, d, chunk):
        cols = slice(lo, lo + chunk)
        ya = _dot(mx_ref[...], wf_ref[:, cols])
        yb = _dot(yh_ref[...], wh_ref[:, cols])
        m_ref[:, cols] = (ga_ref[:, cols].astype(F32) * ya
                          + gb_ref[:, cols].astype(F32) * yb).astype(BF16)
    ss = jnp.zeros((x_ref.shape[0], 1), F32)
    for lo in range(0, d, chunk):
        cols = slice(lo, lo + chunk)
        x1 = x_ref[:, cols] + _dot(m_ref[...], wo_ref[:, cols])
        x1_ref[:, cols] = x1
        ss = ss + jnp.sum(x1 * x1, axis=-1, keepdims=True)
    scale = lax.rsqrt(ss * (1.0 / d) + EPS)
    for lo in range(0, d, chunk):
        cols = slice(lo, lo + chunk)
        h2_ref[:, cols] = (x1_ref[:, cols] * scale * g2_ref[:, cols]).astype(BF16)


def _merge(mixed, yh, proj, x2, wf, wh, wo, g2):
    rows = x2.shape[0]
    tm = 512
    d = D_MODEL
    gate_block = (proj.shape[1] - 2 * d) // d
    return pl.pallas_call(
        functools.partial(_merge_kernel, chunk=512),
        grid=(rows // tm,),
        in_specs=[
            pl.BlockSpec((tm, D_FNET), lambda i: (i, 0)),
            pl.BlockSpec((tm, D_HYENA), lambda i: (i, 0)),
            pl.BlockSpec((tm, d), lambda i: (i, gate_block)),
            pl.BlockSpec((tm, d), lambda i: (i, gate_block + 1)),
            pl.BlockSpec((tm, d), lambda i: (i, 0)),
            _resident((D_FNET, d)), _resident((D_HYENA, d)), _resident((d, d)),
            _resident((1, d)),
        ],
        out_specs=[pl.BlockSpec((tm, d), lambda i: (i, 0))] * 2,
        out_shape=[jax.ShapeDtypeStruct((rows, d), F32),
                   jax.ShapeDtypeStruct((rows, d), BF16)],
        scratch_shapes=[pltpu.VMEM((tm, d), BF16)],
        compiler_params=_params("parallel"),
        name="merge",
    )(mixed, yh, proj, proj, x2, wf, wh, wo, g2)


def _mlp_kernel(h2_ref, x1_ref, w1_ref, w2_ref, gf_ref, o_ref, *, nj):
    j = pl.program_id(1)

    @pl.when(j == 0)
    def _():
        o_ref[...] = x1_ref[...]

    t = _dot(h2_ref[...], w1_ref[...])
    t = jnp.square(jnp.maximum(t, 0.0)).astype(BF16)
    o_ref[...] += _dot(t, w2_ref[...])

    @pl.when(j == nj - 1)
    def _():
        x = o_ref[...]
        ms = jnp.mean(x * x, axis=-1, keepdims=True)
        o_ref[...] = x * lax.rsqrt(ms + EPS) * gf_ref[...]


def _mlp(h2, x1, w1, w2, gf):
    rows = x1.shape[0]
    tm, tf = 512, 1024
    d = D_MODEL
    nj = D_FF // tf
    return pl.pallas_call(
        functools.partial(_mlp_kernel, nj=nj),
        grid=(rows // tm, nj),
        in_specs=[
            pl.BlockSpec((tm, d), lambda i, j: (i, 0)),
            pl.BlockSpec((tm, d), lambda i, j: (i, 0)),
            pl.BlockSpec((d, tf), lambda i, j: (0, j)),
            pl.BlockSpec((tf, d), lambda i, j: (j, 0)),
            pl.BlockSpec((1, d), lambda i, j: (0, 0)),
        ],
        out_specs=pl.BlockSpec((tm, d), lambda i, j: (i, 0)),
        out_shape=jax.ShapeDtypeStruct((rows, d), F32),
        compiler_params=_params("parallel", "arbitrary"),
        name="mlp",
    )(h2, x1, w1, w2, gf)


def _layer_and_final_norm(x, w, filt):
    b, l, d = x.shape
    x2 = x.reshape(b * l, d)
    proj = _project(x2, w["norm1_g"], w["w_cat"], w["b_gate"])
    mixed = _fnet(proj, b, l)
    vx, x1h = _shortconv(proj.reshape(b, l, proj.shape[-1]), w["conv_w"], w["conv_b"])
    kf = _filter_spectrum(l, *filt)
    yh = _longconv(vx, x1h, kf, w["skip"])
    x1, h2 = _merge(mixed, yh, proj, x2, w["w_fnet_map"], w["w_hyena_out"], w["w_out"],
                    w["norm2_g"])
    out = _mlp(h2, x1, w["w_mlp1"], w["w_mlp2"], w["norm_f_g"])
    return out.reshape(b, l, d)


def kernel(x_prompt, x_sample, norm1_g, w_in, conv_w, conv_b, filt_w1, filt_b1, filt_w2, filt_b2, filt_w3, filt_b3, filt_w4, filt_freq, hyena_skip, w_fnet_map, w_hyena_out, w_gate, b_gate, w_out, norm2_g, w_mlp1, w_mlp2, norm_f_g):
    assert norm1_g.shape[0] == 1, "one layer"
    w = {
        "norm1_g": norm1_g[0][None],
        "w_cat": jnp.concatenate([w_in[0], w_gate[0]], axis=1).astype(BF16),
        "b_gate": b_gate[0][None],
        "conv_w": conv_w[0],
        "conv_b": conv_b[0][None],
        "skip": hyena_skip[0][None],
        "w_fnet_map": w_fnet_map[0].astype(BF16),
        "w_hyena_out": w_hyena_out[0].astype(BF16),
        "w_out": w_out[0].astype(BF16),
        "norm2_g": norm2_g[0][None],
        "w_mlp1": w_mlp1[0].astype(BF16),
        "w_mlp2": w_mlp2[0].astype(BF16),
        "norm_f_g": norm_f_g[None],
    }
    filt = (filt_w1[0], filt_b1[0], filt_w2[0], filt_b2[0], filt_w3[0], filt_b3[0],
            filt_w4[0], filt_freq[0])
    return (_layer_and_final_norm(x_prompt, w, filt),
            _layer_and_final_norm(x_sample, w, filt))
```

```python
import functools
import math

import jax
import jax.numpy as jnp
import numpy as np
from jax import lax
from jax.experimental import pallas as pl
from jax.experimental.pallas import tpu as pltpu

D_MODEL = 2048
N_FNET_GROUPS = 4
FNET_GROUP_DIM = 256
D_FNET = N_FNET_GROUPS * FNET_GROUP_DIM
D_HYENA = 1024
FILTER_BANDS = 16
FILTER_HIDDEN = 64
DECAY_TARGET = 1e-2
FAST_DECAY_PCT = 0.3
SLOW_DECAY_PCT = 1.5
D_FF = 4 * D_MODEL
EPS = 1e-6

F32 = jnp.float32
BF16 = jnp.bfloat16

_VMEM_LIMIT_BYTES = 56 * 1024 * 1024
_DFT_N2 = 128
_BF16_SUBLANES = 16
_ROW_GROUP = _BF16_SUBLANES

_dot = functools.partial(jnp.dot, preferred_element_type=F32)
_dot_hi = functools.partial(jnp.dot, preferred_element_type=F32,
                            precision=lax.Precision.HIGHEST)


def _dot_tn_bf16x3(a, b):
    dims = (((0,), (0,)), ((), ()))
    a_hi, b_hi = a.astype(BF16), b.astype(BF16)
    a_lo = (a - a_hi.astype(F32)).astype(BF16)
    b_lo = (b - b_hi.astype(F32)).astype(BF16)
    dg = functools.partial(lax.dot_general, dimension_numbers=dims, preferred_element_type=F32)
    return dg(a_hi, b_hi) + dg(a_lo, b_hi) + dg(a_hi, b_lo)


def _params(*sem):
    return pltpu.CompilerParams(dimension_semantics=sem,
                                vmem_limit_bytes=_VMEM_LIMIT_BYTES)


def _resident(shape):
    zeros = (0,) * len(shape)
    return pl.BlockSpec(shape, lambda *_: zeros, pipeline_mode=pl.Buffered(1))


def _cos_sin(n):
    jk = (np.arange(n)[:, None] * np.arange(n)[None, :]) % n
    ang = 2.0 * np.pi * jk / n
    return np.cos(ang), np.sin(ang)


@functools.lru_cache(maxsize=None)
def _dft_tables(n1, n2):
    n = n1 * n2
    c1, s1 = _cos_sin(n1)
    c2, s2 = _cos_sin(n2)
    fr1, fi1 = c1, -s1
    fr2, fi2 = c2, -s2
    ang = 2.0 * np.pi * ((np.arange(n1)[:, None] * np.arange(n2)[None, :]) % n) / n
    twr, twi = np.cos(ang), -np.sin(ang)
    t = {}
    h = n1 // 2
    t["rows_fwd"] = np.concatenate([fr1, fi1], axis=0)
    t["rows_fwd_c"] = np.block([[fr1[:, :h], -fi1[:, :h]], [fi1[:, :h], fr1[:, :h]]])
    t["p"] = np.block([[fr2, -fi2], [fi2, fr2]])
    t["q"] = np.block([[-fi2, -fr2], [fr2, -fi2]])
    t["tw_k1"] = np.stack([np.concatenate([twr, twr], axis=1),
                           np.concatenate([twi, twi], axis=1)], axis=1)
    t["p_conj"] = np.block([[fr2, fi2], [-fi2, fr2]])
    t["q_conj"] = np.block([[-fi2, fr2], [-fr2, -fi2]])
    t["inv2"] = np.block([[fr2, fi2], [-fi2, fr2]])
    t["pc"] = np.block([[fr1[:h], fi1[:h]], [-fi1[:h], fr1[:h]]])
    t["qc"] = np.block([[-fi1[:h], fr1[:h]], [-fr1[:h], -fi1[:h]]])
    t["tw_n2"] = np.stack([np.concatenate([twr.T, twr.T], axis=1),
                           np.concatenate([twi.T, twi.T], axis=1)], axis=1)
    return t


def _const(a, dtype=F32):
    return jnp.asarray(np.asarray(a, np.float32), dtype)


def _project_kernel(x_ref, g_ref, w_ref, b_ref, o_ref, h_ref, *, gate_tile, chunk):
    j = pl.program_id(1)

    @pl.when(j == 0)
    def _():
        x = x_ref[...]
        ms = jnp.mean(x * x, axis=-1, keepdims=True)
        h_ref[...] = (x * lax.rsqrt(ms + EPS) * g_ref[...]).astype(BF16)

    def emit(gated):
        for lo in range(0, o_ref.shape[1], chunk):
            acc = _dot(h_ref[...], w_ref[:, lo:lo + chunk])
            if gated:
                acc = jax.nn.sigmoid(acc + b_ref[:, lo:lo + chunk])
            o_ref[:, lo:lo + chunk] = acc.astype(BF16)

    @pl.when(j < gate_tile)
    def _():
        emit(False)

    @pl.when(j >= gate_tile)
    def _():
        emit(True)


def _project(x2, g, w_cat, b_gate):
    rows = x2.shape[0]
    n_out = w_cat.shape[1]
    tm, tn = 1024, 2048
    gate_tile = (n_out - b_gate.shape[1]) // tn
    return pl.pallas_call(
        functools.partial(_project_kernel, gate_tile=gate_tile, chunk=512),
        grid=(rows // tm, n_out // tn),
        in_specs=[
            pl.BlockSpec((tm, D_MODEL), lambda i, j: (i, 0)),
            pl.BlockSpec((1, D_MODEL), lambda i, j: (0, 0)),
            pl.BlockSpec((D_MODEL, tn), lambda i, j: (0, j)),
            pl.BlockSpec((1, tn), lambda i, j: (0, jnp.maximum(j - gate_tile, 0))),
        ],
        out_specs=pl.BlockSpec((tm, tn), lambda i, j: (i, j)),
        out_shape=jax.ShapeDtypeStruct((rows, n_out), BF16),
        scratch_shapes=[pltpu.VMEM((tm, D_MODEL), BF16)],
        compiler_params=_params("parallel", "arbitrary"),
        name="project",
    )(x2, g, w_cat, b_gate)


def _swap_major(x):
    return jnp.transpose(x, (1, 0, 2))


def _rows_fwd_kernel(lhs_ref, x_ref, o_ref):
    _, parts, k, sb, cb = x_ref.shape
    x = x_ref[0].astype(BF16).reshape(parts * k, sb, cb)
    xt = _swap_major(x)
    lhs = lhs_ref[0].astype(BF16)
    y = jnp.stack([_dot(lhs, xt[s]).astype(BF16) for s in range(sb)], axis=0)
    o_ref[0] = _swap_major(y).reshape(o_ref.shape[1:])


def _rows_fwd(lhs, x5, name, c=None):
    b, parts, k, n2, _ = x5.shape
    c = c or x5.shape[-1]
    if lhs.ndim == 2:
        lhs = lhs[None]
    per_batch = lhs.shape[0] > 1
    n1 = lhs.shape[1] // 2
    sb, cb = _ROW_GROUP, 512
    return pl.pallas_call(
        _rows_fwd_kernel,
        grid=(b, n2 // sb, c // cb),
        in_specs=[pl.BlockSpec((1,) + lhs.shape[1:],
                               lambda bi, g, ci: (bi if per_batch else 0, 0, 0)),
                  pl.BlockSpec((1, parts, k, sb, cb), lambda bi, g, ci: (bi, 0, 0, g, ci))],
        out_specs=pl.BlockSpec((1, 2, n1, sb, cb), lambda bi, g, ci: (bi, 0, 0, g, ci)),
        out_shape=jax.ShapeDtypeStruct((b, 2, n1, n2, c), BF16),
        compiler_params=_params("parallel", "parallel", "parallel"),
        name=name,
    )(lhs, x5)


def _stage2_fwd(p_ref, q_ref, tw_ref, y_ref, i):
    g = (p_ref[...] * tw_ref[i, 0:1, :] + q_ref[...] * tw_ref[i, 1:2, :]).astype(BF16)
    ys = jnp.concatenate([y_ref[0, 0, i], y_ref[0, 1, i]], axis=0)
    return _dot(g, ys)


def _conv_mid_kernel(p_ref, q_ref, tw_ref, inv_ref, y_ref, kf_ref, o_ref, *, k1b, n2):
    for i in range(k1b):
        s = _stage2_fwd(p_ref, q_ref, tw_ref, y_ref, i)
        xr, xi = s[:n2], s[n2:]
        kr, ki = kf_ref[0, i].astype(F32), kf_ref[1, i].astype(F32)
        prod = jnp.concatenate([xr * kr - xi * ki, xr * ki + xi * kr], axis=0)
        z = _dot(inv_ref[...].astype(BF16), prod.astype(BF16))
        o_ref[0, 0, i] = z[:n2].astype(BF16)
        o_ref[0, 1, i] = z[n2:].astype(BF16)


def _filter_mid_kernel(p_ref, q_ref, tw_ref, pc_ref, qc_ref, y_ref, ss_ref, o_ref, *, k1b, n2):
    scale = lax.rsqrt(ss_ref[...] + EPS)
    for i in range(k1b):
        s = (_stage2_fwd(p_ref, q_ref, tw_ref, y_ref.at[0:1], i)
             + _stage2_fwd(pc_ref, qc_ref, tw_ref, y_ref.at[1:2], i))
        o_ref[0, i] = (s[:n2] * scale).astype(o_ref.dtype)
        o_ref[1, i] = (s[n2:] * scale).astype(o_ref.dtype)


def _fnet_mid_kernel(p_ref, q_ref, tw_ref, y_ref, cc_ref, sc_ref, o_ref, *, k1b, n2):
    cb = o_ref.shape[-1]
    cc = cc_ref[...].astype(BF16)
    sc = sc_ref[...].astype(BF16)
    rows = []
    for i in range(k1b):
        s = _stage2_fwd(p_ref, q_ref, tw_ref, y_ref, i)
        ur, ui = s[:n2].astype(BF16), s[n2:].astype(BF16)
        groups = []
        for lo in range(0, cb, FNET_GROUP_DIM):
            hi = lo + FNET_GROUP_DIM
            groups.append(_dot(ur[:, lo:hi], cc) + _dot(ui[:, lo:hi], sc))
        rows.append(jnp.concatenate(groups, axis=1).astype(BF16))
    o_ref[0] = _swap_major(jnp.stack(rows, axis=0))


def _stage2_specs(n2, c, k1b):
    return [
        _resident((2 * n2, 2 * n2)),
        _resident((2 * n2, 2 * n2)),
        pl.BlockSpec((k1b, 2, 2 * n2), lambda k, b, *_: (k, 0, 0)),
    ], pl.BlockSpec((1, 2, k1b, n2, c), lambda k, b, *_: (b, 0, k, 0, 0))


def _fnet(proj, b, l):
    n2 = _DFT_N2
    n1 = l // n2
    c = D_FNET
    t = _dft_tables(n1, n2)
    lhs = _const(t["rows_fwd"] / math.sqrt(l))
    y = _rows_fwd(lhs, proj.reshape(b, 1, n1, n2, proj.shape[-1]), "fnet_rows", c)
    cc, sc = _cos_sin(FNET_GROUP_DIM)
    norm = 1.0 / math.sqrt(FNET_GROUP_DIM)
    k1b, cb = _ROW_GROUP, 2 * FNET_GROUP_DIM
    head, _ = _stage2_specs(n2, c, k1b)
    out = pl.pallas_call(
        functools.partial(_fnet_mid_kernel, k1b=k1b, n2=n2),
        grid=(n1 // k1b, b, c // cb),
        in_specs=head + [pl.BlockSpec((1, 2, k1b, n2, cb), lambda k, bi, ci: (bi, 0, k, 0, ci)),
                         _resident((FNET_GROUP_DIM, FNET_GROUP_DIM)),
                         _resident((FNET_GROUP_DIM, FNET_GROUP_DIM))],
        out_specs=pl.BlockSpec((1, n2, k1b, cb), lambda k, bi, ci: (bi, 0, k, ci)),
        out_shape=jax.ShapeDtypeStruct((b, n2, n1, c), BF16),
        compiler_params=_params("parallel", "parallel", "parallel"),
        name="fnet_mid",
    )(_const(t["p"]), _const(t["q"]), _const(t["tw_k1"]), y,
      _const(cc * norm), _const(sc * norm))
    return out.reshape(b * l, c)


def _shortconv_kernel(*refs, r, nt):
    u_refs, prev_refs, next_refs = refs[0:3], refs[3:6], refs[6:9]
    w_ref, b_ref, vx_ref, x1_ref = refs[9:]
    i = pl.program_id(1)
    row = lax.broadcasted_iota(jnp.int32, (r, 1), 0)
    outs = []
    for part in range(3):
        cols = slice(part * D_HYENA, (part + 1) * D_HYENA)
        u = u_refs[part][0].astype(F32)
        prev_row = prev_refs[part][0].astype(F32)[_BF16_SUBLANES - 1:_BF16_SUBLANES]
        next_row = next_refs[part][0].astype(F32)[0:1]
        prev_row = jnp.where(i == 0, 0.0, prev_row)
        next_row = jnp.where(i == nt - 1, 0.0, next_row)
        u_prev = jnp.where(row == 0, prev_row, pltpu.roll(u, 1, axis=0))
        u_next = jnp.where(row == r - 1, next_row, pltpu.roll(u, r - 1, axis=0))
        outs.append(u_prev * w_ref[0:1, cols] + u * w_ref[1:2, cols]
                    + u_next * w_ref[2:3, cols] + b_ref[:, cols])
    x1, x2, v = outs
    vx_ref[0] = (v * x2).astype(BF16)
    x1_ref[0] = x1.astype(BF16)


def _shortconv(proj3, conv_w, conv_b):
    b, l, _ = proj3.shape
    c = D_HYENA
    r = 256
    nt = l // r
    hb = _BF16_SUBLANES
    halo = r // hb
    first_block = D_FNET // c
    tiles = [pl.BlockSpec((1, r, c), lambda bi, i, p=p: (bi, i, first_block + p))
             for p in range(3)]
    prevs = [pl.BlockSpec((1, hb, c),
                          lambda bi, i, p=p: (bi, jnp.maximum(i * halo - 1, 0), first_block + p))
             for p in range(3)]
    nexts = [pl.BlockSpec((1, hb, c),
                          lambda bi, i, p=p: (bi, jnp.minimum((i + 1) * halo, l // hb - 1),
                                              first_block + p))
             for p in range(3)]
    return pl.pallas_call(
        functools.partial(_shortconv_kernel, r=r, nt=nt),
        grid=(b, nt),
        in_specs=tiles + prevs + nexts + [_resident((3, 3 * c)), _resident((1, 3 * c))],
        out_specs=[pl.BlockSpec((1, r, c), lambda bi, i: (bi, i, 0))] * 2,
        out_shape=[jax.ShapeDtypeStruct((b, l, c), BF16)] * 2,
        compiler_params=_params("parallel", "parallel"),
        name="shortconv",
    )(*([proj3] * 9), conv_w, conv_b)


def _filter_kernel(bands_ref, w1t_ref, w1c_ref, w1s_ref, b1_ref, w2_ref, b2_ref,
                   w3_ref, b3_ref, fr_ref, w4_ref, deltas_ref, k_ref, ss_ref, *, r, l):
    i = pl.program_id(0)
    c = deltas_ref.shape[1]
    lag = (i * r + lax.broadcasted_iota(jnp.int32, (1, r), 1)).astype(F32)
    ang = bands_ref[...] * ((2.0 * math.pi / l) * lag)
    fr = fr_ref[...]
    pre = (w1t_ref[...] * (lag * (1.0 / (l - 1))) + _dot_hi(w1c_ref[...], jnp.cos(ang))
           - _dot_hi(w1s_ref[...], jnp.sin(ang)) + b1_ref[...])
    h = jnp.sin(fr * pre)
    h = jnp.sin(fr * (_dot_hi(w2_ref[...], h) + b2_ref[...]))
    h = jnp.sin(fr * (_dot_hi(w3_ref[...], h) + b3_ref[...]))
    taps = _dot_tn_bf16x3(h, w4_ref[...])
    row = i * r + lax.broadcasted_iota(jnp.int32, (r, 1), 0)
    decay = jnp.exp(-(row.astype(F32) * (1.0 / (l - 1))) * deltas_ref[...])
    k_fwd = taps[:, :c] * decay
    k_bwd = jnp.where(row == 0, 0.0, taps[:, c:] * decay)
    k_ref[0] = k_fwd
    k_ref[1] = k_bwd

    @pl.when(i == 0)
    def _():
        ss_ref[...] = jnp.zeros_like(ss_ref)

    ss_ref[...] += jnp.sum(k_fwd * k_fwd + k_bwd * k_bwd, axis=0, keepdims=True)


def _filter_spectrum(l, w1, b1, w2, b2, w3, b3, w4, freq):
    n = 2 * l
    n2 = _DFT_N2
    n1 = n // n2
    half = n1 // 2
    c = D_HYENA
    r = 1024
    bands = np.linspace(1e-4, FILTER_BANDS - 1, FILTER_BANDS)[:, None]
    deltas = np.abs(np.linspace(math.log(DECAY_TARGET) / FAST_DECAY_PCT,
                                math.log(DECAY_TARGET) / SLOW_DECAY_PCT, D_HYENA))[None, :]
    small = [_const(bands), w1[0:1].T, w1[1:1 + FILTER_BANDS].T, w1[1 + FILTER_BANDS:].T,
             b1[:, None], w2.T, b2[:, None], w3.T, b3[:, None], freq[:, None], w4, _const(deltas)]
    k, ss = pl.pallas_call(
        functools.partial(_filter_kernel, r=r, l=l),
        grid=(l // r,),
        in_specs=[_resident(s.shape) for s in small],
        out_specs=[pl.BlockSpec((2, r, c), lambda i: (0, i, 0)),
                   pl.BlockSpec((1, c), lambda i: (0, 0))],
        out_shape=[jax.ShapeDtypeStruct((2, l, c), F32),
                   jax.ShapeDtypeStruct((1, c), F32)],
        compiler_params=_params("arbitrary"),
        name="hyena_filter",
    )(*small)

    t = _dft_tables(n1, n2)
    fwd = t["rows_fwd"][:, :half]
    lhs = np.stack([fwd, np.concatenate([fwd[:n1], -fwd[n1:]], axis=0)])
    y = _rows_fwd(_const(lhs), k.reshape(2, 1, half, n2, c), "filter_rows")
    k1b = 4
    head, _ = _stage2_specs(n2, c, k1b)
    return pl.pallas_call(
        functools.partial(_filter_mid_kernel, k1b=k1b, n2=n2),
        grid=(n1 // k1b, 1),
        in_specs=head + [_resident((2 * n2, 2 * n2)), _resident((2 * n2, 2 * n2)),
                         pl.BlockSpec((2, 2, k1b, n2, c), lambda k, bi: (0, 0, k, 0, 0)),
                         _resident((1, c))],
        out_specs=pl.BlockSpec((2, k1b, n2, c), lambda k, bi: (0, k, 0, 0)),
        out_shape=jax.ShapeDtypeStruct((2, n1, n2, c), BF16),
        compiler_params=_params("parallel", "parallel"),
        name="filter_mid",
    )(_const(t["p"]), _const(t["q"]), _const(t["tw_k1"]), _const(t["p_conj"]),
      _const(t["q_conj"]), y, ss)


def _conv_out_kernel(pc_ref, qc_ref, tw_ref, z_ref, vx_ref, x1_ref, skip_ref, o_ref):
    _, _, n1, sb, cb = z_ref.shape
    z = _swap_major(z_ref[0].reshape(2 * n1, sb, cb))
    ys = []
    for s in range(sb):
        lhs = (pc_ref[...] * tw_ref[s, 0:1, :] + qc_ref[...] * tw_ref[s, 1:2, :]).astype(BF16)
        ys.append(_dot(lhs, z[s]))
    y = _swap_major(jnp.stack(ys, axis=0)).reshape(o_ref.shape)
    y = y + vx_ref[...].astype(F32) * skip_ref[...]
    o_ref[...] = (y * x1_ref[...].astype(F32)).astype(BF16)


def _longconv(vx, x1, kf, skip):
    b, l, c = vx.shape
    assert b == 2, "batch pair is packed as (re, im)"
    n = 2 * l
    n2 = _DFT_N2
    n1 = n // n2
    half = n1 // 2
    t = _dft_tables(n1, n2)
    y = _rows_fwd(_const(t["rows_fwd_c"]), vx.reshape(1, 2, half, n2, c), "conv_rows")
    k1b = 4
    head, y_spec = _stage2_specs(n2, c, k1b)
    z = pl.pallas_call(
        functools.partial(_conv_mid_kernel, k1b=k1b, n2=n2),
        grid=(n1 // k1b, 1),
        in_specs=head + [_resident((2 * n2, 2 * n2)), y_spec,
                         pl.BlockSpec((2, k1b, n2, c), lambda k, bi: (0, k, 0, 0))],
        out_specs=y_spec,
        out_shape=jax.ShapeDtypeStruct((1, 2, n1, n2, c), BF16),
        compiler_params=_params("parallel", "parallel"),
        name="conv_mid",
    )(_const(t["p"]), _const(t["q"]), _const(t["tw_k1"]), _const(t["inv2"] / n), y, kf)

    sb, cb = _ROW_GROUP, 512
    seq_spec = pl.BlockSpec((2, half, sb, cb), lambda g, ci: (0, 0, g, ci))
    out = pl.pallas_call(
        _conv_out_kernel,
        grid=(n2 // sb, c // cb),
        in_specs=[_resident((n1, 2 * n1)), _resident((n1, 2 * n1)),
                  pl.BlockSpec((sb, 2, 2 * n1), lambda g, ci: (g, 0, 0)),
                  pl.BlockSpec((1, 2, n1, sb, cb), lambda g, ci: (0, 0, 0, g, ci)),
                  seq_spec, seq_spec,
                  pl.BlockSpec((1, cb), lambda g, ci: (0, ci))],
        out_specs=seq_spec,
        out_shape=jax.ShapeDtypeStruct((2, half, n2, c), BF16),
        compiler_params=_params("parallel", "parallel"),
        name="conv_out",
    )(_const(t["pc"]), _const(t["qc"]), _const(t["tw_n2"]), z,
      vx.reshape(2, half, n2, c), x1.reshape(2, half, n2, c), skip)
    return out.reshape(b * l, c)


def _merge_kernel(mx_ref, yh_ref, ga_ref, gb_ref, x_ref, wf_ref, wh_ref, wo_ref, g2_ref,
                  x1_ref, h2_ref, m_ref, *, chunk):
    d = x_ref.shape[1]
    for lo in range(0, d, chunk):
        cols = slice(lo, lo + chunk)
        ya = _dot(mx_ref[...], wf_ref[:, cols])
        yb = _dot(yh_ref[...], wh_ref[:, cols])
        m_ref[:, cols] = (ga_ref[:, cols].astype(F32) * ya
                          + gb_ref[:, cols].astype(F32) * yb).astype(BF16)
    ss = jnp.zeros((x_ref.shape[0], 1), F32)
    for lo in range(0, d, chunk):
        cols = slice(lo, lo + chunk)
        x1 = x_ref[:, cols] + _dot(m_ref[...], wo_ref[:, cols])
        x1_ref[:, cols] = x1
        ss = ss + jnp.sum(x1 * x1, axis=-1, keepdims=True)
    scale = lax.rsqrt(ss * (1.0 / d) + EPS)
    for lo in range(0, d, chunk):
        cols = slice(lo, lo + chunk)
        h2_ref[:, cols] = (x1_ref[:, cols] * scale * g2_ref[:, cols]).astype(BF16)


def _merge(mixed, yh, proj, x2, wf, wh, wo, g2):
    rows = x2.shape[0]
    tm = 512
    d = D_MODEL
    gate_block = (proj.shape[1] - 2 * d) // d
    return pl.pallas_call(
        functools.partial(_merge_kernel, chunk=512),
        grid=(rows // tm,),
        in_specs=[
            pl.BlockSpec((tm, D_FNET), lambda i: (i, 0)),
            pl.BlockSpec((tm, D_HYENA), lambda i: (i, 0)),
            pl.BlockSpec((tm, d), lambda i: (i, gate_block)),
            pl.BlockSpec((tm, d), lambda i: (i, gate_block + 1)),
            pl.BlockSpec((tm, d), lambda i: (i, 0)),
            _resident((D_FNET, d)), _resident((D_HYENA, d)), _resident((d, d)),
            _resident((1, d)),
        ],
        out_specs=[pl.BlockSpec((tm, d), lambda i: (i, 0))] * 2,
        out_shape=[jax.ShapeDtypeStruct((rows, d), F32),
                   jax.ShapeDtypeStruct((rows, d), BF16)],
        scratch_shapes=[pltpu.VMEM((tm, d), BF16)],
        compiler_params=_params("parallel"),
        name="merge",
    )(mixed, yh, proj, proj, x2, wf, wh, wo, g2)


def _mlp_kernel(h2_ref, x1_ref, w1_ref, w2_ref, gf_ref, o_ref, *, nj):
    j = pl.program_id(1)

    @pl.when(j == 0)
    def _():
        o_ref[...] = x1_ref[...]

    t = _dot(h2_ref[...], w1_ref[...])
    t = jnp.square(jnp.maximum(t, 0.0)).astype(BF16)
    o_ref[...] += _dot(t, w2_ref[...])

    @pl.when(j == nj - 1)
    def _():
        x = o_ref[...]
        ms = jnp.mean(x * x, axis=-1, keepdims=True)
        o_ref[...] = x * lax.rsqrt(ms + EPS) * gf_ref[...]


def _mlp(h2, x1, w1, w2, gf):
    rows = x1.shape[0]
    tm, tf = 512, 1024
    d = D_MODEL
    nj = D_FF // tf
    return pl.pallas_call(
        functools.partial(_mlp_kernel, nj=nj),
        grid=(rows // tm, nj),
        in_specs=[
            pl.BlockSpec((tm, d), lambda i, j: (i, 0)),
            pl.BlockSpec((tm, d), lambda i, j: (i, 0)),
            pl.BlockSpec((d, tf), lambda i, j: (0, j)),
            pl.BlockSpec((tf, d), lambda i, j: (j, 0)),
            pl.BlockSpec((1, d), lambda i, j: (0, 0)),
        ],
        out_specs=pl.BlockSpec((tm, d), lambda i, j: (i, 0)),
        out_shape=jax.ShapeDtypeStruct((rows, d), F32),
        compiler_params=_params("parallel", "arbitrary"),
        name="mlp",
    )(h2, x1, w1, w2, gf)


def _layer_and_final_norm(x, w, filt):
    b, l, d = x.shape
    x2 = x.reshape(b * l, d)
    proj = _project(x2, w["norm1_g"], w["w_cat"], w["b_gate"])
    mixed = _fnet(proj, b, l)
    vx, x1h = _shortconv(proj.reshape(b, l, proj.shape[-1]), w["conv_w"], w["conv_b"])
    kf = _filter_spectrum(l, *filt)
    yh = _longconv(vx, x1h, kf, w["skip"])
    x1, h2 = _merge(mixed, yh, proj, x2, w["w_fnet_map"], w["w_hyena_out"], w["w_out"],
                    w["norm2_g"])
    out = _mlp(h2, x1, w["w_mlp1"], w["w_mlp2"], w["norm_f_g"])
    return out.reshape(b, l, d)


def kernel(x_prompt, x_sample, norm1_g, w_in, conv_w, conv_b, filt_w1, filt_b1, filt_w2, filt_b2, filt_w3, filt_b3, filt_w4, filt_freq, hyena_skip, w_fnet_map, w_hyena_out, w_gate, b_gate, w_out, norm2_g, w_mlp1, w_mlp2, norm_f_g):
    assert norm1_g.shape[0] == 1, "one layer"
    w = {
        "norm1_g": norm1_g[0][None],
        "w_cat": jnp.concatenate([w_in[0], w_gate[0]], axis=1).astype(BF16),
        "b_gate": b_gate[0][None],
        "conv_w": conv_w[0],
        "conv_b": conv_b[0][None],
        "skip": hyena_skip[0][None],
        "w_fnet_map": w_fnet_map[0].astype(BF16),
        "w_hyena_out": w_hyena_out[0].astype(BF16),
        "w_out": w_out[0].astype(BF16),
        "norm2_g": norm2_g[0][None],
        "w_mlp1": w_mlp1[0].astype(BF16),
        "w_mlp2": w_mlp2[0].astype(BF16),
        "norm_f_g": norm_f_g[None],
    }
    filt = (filt_w1[0], filt_b1[0], filt_w2[0], filt_b2[0], filt_w3[0], filt_b3[0],
            filt_w4[0], filt_freq[0])
    return (_layer_and_final_norm(x_prompt, w, filt),
            _layer_and_final_norm(x_sample, w, filt))
```

```python
import functools
import math

import jax
import jax.numpy as jnp
import numpy as np
from jax import lax
from jax.experimental import pallas as pl
from jax.experimental.pallas import tpu as pltpu

D_MODEL = 2048
N_FNET_GROUPS = 4
FNET_GROUP_DIM = 256
D_FNET = N_FNET_GROUPS * FNET_GROUP_DIM
D_HYENA = 1024
FILTER_BANDS = 16
FILTER_HIDDEN = 64
DECAY_TARGET = 1e-2
FAST_DECAY_PCT = 0.3
SLOW_DECAY_PCT = 1.5
D_FF = 4 * D_MODEL
EPS = 1e-6

F32 = jnp.float32
BF16 = jnp.bfloat16

_VMEM_LIMIT_BYTES = 56 * 1024 * 1024
_DFT_N2 = 128
_BF16_SUBLANES = 16
_ROW_GROUP = _BF16_SUBLANES

_dot = functools.partial(jnp.dot, preferred_element_type=F32)
_dot_hi = functools.partial(jnp.dot, preferred_element_type=F32,
                            precision=lax.Precision.HIGHEST)


def _dot_tn_bf16x3(a, b):
    dims = (((0,), (0,)), ((), ()))
    a_hi, b_hi = a.astype(BF16), b.astype(BF16)
    a_lo = (a - a_hi.astype(F32)).astype(BF16)
    b_lo = (b - b_hi.astype(F32)).astype(BF16)
    dg = functools.partial(lax.dot_general, dimension_numbers=dims, preferred_element_type=F32)
    return dg(a_hi, b_hi) + dg(a_lo, b_hi) + dg(a_hi, b_lo)


def _params(*sem):
    return pltpu.CompilerParams(dimension_semantics=sem,
                                vmem_limit_bytes=_VMEM_LIMIT_BYTES)


def _resident(shape):
    zeros = (0,) * len(shape)
    return pl.BlockSpec(shape, lambda *_: zeros, pipeline_mode=pl.Buffered(1))


def _cos_sin(n):
    jk = (np.arange(n)[:, None] * np.arange(n)[None, :]) % n
    ang = 2.0 * np.pi * jk / n
    return np.cos(ang), np.sin(ang)


@functools.lru_cache(maxsize=None)
def _dft_tables(n1, n2):
    n = n1 * n2
    c1, s1 = _cos_sin(n1)
    c2, s2 = _cos_sin(n2)
    fr1, fi1 = c1, -s1
    fr2, fi2 = c2, -s2
    ang = 2.0 * np.pi * ((np.arange(n1)[:, None] * np.arange(n2)[None, :]) % n) / n
    twr, twi = np.cos(ang), -np.sin(ang)
    t = {}
    h = n1 // 2
    t["rows_fwd"] = np.concatenate([fr1, fi1], axis=0)
    t["rows_fwd_c"] = np.block([[fr1[:, :h], -fi1[:, :h]], [fi1[:, :h], fr1[:, :h]]])
    t["p"] = np.block([[fr2, -fi2], [fi2, fr2]])
    t["q"] = np.block([[-fi2, -fr2], [fr2, -fi2]])
    t["tw_k1"] = np.stack([np.concatenate([twr, twr], axis=1),
                           np.concatenate([twi, twi], axis=1)], axis=1)
    t["p_conj"] = np.block([[fr2, fi2], [-fi2, fr2]])
    t["q_conj"] = np.block([[-fi2, fr2], [-fr2, -fi2]])
    t["inv2"] = np.block([[fr2, fi2], [-fi2, fr2]])
    t["pc"] = np.block([[fr1[:h], fi1[:h]], [-fi1[:h], fr1[:h]]])
    t["qc"] = np.block([[-fi1[:h], fr1[:h]], [-fr1[:h], -fi1[:h]]])
    t["tw_n2"] = np.stack([np.concatenate([twr.T, twr.T], axis=1),
                           np.concatenate([twi.T, twi.T], axis=1)], axis=1)
    return t


def _const(a, dtype=F32):
    return jnp.asarray(np.asarray(a, np.float32), dtype)


def _project_kernel(x_ref, g_ref, w_ref, b_ref, o_ref, h_ref, *, gate_tile, chunk):
    j = pl.program_id(1)

    @pl.when(j == 0)
    def _():
        x = x_ref[...]
        ms = jnp.mean(x * x, axis=-1, keepdims=True)
        h_ref[...] = (x * lax.rsqrt(ms + EPS) * g_ref[...]).astype(BF16)

    def emit(gated):
        for lo in range(0, o_ref.shape[1], chunk):
            acc = _dot(h_ref[...], w_ref[:, lo:lo + chunk])
            if gated:
                acc = jax.nn.sigmoid(acc + b_ref[:, lo:lo + chunk])
            o_ref[:, lo:lo + chunk] = acc.astype(BF16)

    @pl.when(j < gate_tile)
    def _():
        emit(False)

    @pl.when(j >= gate_tile)
    def _():
        emit(True)


def _project(x2, g, w_cat, b_gate):
    rows = x2.shape[0]
    n_out = w_cat.shape[1]
    tm, tn = 1024, 2048
    gate_tile = (n_out - b_gate.shape[1]) // tn
    return pl.pallas_call(
        functools.partial(_project_kernel, gate_tile=gate_tile, chunk=512),
        grid=(rows // tm, n_out // tn),
        in_specs=[
            pl.BlockSpec((tm, D_MODEL), lambda i, j: (i, 0)),
            pl.BlockSpec((1, D_MODEL), lambda i, j: (0, 0)),
            pl.BlockSpec((D_MODEL, tn), lambda i, j: (0, j)),
            pl.BlockSpec((1, tn), lambda i, j: (0, jnp.maximum(j - gate_tile, 0))),
        ],
        out_specs=pl.BlockSpec((tm, tn), lambda i, j: (i, j)),
        out_shape=jax.ShapeDtypeStruct((rows, n_out), BF16),
        scratch_shapes=[pltpu.VMEM((tm, D_MODEL), BF16)],
        compiler_params=_params("parallel", "arbitrary"),
        name="project",
    )(x2, g, w_cat, b_gate)


def _swap_major(x):
    return jnp.transpose(x, (1, 0, 2))


def _rows_fwd_kernel(lhs_ref, x_ref, o_ref):
    _, parts, k, sb, cb = x_ref.shape
    x = x_ref[0].astype(BF16).reshape(parts * k, sb, cb)
    xt = _swap_major(x)
    lhs = lhs_ref[0].astype(BF16)
    y = jnp.stack([_dot(lhs, xt[s]).astype(BF16) for s in range(sb)], axis=0)
    o_ref[0] = _swap_major(y).reshape(o_ref.shape[1:])


def _rows_fwd(lhs, x5, name, c=None):
    b, parts, k, n2, _ = x5.shape
    c = c or x5.shape[-1]
    if lhs.ndim == 2:
        lhs = lhs[None]
    per_batch = lhs.shape[0] > 1
    n1 = lhs.shape[1] // 2
    sb, cb = _ROW_GROUP, 512
    return pl.pallas_call(
        _rows_fwd_kernel,
        grid=(b, n2 // sb, c // cb),
        in_specs=[pl.BlockSpec((1,) + lhs.shape[1:],
                               lambda bi, g, ci: (bi if per_batch else 0, 0, 0)),
                  pl.BlockSpec((1, parts, k, sb, cb), lambda bi, g, ci: (bi, 0, 0, g, ci))],
        out_specs=pl.BlockSpec((1, 2, n1, sb, cb), lambda bi, g, ci: (bi, 0, 0, g, ci)),
        out_shape=jax.ShapeDtypeStruct((b, 2, n1, n2, c), BF16),
        compiler_params=_params("parallel", "parallel", "parallel"),
        name=name,
    )(lhs, x5)


def _stage2_fwd(p_ref, q_ref, tw_ref, y_ref, i):
    g = (p_ref[...] * tw_ref[i, 0:1, :] + q_ref[...] * tw_ref[i, 1:2, :]).astype(BF16)
    ys = jnp.concatenate([y_ref[0, 0, i], y_ref[0, 1, i]], axis=0)
    return _dot(g, ys)


def _conv_mid_kernel(p_ref, q_ref, tw_ref, inv_ref, y_ref, kf_ref, o_ref, *, k1b, n2):
    for i in range(k1b):
        s = _stage2_fwd(p_ref, q_ref, tw_ref, y_ref, i)
        xr, xi = s[:n2], s[n2:]
        kr, ki = kf_ref[0, i].astype(F32), kf_ref[1, i].astype(F32)
        prod = jnp.concatenate([xr * kr - xi * ki, xr * ki + xi * kr], axis=0)
        z = _dot(inv_ref[...].astype(BF16), prod.astype(BF16))
        o_ref[0, 0, i] = z[:n2].astype(BF16)
        o_ref[0, 1, i] = z[n2:].astype(BF16)


def _filter_mid_kernel(p_ref, q_ref, tw_ref, pc_ref, qc_ref, y_ref, ss_ref, o_ref, *, k1b, n2):
    scale = lax.rsqrt(ss_ref[...] + EPS)
    for i in range(k1b):
        s = (_stage2_fwd(p_ref, q_ref, tw_ref, y_ref.at[0:1], i)
             + _stage2_fwd(pc_ref, qc_ref, tw_ref, y_ref.at[1:2], i))
        o_ref[0, i] = (s[:n2] * scale).astype(o_ref.dtype)
        o_ref[1, i] = (s[n2:] * scale).astype(o_ref.dtype)


def _fnet_mid_kernel(p_ref, q_ref, tw_ref, y_ref, cc_ref, sc_ref, o_ref, *, k1b, n2):
    cb = o_ref.shape[-1]
    cc = cc_ref[...].astype(BF16)
    sc = sc_ref[...].astype(BF16)
    rows = []
    for i in range(k1b):
        s = _stage2_fwd(p_ref, q_ref, tw_ref, y_ref, i)
        ur, ui = s[:n2].astype(BF16), s[n2:].astype(BF16)
        groups = []
        for lo in range(0, cb, FNET_GROUP_DIM):
            hi = lo + FNET_GROUP_DIM
            groups.append(_dot(ur[:, lo:hi], cc) + _dot(ui[:, lo:hi], sc))
        rows.append(jnp.concatenate(groups, axis=1).astype(BF16))
    o_ref[0] = _swap_major(jnp.stack(rows, axis=0))


def _stage2_specs(n2, c, k1b):
    return [
        _resident((2 * n2, 2 * n2)),
        _resident((2 * n2, 2 * n2)),
        pl.BlockSpec((k1b, 2, 2 * n2), lambda k, b, *_: (k, 0, 0)),
    ], pl.BlockSpec((1, 2, k1b, n2, c), lambda k, b, *_: (b, 0, k, 0, 0))


def _fnet(proj, b, l):
    n2 = _DFT_N2
    n1 = l // n2
    c = D_FNET
    t = _dft_tables(n1, n2)
    lhs = _const(t["rows_fwd"] / math.sqrt(l))
    y = _rows_fwd(lhs, proj.reshape(b, 1, n1, n2, proj.shape[-1]), "fnet_rows", c)
    cc, sc = _cos_sin(FNET_GROUP_DIM)
    norm = 1.0 / math.sqrt(FNET_GROUP_DIM)
    k1b, cb = _ROW_GROUP, 2 * FNET_GROUP_DIM
    head, _ = _stage2_specs(n2, c, k1b)
    out = pl.pallas_call(
        functools.partial(_fnet_mid_kernel, k1b=k1b, n2=n2),
        grid=(n1 // k1b, b, c // cb),
        in_specs=head + [pl.BlockSpec((1, 2, k1b, n2, cb), lambda k, bi, ci: (bi, 0, k, 0, ci)),
                         _resident((FNET_GROUP_DIM, FNET_GROUP_DIM)),
                         _resident((FNET_GROUP_DIM, FNET_GROUP_DIM))],
        out_specs=pl.BlockSpec((1, n2, k1b, cb), lambda k, bi, ci: (bi, 0, k, ci)),
        out_shape=jax.ShapeDtypeStruct((b, n2, n1, c), BF16),
        compiler_params=_params("parallel", "parallel", "parallel"),
        name="fnet_mid",
    )(_const(t["p"]), _const(t["q"]), _const(t["tw_k1"]), y,
      _const(cc * norm), _const(sc * norm))
    return out.reshape(b * l, c)


def _shortconv_kernel(*refs, r, nt):
    u_refs, prev_refs, next_refs = refs[0:3], refs[3:6], refs[6:9]
    w_ref, b_ref, vx_ref, x1_ref = refs[9:]
    i = pl.program_id(1)
    row = lax.broadcasted_iota(jnp.int32, (r, 1), 0)
    outs = []
    for part in range(3):
        cols = slice(part * D_HYENA, (part + 1) * D_HYENA)
        u = u_refs[part][0].astype(F32)
        prev_row = prev_refs[part][0].astype(F32)[_BF16_SUBLANES - 1:_BF16_SUBLANES]
        next_row = next_refs[part][0].astype(F32)[0:1]
        prev_row = jnp.where(i == 0, 0.0, prev_row)
        next_row = jnp.where(i == nt - 1, 0.0, next_row)
        u_prev = jnp.where(row == 0, prev_row, pltpu.roll(u, 1, axis=0))
        u_next = jnp.where(row == r - 1, next_row, pltpu.roll(u, r - 1, axis=0))
        outs.append(u_prev * w_ref[0:1, cols] + u * w_ref[1:2, cols]
                    + u_next * w_ref[2:3, cols] + b_ref[:, cols])
    x1, x2, v = outs
    vx_ref[0] = (v * x2).astype(BF16)
    x1_ref[0] = x1.astype(BF16)


def _shortconv(proj3, conv_w, conv_b):
    b, l, _ = proj3.shape
    c = D_HYENA
    r = 512
    nt = l // r
    hb = _BF16_SUBLANES
    halo = r // hb
    first_block = D_FNET // c
    tiles = [pl.BlockSpec((1, r, c), lambda bi, i, p=p: (bi, i, first_block + p))
             for p in range(3)]
    prevs = [pl.BlockSpec((1, hb, c),
                          lambda bi, i, p=p: (bi, jnp.maximum(i * halo - 1, 0), first_block + p))
             for p in range(3)]
    nexts = [pl.BlockSpec((1, hb, c),
                          lambda bi, i, p=p: (bi, jnp.minimum((i + 1) * halo, l // hb - 1),
                                              first_block + p))
             for p in range(3)]
    return pl.pallas_call(
        functools.partial(_shortconv_kernel, r=r, nt=nt),
        grid=(b, nt),
        in_specs=tiles + prevs + nexts + [_resident((3, 3 * c)), _resident((1, 3 * c))],
        out_specs=[pl.BlockSpec((1, r, c), lambda bi, i: (bi, i, 0))] * 2,
        out_shape=[jax.ShapeDtypeStruct((b, l, c), BF16)] * 2,
        compiler_params=_params("parallel", "parallel"),
        name="shortconv",
    )(*([proj3] * 9), conv_w, conv_b)


def _filter_kernel(bands_ref, w1t_ref, w1c_ref, w1s_ref, b1_ref, w2_ref, b2_ref,
                   w3_ref, b3_ref, fr_ref, w4_ref, deltas_ref, k_ref, ss_ref, *, r, l):
    i = pl.program_id(0)
    c = deltas_ref.shape[1]
    lag = (i * r + lax.broadcasted_iota(jnp.int32, (1, r), 1)).astype(F32)
    ang = bands_ref[...] * ((2.0 * math.pi / l) * lag)
    fr = fr_ref[...]
    pre = (w1t_ref[...] * (lag * (1.0 / (l - 1))) + _dot_hi(w1c_ref[...], jnp.cos(ang))
           - _dot_hi(w1s_ref[...], jnp.sin(ang)) + b1_ref[...])
    h = jnp.sin(fr * pre)
    h = jnp.sin(fr * (_dot_hi(w2_ref[...], h) + b2_ref[...]))
    h = jnp.sin(fr * (_dot_hi(w3_ref[...], h) + b3_ref[...]))
    taps = _dot_tn_bf16x3(h, w4_ref[...])
    row = i * r + lax.broadcasted_iota(jnp.int32, (r, 1), 0)
    decay = jnp.exp(-(row.astype(F32) * (1.0 / (l - 1))) * deltas_ref[...])
    k_fwd = taps[:, :c] * decay
    k_bwd = jnp.where(row == 0, 0.0, taps[:, c:] * decay)
    k_ref[0] = k_fwd.astype(k_ref.dtype)
    k_ref[1] = k_bwd.astype(k_ref.dtype)

    @pl.when(i == 0)
    def _():
        ss_ref[...] = jnp.zeros_like(ss_ref)

    ss_ref[...] += jnp.sum(k_fwd * k_fwd + k_bwd * k_bwd, axis=0, keepdims=True)


def _filter_spectrum(l, w1, b1, w2, b2, w3, b3, w4, freq):
    n = 2 * l
    n2 = _DFT_N2
    n1 = n // n2
    half = n1 // 2
    c = D_HYENA
    r = 1024
    bands = np.linspace(1e-4, FILTER_BANDS - 1, FILTER_BANDS)[:, None]
    deltas = np.abs(np.linspace(math.log(DECAY_TARGET) / FAST_DECAY_PCT,
                                math.log(DECAY_TARGET) / SLOW_DECAY_PCT, D_HYENA))[None, :]
    small = [_const(bands), w1[0:1].T, w1[1:1 + FILTER_BANDS].T, w1[1 + FILTER_BANDS:].T,
             b1[:, None], w2.T, b2[:, None], w3.T, b3[:, None], freq[:, None], w4, _const(deltas)]
    k, ss = pl.pallas_call(
        functools.partial(_filter_kernel, r=r, l=l),
        grid=(l // r,),
        in_specs=[_resident(s.shape) for s in small],
        out_specs=[pl.BlockSpec((2, r, c), lambda i: (0, i, 0)),
                   pl.BlockSpec((1, c), lambda i: (0, 0))],
        out_shape=[jax.ShapeDtypeStruct((2, l, c), BF16),
                   jax.ShapeDtypeStruct((1, c), F32)],
        compiler_params=_params("arbitrary"),
        name="hyena_filter",
    )(*small)

    t = _dft_tables(n1, n2)
    fwd = t["rows_fwd"][:, :half]
    lhs = np.stack([fwd, np.concatenate([fwd[:n1], -fwd[n1:]], axis=0)])
    y = _rows_fwd(_const(lhs), k.reshape(2, 1, half, n2, c), "filter_rows")
    k1b = 8
    head, _ = _stage2_specs(n2, c, k1b)
    return pl.pallas_call(
        functools.partial(_filter_mid_kernel, k1b=k1b, n2=n2),
        grid=(n1 // k1b, 1),
        in_specs=head + [_resident((2 * n2, 2 * n2)), _resident((2 * n2, 2 * n2)),
                         pl.BlockSpec((2, 2, k1b, n2, c), lambda k, bi: (0, 0, k, 0, 0)),
                         _resident((1, c))],
        out_specs=pl.BlockSpec((2, k1b, n2, c), lambda k, bi: (0, k, 0, 0)),
        out_shape=jax.ShapeDtypeStruct((2, n1, n2, c), BF16),
        compiler_params=_params("parallel", "parallel"),
        name="filter_mid",
    )(_const(t["p"]), _const(t["q"]), _const(t["tw_k1"]), _const(t["p_conj"]),
      _const(t["q_conj"]), y, ss)


def _conv_out_kernel(pc_ref, qc_ref, tw_ref, z_ref, vx_ref, x1_ref, skip_ref, o_ref):
    _, _, n1, sb, cb = z_ref.shape
    z = _swap_major(z_ref[0].reshape(2 * n1, sb, cb))
    ys = []
    for s in range(sb):
        lhs = (pc_ref[...] * tw_ref[s, 0:1, :] + qc_ref[...] * tw_ref[s, 1:2, :]).astype(BF16)
        ys.append(_dot(lhs, z[s]))
    y = _swap_major(jnp.stack(ys, axis=0)).reshape(o_ref.shape)
    y = y + vx_ref[...].astype(F32) * skip_ref[...]
    o_ref[...] = (y * x1_ref[...].astype(F32)).astype(BF16)


def _longconv(vx, x1, kf, skip):
    b, l, c = vx.shape
    assert b == 2, "batch pair is packed as (re, im)"
    n = 2 * l
    n2 = _DFT_N2
    n1 = n // n2
    half = n1 // 2
    t = _dft_tables(n1, n2)
    y = _rows_fwd(_const(t["rows_fwd_c"]), vx.reshape(1, 2, half, n2, c), "conv_rows")
    k1b = 8
    head, y_spec = _stage2_specs(n2, c, k1b)
    z = pl.pallas_call(
        functools.partial(_conv_mid_kernel, k1b=k1b, n2=n2),
        grid=(n1 // k1b, 1),
        in_specs=head + [_resident((2 * n2, 2 * n2)), y_spec,
                         pl.BlockSpec((2, k1b, n2, c), lambda k, bi: (0, k, 0, 0))],
        out_specs=y_spec,
        out_shape=jax.ShapeDtypeStruct((1, 2, n1, n2, c), BF16),
        compiler_params=_params("parallel", "parallel"),
        name="conv_mid",
    )(_const(t["p"]), _const(t["q"]), _const(t["tw_k1"]), _const(t["inv2"] / n), y, kf)

    sb, cb = _ROW_GROUP, 512
    seq_spec = pl.BlockSpec((2, half, sb, cb), lambda g, ci: (0, 0, g, ci))
    out = pl.pallas_call(
        _conv_out_kernel,
        grid=(n2 // sb, c // cb),
        in_specs=[_resident((n1, 2 * n1)), _resident((n1, 2 * n1)),
                  pl.BlockSpec((sb, 2, 2 * n1), lambda g, ci: (g, 0, 0)),
                  pl.BlockSpec((1, 2, n1, sb, cb), lambda g, ci: (0, 0, 0, g, ci)),
                  seq_spec, seq_spec,
                  pl.BlockSpec((1, cb), lambda g, ci: (0, ci))],
        out_specs=seq_spec,
        out_shape=jax.ShapeDtypeStruct((2, half, n2, c), BF16),
        compiler_params=_params("parallel", "parallel"),
        name="conv_out",
    )(_const(t["pc"]), _const(t["qc"]), _const(t["tw_n2"]), z,
      vx.reshape(2, half, n2, c), x1.reshape(2, half, n2, c), skip)
    return out.reshape(b * l, c)


def _merge_kernel(mx_ref, yh_ref, ga_ref, gb_ref, x_ref, wf_ref, wh_ref, wo_ref, g2_ref,
                  x1_ref, h2_ref, m_ref, *, chunk):
    d = x_ref.shape[1]
    for lo in range(0, d, chunk):
        cols = slice(lo, lo + chunk)
        ya = _dot(mx_ref[...], wf_ref[:, cols])
        yb = _dot(yh_ref[...], wh_ref[:, cols])
        m_ref[:, cols] = (ga_ref[:, cols].astype(F32) * ya
                          + gb_ref[:, cols].astype(F32) * yb).astype(BF16)
    ss = jnp.zeros((x_ref.shape[0], 1), F32)
    for lo in range(0, d, chunk):
        cols = slice(lo, lo + chunk)
        x1 = x_ref[:, cols] + _dot(m_ref[...], wo_ref[:, cols])
        x1_ref[:, cols] = x1
        ss = ss + jnp.sum(x1 * x1, axis=-1, keepdims=True)
    scale = lax.rsqrt(ss * (1.0 / d) + EPS)
    for lo in range(0, d, chunk):
        cols = slice(lo, lo + chunk)
        h2_ref[:, cols] = (x1_ref[:, cols] * scale * g2_ref[:, cols]).astype(BF16)


def _merge(mixed, yh, proj, x2, wf, wh, wo, g2):
    rows = x2.shape[0]
    tm = 512
    d = D_MODEL
    gate_block = (proj.shape[1] - 2 * d) // d
    return pl.pallas_call(
        functools.partial(_merge_kernel, chunk=512),
        grid=(rows // tm,),
        in_specs=[
            pl.BlockSpec((tm, D_FNET), lambda i: (i, 0)),
            pl.BlockSpec((tm, D_HYENA), lambda i: (i, 0)),
            pl.BlockSpec((tm, d), lambda i: (i, gate_block)),
            pl.BlockSpec((tm, d), lambda i: (i, gate_block + 1)),
            pl.BlockSpec((tm, d), lambda i: (i, 0)),
            _resident((D_FNET, d)), _resident((D_HYENA, d)), _resident((d, d)),
            _resident((1, d)),
        ],
        out_specs=[pl.BlockSpec((tm, d), lambda i: (i, 0))] * 2,
        out_shape=[jax.ShapeDtypeStruct((rows, d), F32),
                   jax.ShapeDtypeStruct((rows, d), BF16)],
        scratch_shapes=[pltpu.VMEM((tm, d), BF16)],
        compiler_params=_params("parallel"),
        name="merge",
    )(mixed, yh, proj, proj, x2, wf, wh, wo, g2)


def _mlp_kernel(h2_ref, x1_ref, w1_ref, w2_ref, gf_ref, o_ref, *, nj):
    j = pl.program_id(1)
    slice_rows = x1_ref.shape[0]

    def step(first):
        t = _dot(h2_ref[...], w1_ref[...])
        t = jnp.square(jnp.maximum(t, 0.0)).astype(BF16)
        part = _dot(t, w2_ref[...])
        if first:
            o_ref[...] = part
        else:
            o_ref[...] += part
        rows = pl.ds(pl.multiple_of(j * slice_rows, slice_rows), slice_rows)
        o_ref[rows, :] += x1_ref[...]

    @pl.when(j == 0)
    def _():
        step(True)

    @pl.when(j > 0)
    def _():
        step(False)

    @pl.when(j == nj - 1)
    def _():
        x = o_ref[...]
        ms = jnp.mean(x * x, axis=-1, keepdims=True)
        o_ref[...] = x * lax.rsqrt(ms + EPS) * gf_ref[...]


def _mlp(h2, x1, w1, w2, gf):
    rows = x1.shape[0]
    tm, tf = 1024, 1024
    d = D_MODEL
    nj = D_FF // tf
    return pl.pallas_call(
        functools.partial(_mlp_kernel, nj=nj),
        grid=(rows // tm, nj),
        in_specs=[
            pl.BlockSpec((tm, d), lambda i, j: (i, 0)),
            pl.BlockSpec((tm // nj, d), lambda i, j: (i * nj + j, 0)),
            pl.BlockSpec((d, tf), lambda i, j: (0, j)),
            pl.BlockSpec((tf, d), lambda i, j: (j, 0)),
            pl.BlockSpec((1, d), lambda i, j: (0, 0)),
        ],
        out_specs=pl.BlockSpec((tm, d), lambda i, j: (i, 0)),
        out_shape=jax.ShapeDtypeStruct((rows, d), F32),
        compiler_params=_params("parallel", "arbitrary"),
        name="mlp",
    )(h2, x1, w1, w2, gf)


def _layer_and_final_norm(x, w, filt):
    b, l, d = x.shape
    x2 = x.reshape(b * l, d)
    proj = _project(x2, w["norm1_g"], w["w_cat"], w["b_gate"])
    mixed = _fnet(proj, b, l)
    vx, x1h = _shortconv(proj.reshape(b, l, proj.shape[-1]), w["conv_w"], w["conv_b"])
    kf = _filter_spectrum(l, *filt)
    yh = _longconv(vx, x1h, kf, w["skip"])
    x1, h2 = _merge(mixed, yh, proj, x2, w["w_fnet_map"], w["w_hyena_out"], w["w_out"],
                    w["norm2_g"])
    out = _mlp(h2, x1, w["w_mlp1"], w["w_mlp2"], w["norm_f_g"])
    return out.reshape(b, l, d)


def kernel(x_prompt, x_sample, norm1_g, w_in, conv_w, conv_b, filt_w1, filt_b1, filt_w2, filt_b2, filt_w3, filt_b3, filt_w4, filt_freq, hyena_skip, w_fnet_map, w_hyena_out, w_gate, b_gate, w_out, norm2_g, w_mlp1, w_mlp2, norm_f_g):
    assert norm1_g.shape[0] == 1, "one layer"
    w = {
        "norm1_g": norm1_g[0][None],
        "w_cat": jnp.concatenate([w_in[0], w_gate[0]], axis=1).astype(BF16),
        "b_gate": b_gate[0][None],
        "conv_w": conv_w[0],
        "conv_b": conv_b[0][None],
        "skip": hyena_skip[0][None],
        "w_fnet_map": w_fnet_map[0].astype(BF16),
        "w_hyena_out": w_hyena_out[0].astype(BF16),
        "w_out": w_out[0].astype(BF16),
        "norm2_g": norm2_g[0][None],
        "w_mlp1": w_mlp1[0].astype(BF16),
        "w_mlp2": w_mlp2[0].astype(BF16),
        "norm_f_g": norm_f_g[None],
    }
    filt = (filt_w1[0], filt_b1[0], filt_w2[0], filt_b2[0], filt_w3[0], filt_b3[0],
            filt_w4[0], filt_freq[0])
    return (_layer_and_final_norm(x_prompt, w, filt),
            _layer_and_final_norm(x_sample, w, filt))
```

```python
import functools
import math

import jax
import jax.numpy as jnp
import numpy as np
from jax import lax
from jax.experimental import pallas as pl
from jax.experimental.pallas import tpu as pltpu

D_MODEL = 2048
N_FNET_GROUPS = 4
FNET_GROUP_DIM = 256
D_FNET = N_FNET_GROUPS * FNET_GROUP_DIM
D_HYENA = 1024
FILTER_BANDS = 16
FILTER_HIDDEN = 64
DECAY_TARGET = 1e-2
FAST_DECAY_PCT = 0.3
SLOW_DECAY_PCT = 1.5
D_FF = 4 * D_MODEL
EPS = 1e-6

F32 = jnp.float32
BF16 = jnp.bfloat16

_VMEM_LIMIT_BYTES = 56 * 1024 * 1024
_DFT_N2 = 128
_BF16_SUBLANES = 16
_ROW_GROUP = _BF16_SUBLANES

_dot = functools.partial(jnp.dot, preferred_element_type=F32)
_dot_hi = functools.partial(jnp.dot, preferred_element_type=F32,
                            precision=lax.Precision.HIGHEST)


def _dot_tn_bf16x3(a, b):
    dims = (((0,), (0,)), ((), ()))
    a_hi, b_hi = a.astype(BF16), b.astype(BF16)
    a_lo = (a - a_hi.astype(F32)).astype(BF16)
    b_lo = (b - b_hi.astype(F32)).astype(BF16)
    dg = functools.partial(lax.dot_general, dimension_numbers=dims, preferred_element_type=F32)
    return dg(a_hi, b_hi) + dg(a_lo, b_hi) + dg(a_hi, b_lo)


def _params(*sem):
    return pltpu.CompilerParams(dimension_semantics=sem,
                                vmem_limit_bytes=_VMEM_LIMIT_BYTES)


def _resident(shape):
    zeros = (0,) * len(shape)
    return pl.BlockSpec(shape, lambda *_: zeros, pipeline_mode=pl.Buffered(1))


def _cos_sin(n):
    jk = (np.arange(n)[:, None] * np.arange(n)[None, :]) % n
    ang = 2.0 * np.pi * jk / n
    return np.cos(ang), np.sin(ang)


@functools.lru_cache(maxsize=None)
def _dft_tables(n1, n2):
    n = n1 * n2
    c1, s1 = _cos_sin(n1)
    c2, s2 = _cos_sin(n2)
    fr1, fi1 = c1, -s1
    fr2, fi2 = c2, -s2
    ang = 2.0 * np.pi * ((np.arange(n1)[:, None] * np.arange(n2)[None, :]) % n) / n
    twr, twi = np.cos(ang), -np.sin(ang)
    t = {}
    h = n1 // 2
    t["rows_fwd"] = np.concatenate([fr1, fi1], axis=0)
    t["rows_fwd_c"] = np.block([[fr1[:, :h], -fi1[:, :h]], [fi1[:, :h], fr1[:, :h]]])
    t["p"] = np.block([[fr2, -fi2], [fi2, fr2]])
    t["q"] = np.block([[-fi2, -fr2], [fr2, -fi2]])
    t["tw_k1"] = np.stack([np.concatenate([twr, twr], axis=1),
                           np.concatenate([twi, twi], axis=1)], axis=1)
    t["inv2"] = np.block([[fr2, fi2], [-fi2, fr2]])
    t["pc"] = np.block([[fr1[:h], fi1[:h]], [-fi1[:h], fr1[:h]]])
    t["qc"] = np.block([[-fi1[:h], fr1[:h]], [-fr1[:h], -fi1[:h]]])
    t["tw_n2"] = np.stack([np.concatenate([twr.T, twr.T], axis=1),
                           np.concatenate([twi.T, twi.T], axis=1)], axis=1)
    return t


def _const(a, dtype=F32):
    return jnp.asarray(np.asarray(a, np.float32), dtype)


def _project_kernel(x_ref, g_ref, w_ref, b_ref, o_ref, h_ref, *, gate_tile, chunk):
    j = pl.program_id(1)

    @pl.when(j == 0)
    def _():
        x = x_ref[...]
        ms = jnp.mean(x * x, axis=-1, keepdims=True)
        h_ref[...] = (x * lax.rsqrt(ms + EPS) * g_ref[...]).astype(BF16)

    def emit(gated):
        for lo in range(0, o_ref.shape[1], chunk):
            acc = _dot(h_ref[...], w_ref[:, lo:lo + chunk])
            if gated:
                acc = jax.nn.sigmoid(acc + b_ref[:, lo:lo + chunk])
            o_ref[:, lo:lo + chunk] = acc.astype(BF16)

    @pl.when(j < gate_tile)
    def _():
        emit(False)

    @pl.when(j >= gate_tile)
    def _():
        emit(True)


def _project(x2, g, w_cat, b_gate):
    rows = x2.shape[0]
    n_out = w_cat.shape[1]
    tm, tn = 1024, 2048
    gate_tile = (n_out - b_gate.shape[1]) // tn
    return pl.pallas_call(
        functools.partial(_project_kernel, gate_tile=gate_tile, chunk=512),
        grid=(rows // tm, n_out // tn),
        in_specs=[
            pl.BlockSpec((tm, D_MODEL), lambda i, j: (i, 0)),
            pl.BlockSpec((1, D_MODEL), lambda i, j: (0, 0)),
            pl.BlockSpec((D_MODEL, tn), lambda i, j: (0, j)),
            pl.BlockSpec((1, tn), lambda i, j: (0, jnp.maximum(j - gate_tile, 0))),
        ],
        out_specs=pl.BlockSpec((tm, tn), lambda i, j: (i, j)),
        out_shape=jax.ShapeDtypeStruct((rows, n_out), BF16),
        scratch_shapes=[pltpu.VMEM((tm, D_MODEL), BF16)],
        compiler_params=_params("parallel", "arbitrary"),
        name="project",
    )(x2, g, w_cat, b_gate)


def _swap_major(x):
    return jnp.transpose(x, (1, 0, 2))


def _rows_fwd_kernel(lhs_ref, x_ref, o_ref):
    _, parts, k, sb, cb = x_ref.shape
    x = x_ref[0].astype(BF16).reshape(parts * k, sb, cb)
    xt = _swap_major(x)
    lhs = lhs_ref[...].astype(BF16)
    y = jnp.stack([_dot(lhs, xt[s]).astype(BF16) for s in range(sb)], axis=0)
    o_ref[0] = _swap_major(y).reshape(o_ref.shape[1:])


def _rows_fwd(lhs, x5, name, c=None):
    b, parts, k, n2, _ = x5.shape
    c = c or x5.shape[-1]
    n1 = lhs.shape[0] // 2
    sb, cb = _ROW_GROUP, 512
    return pl.pallas_call(
        _rows_fwd_kernel,
        grid=(b, n2 // sb, c // cb),
        in_specs=[_resident(lhs.shape),
                  pl.BlockSpec((1, parts, k, sb, cb), lambda bi, g, ci: (bi, 0, 0, g, ci))],
        out_specs=pl.BlockSpec((1, 2, n1, sb, cb), lambda bi, g, ci: (bi, 0, 0, g, ci)),
        out_shape=jax.ShapeDtypeStruct((b, 2, n1, n2, c), BF16),
        compiler_params=_params("parallel", "parallel", "parallel"),
        name=name,
    )(lhs, x5)


def _stage2_fwd(p_ref, q_ref, tw_ref, y_ref, i):
    g = (p_ref[...] * tw_ref[i, 0:1, :] + q_ref[...] * tw_ref[i, 1:2, :]).astype(BF16)
    ys = jnp.concatenate([y_ref[0, 0, i], y_ref[0, 1, i]], axis=0)
    return _dot(g, ys)


def _conv_mid_kernel(p_ref, q_ref, tw_ref, inv_ref, y_ref, kf_ref, o_ref, *, k1b, n2):
    for i in range(k1b):
        s = _stage2_fwd(p_ref, q_ref, tw_ref, y_ref, i)
        xr, xi = s[:n2], s[n2:]
        kr, ki = kf_ref[0, i].astype(F32), kf_ref[1, i].astype(F32)
        prod = jnp.concatenate([xr * kr - xi * ki, xr * ki + xi * kr], axis=0)
        z = _dot(inv_ref[...].astype(BF16), prod.astype(BF16))
        o_ref[0, 0, i] = z[:n2].astype(BF16)
        o_ref[0, 1, i] = z[n2:].astype(BF16)


def _filter_mid_kernel(p_ref, q_ref, tw_ref, y_ref, ss_ref, o_ref, *, k1b, n2):
    scale = lax.rsqrt(ss_ref[...] + EPS)
    for i in range(k1b):
        s = _stage2_fwd(p_ref, q_ref, tw_ref, y_ref, i)
        o_ref[0, i] = (s[:n2] * scale).astype(o_ref.dtype)
        o_ref[1, i] = (s[n2:] * scale).astype(o_ref.dtype)


def _fnet_mid_kernel(p_ref, q_ref, tw_ref, y_ref, cc_ref, sc_ref, o_ref, *, k1b, n2):
    cb = o_ref.shape[-1]
    cc = cc_ref[...].astype(BF16)
    sc = sc_ref[...].astype(BF16)
    urs, uis = [], []
    for i in range(k1b):
        s = _stage2_fwd(p_ref, q_ref, tw_ref, y_ref, i)
        urs.append(s[:n2].astype(BF16))
        uis.append(s[n2:].astype(BF16))
    ur = jnp.concatenate(urs, axis=0)
    ui = jnp.concatenate(uis, axis=0)
    groups = []
    for lo in range(0, cb, FNET_GROUP_DIM):
        hi = lo + FNET_GROUP_DIM
        groups.append((_dot(ur[:, lo:hi], cc) + _dot(ui[:, lo:hi], sc)).astype(BF16))
    mixed = jnp.concatenate(groups, axis=1).reshape(k1b, n2, cb)
    o_ref[0] = _swap_major(mixed)


def _stage2_specs(n2, c, k1b):
    return [
        _resident((2 * n2, 2 * n2)),
        _resident((2 * n2, 2 * n2)),
        pl.BlockSpec((k1b, 2, 2 * n2), lambda k, b, *_: (k, 0, 0)),
    ], pl.BlockSpec((1, 2, k1b, n2, c), lambda k, b, *_: (b, 0, k, 0, 0))


def _fnet(proj, b, l):
    n2 = _DFT_N2
    n1 = l // n2
    c = D_FNET
    t = _dft_tables(n1, n2)
    lhs = _const(t["rows_fwd"] / math.sqrt(l))
    y = _rows_fwd(lhs, proj.reshape(b, 1, n1, n2, proj.shape[-1]), "fnet_rows", c)
    cc, sc = _cos_sin(FNET_GROUP_DIM)
    norm = 1.0 / math.sqrt(FNET_GROUP_DIM)
    k1b, cb = _ROW_GROUP, 2 * FNET_GROUP_DIM
    head, _ = _stage2_specs(n2, c, k1b)
    out = pl.pallas_call(
        functools.partial(_fnet_mid_kernel, k1b=k1b, n2=n2),
        grid=(n1 // k1b, b, c // cb),
        in_specs=head + [pl.BlockSpec((1, 2, k1b, n2, cb), lambda k, bi, ci: (bi, 0, k, 0, ci)),
                         _resident((FNET_GROUP_DIM, FNET_GROUP_DIM)),
                         _resident((FNET_GROUP_DIM, FNET_GROUP_DIM))],
        out_specs=pl.BlockSpec((1, n2, k1b, cb), lambda k, bi, ci: (bi, 0, k, ci)),
        out_shape=jax.ShapeDtypeStruct((b, n2, n1, c), BF16),
        compiler_params=_params("parallel", "parallel", "parallel"),
        name="fnet_mid",
    )(_const(t["p"]), _const(t["q"]), _const(t["tw_k1"]), y,
      _const(cc * norm), _const(sc * norm))
    return out.reshape(b * l, c)


def _shortconv_kernel(*refs, r, nt):
    u_refs, prev_refs, next_refs = refs[0:3], refs[3:6], refs[6:9]
    w_ref, b_ref, vx_ref, x1_ref = refs[9:]
    i = pl.program_id(1)
    row = lax.broadcasted_iota(jnp.int32, (r, 1), 0)
    outs = []
    for part in range(3):
        cols = slice(part * D_HYENA, (part + 1) * D_HYENA)
        u = u_refs[part][0].astype(F32)
        prev_row = prev_refs[part][0].astype(F32)[_BF16_SUBLANES - 1:_BF16_SUBLANES]
        next_row = next_refs[part][0].astype(F32)[0:1]
        prev_row = jnp.where(i == 0, 0.0, prev_row)
        next_row = jnp.where(i == nt - 1, 0.0, next_row)
        u_prev = jnp.where(row == 0, prev_row, pltpu.roll(u, 1, axis=0))
        u_next = jnp.where(row == r - 1, next_row, pltpu.roll(u, r - 1, axis=0))
        outs.append(u_prev * w_ref[0:1, cols] + u * w_ref[1:2, cols]
                    + u_next * w_ref[2:3, cols] + b_ref[:, cols])
    x1, x2, v = outs
    vx_ref[0] = (v * x2).astype(BF16)
    x1_ref[0] = x1.astype(BF16)


def _shortconv(proj3, conv_w, conv_b):
    b, l, _ = proj3.shape
    c = D_HYENA
    r = 512
    nt = l // r
    hb = _BF16_SUBLANES
    halo = r // hb
    first_block = D_FNET // c
    tiles = [pl.BlockSpec((1, r, c), lambda bi, i, p=p: (bi, i, first_block + p))
             for p in range(3)]
    prevs = [pl.BlockSpec((1, hb, c),
                          lambda bi, i, p=p: (bi, jnp.maximum(i * halo - 1, 0), first_block + p))
             for p in range(3)]
    nexts = [pl.BlockSpec((1, hb, c),
                          lambda bi, i, p=p: (bi, jnp.minimum((i + 1) * halo, l // hb - 1),
                                              first_block + p))
             for p in range(3)]
    return pl.pallas_call(
        functools.partial(_shortconv_kernel, r=r, nt=nt),
        grid=(b, nt),
        in_specs=tiles + prevs + nexts + [_resident((3, 3 * c)), _resident((1, 3 * c))],
        out_specs=[pl.BlockSpec((1, r, c), lambda bi, i: (bi, i, 0))] * 2,
        out_shape=[jax.ShapeDtypeStruct((b, l, c), BF16)] * 2,
        compiler_params=_params("parallel", "parallel"),
        name="shortconv",
    )(*([proj3] * 9), conv_w, conv_b)


def _filter_kernel(bands_ref, w1t_ref, w1c_ref, w1s_ref, b1_ref, w2_ref, b2_ref,
                   w3_ref, b3_ref, fr_ref, w4_ref, deltas_ref, k_ref, ss_ref, *, r, l):
    i = pl.program_id(0)
    base = i * r

    def lags(shape, axis):
        row = base + lax.broadcasted_iota(jnp.int32, shape, axis)
        return row, jnp.where(row < l, row, 2 * l - row).astype(F32)

    _, lag = lags((1, r), 1)
    ang = bands_ref[...] * ((2.0 * math.pi / l) * lag)
    fr = fr_ref[...]
    pre = (w1t_ref[...] * (lag * (1.0 / (l - 1))) + _dot_hi(w1c_ref[...], jnp.cos(ang))
           - _dot_hi(w1s_ref[...], jnp.sin(ang)) + b1_ref[...])
    h = jnp.sin(fr * pre)
    h = jnp.sin(fr * (_dot_hi(w2_ref[...], h) + b2_ref[...]))
    h = jnp.sin(fr * (_dot_hi(w3_ref[...], h) + b3_ref[...]))
    row, lag_col = lags((r, 1), 0)
    decay = jnp.exp(-(lag_col * (1.0 / (l - 1))) * deltas_ref[...])
    k = jnp.where(row == l, 0.0, _dot_tn_bf16x3(h, w4_ref[...]) * decay)
    k_ref[...] = k.astype(k_ref.dtype)

    @pl.when(i == 0)
    def _():
        ss_ref[...] = jnp.zeros_like(ss_ref)

    ss_ref[...] += jnp.sum(k * k, axis=0, keepdims=True)


def _filter_spectrum(l, w1, b1, w2, b2, w3, b3, w4, freq):
    n = 2 * l
    n2 = _DFT_N2
    n1 = n // n2
    c = D_HYENA
    r = 1024
    tiles_per_half = l // r
    bands = np.linspace(1e-4, FILTER_BANDS - 1, FILTER_BANDS)[:, None]
    deltas = np.abs(np.linspace(math.log(DECAY_TARGET) / FAST_DECAY_PCT,
                                math.log(DECAY_TARGET) / SLOW_DECAY_PCT, D_HYENA))[None, :]
    small = [_const(bands), w1[0:1].T, w1[1:1 + FILTER_BANDS].T, w1[1 + FILTER_BANDS:].T,
             b1[:, None], w2.T, b2[:, None], w3.T, b3[:, None], freq[:, None]]
    k, ss = pl.pallas_call(
        functools.partial(_filter_kernel, r=r, l=l),
        grid=(n // r,),
        in_specs=[_resident(s.shape) for s in small] + [
            pl.BlockSpec((FILTER_HIDDEN, c), lambda i: (0, i // tiles_per_half)),
            _resident((1, c))],
        out_specs=[pl.BlockSpec((r, c), lambda i: (i, 0)),
                   pl.BlockSpec((1, c), lambda i: (0, 0))],
        out_shape=[jax.ShapeDtypeStruct((n, c), BF16),
                   jax.ShapeDtypeStruct((1, c), F32)],
        compiler_params=_params("arbitrary"),
        name="hyena_filter",
    )(*small, w4, _const(deltas))

    t = _dft_tables(n1, n2)
    y = _rows_fwd(_const(t["rows_fwd"]), k.reshape(1, 1, n1, n2, c), "filter_rows")
    k1b = 8
    head, y_spec = _stage2_specs(n2, c, k1b)
    return pl.pallas_call(
        functools.partial(_filter_mid_kernel, k1b=k1b, n2=n2),
        grid=(n1 // k1b, 1),
        in_specs=head + [y_spec, _resident((1, c))],
        out_specs=pl.BlockSpec((2, k1b, n2, c), lambda k, bi: (0, k, 0, 0)),
        out_shape=jax.ShapeDtypeStruct((2, n1, n2, c), BF16),
        compiler_params=_params("parallel", "parallel"),
        name="filter_mid",
    )(_const(t["p"]), _const(t["q"]), _const(t["tw_k1"]), y, ss)


def _conv_out_kernel(pc_ref, qc_ref, tw_ref, z_ref, vx_ref, x1_ref, skip_ref, o_ref):
    _, _, n1, sb, cb = z_ref.shape
    z = _swap_major(z_ref[0].reshape(2 * n1, sb, cb))
    ys = []
    for s in range(sb):
        lhs = (pc_ref[...] * tw_ref[s, 0:1, :] + qc_ref[...] * tw_ref[s, 1:2, :]).astype(BF16)
        ys.append(_dot(lhs, z[s]))
    y = _swap_major(jnp.stack(ys, axis=0)).reshape(o_ref.shape)
    y = y + vx_ref[...].astype(F32) * skip_ref[...]
    o_ref[...] = (y * x1_ref[...].astype(F32)).astype(BF16)


def _longconv(vx, x1, kf, skip):
    b, l, c = vx.shape
    assert b == 2, "batch pair is packed as (re, im)"
    n = 2 * l
    n2 = _DFT_N2
    n1 = n // n2
    half = n1 // 2
    t = _dft_tables(n1, n2)
    y = _rows_fwd(_const(t["rows_fwd_c"]), vx.reshape(1, 2, half, n2, c), "conv_rows")
    k1b = 8
    head, y_spec = _stage2_specs(n2, c, k1b)
    z = pl.pallas_call(
        functools.partial(_conv_mid_kernel, k1b=k1b, n2=n2),
        grid=(n1 // k1b, 1),
        in_specs=head + [_resident((2 * n2, 2 * n2)), y_spec,
                         pl.BlockSpec((2, k1b, n2, c), lambda k, bi: (0, k, 0, 0))],
        out_specs=y_spec,
        out_shape=jax.ShapeDtypeStruct((1, 2, n1, n2, c), BF16),
        compiler_params=_params("parallel", "parallel"),
        name="conv_mid",
    )(_const(t["p"]), _const(t["q"]), _const(t["tw_k1"]), _const(t["inv2"] / n), y, kf)

    sb, cb = _ROW_GROUP, 512
    seq_spec = pl.BlockSpec((2, half, sb, cb), lambda g, ci: (0, 0, g, ci))
    out = pl.pallas_call(
        _conv_out_kernel,
        grid=(n2 // sb, c // cb),
        in_specs=[_resident((n1, 2 * n1)), _resident((n1, 2 * n1)),
                  pl.BlockSpec((sb, 2, 2 * n1), lambda g, ci: (g, 0, 0)),
                  pl.BlockSpec((1, 2, n1, sb, cb), lambda g, ci: (0, 0, 0, g, ci)),
                  seq_spec, seq_spec,
                  pl.BlockSpec((1, cb), lambda g, ci: (0, ci))],
        out_specs=seq_spec,
        out_shape=jax.ShapeDtypeStruct((2, half, n2, c), BF16),
        compiler_params=_params("parallel", "parallel"),
        name="conv_out",
    )(_const(t["pc"]), _const(t["qc"]), _const(t["tw_n2"]), z,
      vx.reshape(2, half, n2, c), x1.reshape(2, half, n2, c), skip)
    return out.reshape(b * l, c)


def _merge_kernel(mx_ref, yh_ref, ga_ref, gb_ref, x_ref, wf_ref, wh_ref, wo_ref, g2_ref,
                  x1_ref, h2_ref, m_ref, *, chunk):
    d = x_ref.shape[1]
    for lo in range(0, d, chunk):
        cols = slice(lo, lo + chunk)
        ya = _dot(mx_ref[...], wf_ref[:, cols])
        yb = _dot(yh_ref[...], wh_ref[:, cols])
        m_ref[:, cols] = (ga_ref[:, cols].astype(F32) * ya
                          + gb_ref[:, cols].astype(F32) * yb).astype(BF16)
    ss = jnp.zeros((x_ref.shape[0], 1), F32)
    for lo in range(0, d, chunk):
        cols = slice(lo, lo + chunk)
        x1 = x_ref[:, cols] + _dot(m_ref[...], wo_ref[:, cols])
        x1_ref[:, cols] = x1
        ss = ss + jnp.sum(x1 * x1, axis=-1, keepdims=True)
    scale = lax.rsqrt(ss * (1.0 / d) + EPS)
    for lo in range(0, d, chunk):
        cols = slice(lo, lo + chunk)
        h2_ref[:, cols] = (x1_ref[:, cols] * scale * g2_ref[:, cols]).astype(BF16)


def _merge(mixed, yh, proj, x2, wf, wh, wo, g2):
    rows = x2.shape[0]
    tm = 512
    d = D_MODEL
    gate_block = (proj.shape[1] - 2 * d) // d
    return pl.pallas_call(
        functools.partial(_merge_kernel, chunk=512),
        grid=(rows // tm,),
        in_specs=[
            pl.BlockSpec((tm, D_FNET), lambda i: (i, 0)),
            pl.BlockSpec((tm, D_HYENA), lambda i: (i, 0)),
            pl.BlockSpec((tm, d), lambda i: (i, gate_block)),
            pl.BlockSpec((tm, d), lambda i: (i, gate_block + 1)),
            pl.BlockSpec((tm, d), lambda i: (i, 0)),
            _resident((D_FNET, d)), _resident((D_HYENA, d)), _resident((d, d)),
            _resident((1, d)),
        ],
        out_specs=[pl.BlockSpec((tm, d), lambda i: (i, 0))] * 2,
        out_shape=[jax.ShapeDtypeStruct((rows, d), F32),
                   jax.ShapeDtypeStruct((rows, d), BF16)],
        scratch_shapes=[pltpu.VMEM((tm, d), BF16)],
        compiler_params=_params("parallel"),
        name="merge",
    )(mixed, yh, proj, proj, x2, wf, wh, wo, g2)


def _mlp_kernel(h2_ref, x1_ref, w1_ref, w2_ref, gf_ref, o_ref, *, nj):
    j = pl.program_id(1)
    slice_rows = x1_ref.shape[0]

    def step(first):
        t = _dot(h2_ref[...], w1_ref[...])
        t = jnp.square(jnp.maximum(t, 0.0)).astype(BF16)
        part = _dot(t, w2_ref[...])
        if first:
            o_ref[...] = part
        else:
            o_ref[...] += part
        rows = pl.ds(pl.multiple_of(j * slice_rows, slice_rows), slice_rows)
        o_ref[rows, :] += x1_ref[...]

    @pl.when(j == 0)
    def _():
        step(True)

    @pl.when(j > 0)
    def _():
        step(False)

    @pl.when(j == nj - 1)
    def _():
        x = o_ref[...]
        ms = jnp.mean(x * x, axis=-1, keepdims=True)
        o_ref[...] = x * lax.rsqrt(ms + EPS) * gf_ref[...]


def _mlp(h2, x1, w1, w2, gf):
    rows = x1.shape[0]
    tm, tf = 1024, 1024
    d = D_MODEL
    nj = D_FF // tf
    return pl.pallas_call(
        functools.partial(_mlp_kernel, nj=nj),
        grid=(rows // tm, nj),
        in_specs=[
            pl.BlockSpec((tm, d), lambda i, j: (i, 0)),
            pl.BlockSpec((tm // nj, d), lambda i, j: (i * nj + j, 0)),
            pl.BlockSpec((d, tf), lambda i, j: (0, j)),
            pl.BlockSpec((tf, d), lambda i, j: (j, 0)),
            pl.BlockSpec((1, d), lambda i, j: (0, 0)),
        ],
        out_specs=pl.BlockSpec((tm, d), lambda i, j: (i, 0)),
        out_shape=jax.ShapeDtypeStruct((rows, d), F32),
        compiler_params=_params("parallel", "arbitrary"),
        name="mlp",
    )(h2, x1, w1, w2, gf)


def _layer_and_final_norm(x, w, filt):
    b, l, d = x.shape
    x2 = x.reshape(b * l, d)
    proj = _project(x2, w["norm1_g"], w["w_cat"], w["b_gate"])
    mixed = _fnet(proj, b, l)
    vx, x1h = _shortconv(proj.reshape(b, l, proj.shape[-1]), w["conv_w"], w["conv_b"])
    kf = _filter_spectrum(l, *filt)
    yh = _longconv(vx, x1h, kf, w["skip"])
    x1, h2 = _merge(mixed, yh, proj, x2, w["w_fnet_map"], w["w_hyena_out"], w["w_out"],
                    w["norm2_g"])
    out = _mlp(h2, x1, w["w_mlp1"], w["w_mlp2"], w["norm_f_g"])
    return out.reshape(b, l, d)


def kernel(x_prompt, x_sample, norm1_g, w_in, conv_w, conv_b, filt_w1, filt_b1, filt_w2, filt_b2, filt_w3, filt_b3, filt_w4, filt_freq, hyena_skip, w_fnet_map, w_hyena_out, w_gate, b_gate, w_out, norm2_g, w_mlp1, w_mlp2, norm_f_g):
    assert norm1_g.shape[0] == 1, "one layer"
    w = {
        "norm1_g": norm1_g[0][None],
        "w_cat": jnp.concatenate([w_in[0], w_gate[0]], axis=1).astype(BF16),
        "b_gate": b_gate[0][None],
        "conv_w": conv_w[0],
        "conv_b": conv_b[0][None],
        "skip": hyena_skip[0][None],
        "w_fnet_map": w_fnet_map[0].astype(BF16),
        "w_hyena_out": w_hyena_out[0].astype(BF16),
        "w_out": w_out[0].astype(BF16),
        "norm2_g": norm2_g[0][None],
        "w_mlp1": w_mlp1[0].astype(BF16),
        "w_mlp2": w_mlp2[0].astype(BF16),
        "norm_f_g": norm_f_g[None],
    }
    filt = (filt_w1[0], filt_b1[0], filt_w2[0], filt_b2[0], filt_w3[0], filt_b3[0],
            filt_w4[0], filt_freq[0])
    return (_layer_and_final_norm(x_prompt, w, filt),
            _layer_and_final_norm(x_sample, w, filt))
```

```python
import functools
import math

import jax
import jax.numpy as jnp
import numpy as np
from jax import lax
from jax.experimental import pallas as pl
from jax.experimental.pallas import tpu as pltpu

D_MODEL = 2048
N_FNET_GROUPS = 4
FNET_GROUP_DIM = 256
D_FNET = N_FNET_GROUPS * FNET_GROUP_DIM
D_HYENA = 1024
FILTER_BANDS = 16
FILTER_HIDDEN = 64
DECAY_TARGET = 1e-2
FAST_DECAY_PCT = 0.3
SLOW_DECAY_PCT = 1.5
D_FF = 4 * D_MODEL
EPS = 1e-6

F32 = jnp.float32
BF16 = jnp.bfloat16

_VMEM_LIMIT_BYTES = 56 * 1024 * 1024
_DFT_N2 = 128
_BF16_SUBLANES = 16
_ROW_GROUP = _BF16_SUBLANES

_dot = functools.partial(jnp.dot, preferred_element_type=F32)
_dot_hi = functools.partial(jnp.dot, preferred_element_type=F32,
                            precision=lax.Precision.HIGHEST)


def _dot_tn_bf16x3(a, b):
    dims = (((0,), (0,)), ((), ()))
    a_hi, b_hi = a.astype(BF16), b.astype(BF16)
    a_lo = (a - a_hi.astype(F32)).astype(BF16)
    b_lo = (b - b_hi.astype(F32)).astype(BF16)
    dg = functools.partial(lax.dot_general, dimension_numbers=dims, preferred_element_type=F32)
    return dg(a_hi, b_hi) + dg(a_lo, b_hi) + dg(a_hi, b_lo)


def _params(*sem):
    return pltpu.CompilerParams(dimension_semantics=sem,
                                vmem_limit_bytes=_VMEM_LIMIT_BYTES)


def _resident(shape):
    zeros = (0,) * len(shape)
    return pl.BlockSpec(shape, lambda *_: zeros, pipeline_mode=pl.Buffered(1))


def _cos_sin(n):
    jk = (np.arange(n)[:, None] * np.arange(n)[None, :]) % n
    ang = 2.0 * np.pi * jk / n
    return np.cos(ang), np.sin(ang)


@functools.lru_cache(maxsize=None)
def _dft_tables(n1, n2):
    n = n1 * n2
    c1, s1 = _cos_sin(n1)
    c2, s2 = _cos_sin(n2)
    fr1, fi1 = c1, -s1
    fr2, fi2 = c2, -s2
    ang = 2.0 * np.pi * ((np.arange(n1)[:, None] * np.arange(n2)[None, :]) % n) / n
    twr, twi = np.cos(ang), -np.sin(ang)
    t = {}
    h = n1 // 2
    t["rows_fwd"] = np.concatenate([fr1, fi1], axis=0)
    t["rows_fwd_c"] = np.block([[fr1[:, :h], -fi1[:, :h]], [fi1[:, :h], fr1[:, :h]]])
    t["p"] = np.block([[fr2, -fi2], [fi2, fr2]])
    t["q"] = np.block([[-fi2, -fr2], [fr2, -fi2]])
    t["tw_k1"] = np.stack([np.concatenate([twr, twr], axis=1),
                           np.concatenate([twi, twi], axis=1)], axis=1)
    t["inv2"] = np.block([[fr2, fi2], [-fi2, fr2]])
    t["pc"] = np.block([[fr1[:h], fi1[:h]], [-fi1[:h], fr1[:h]]])
    t["qc"] = np.block([[-fi1[:h], fr1[:h]], [-fr1[:h], -fi1[:h]]])
    t["tw_n2"] = np.stack([np.concatenate([twr.T, twr.T], axis=1),
                           np.concatenate([twi.T, twi.T], axis=1)], axis=1)
    return t


def _const(a, dtype=F32):
    return jnp.asarray(np.asarray(a, np.float32), dtype)


def _project_kernel(x_ref, g_ref, w_ref, b_ref, o_ref, h_ref, *, gate_tile, chunk):
    j = pl.program_id(1)
    tm = x_ref.shape[0]

    def emit(gated, rows):
        for lo in range(0, o_ref.shape[1], chunk):
            acc = _dot(h_ref[rows, :], w_ref[:, lo:lo + chunk])
            if gated:
                acc = jax.nn.sigmoid(acc + b_ref[:, lo:lo + chunk])
            o_ref[rows, lo:lo + chunk] = acc.astype(BF16)

    @pl.when(j == 0)
    def _():
        for r0 in range(0, tm, tm // 2):
            rows = slice(r0, r0 + tm // 2)
            x = x_ref[rows, :]
            ms = jnp.mean(x * x, axis=-1, keepdims=True)
            h_ref[rows, :] = (x * lax.rsqrt(ms + EPS) * g_ref[...]).astype(BF16)
            emit(False, rows)

    @pl.when(jnp.logical_and(j > 0, j < gate_tile))
    def _():
        emit(False, slice(None))

    @pl.when(j >= gate_tile)
    def _():
        emit(True, slice(None))


def _project(x2, g, w_cat, b_gate):
    rows = x2.shape[0]
    n_out = w_cat.shape[1]
    tm, tn = 1024, 2048
    gate_tile = (n_out - b_gate.shape[1]) // tn
    assert gate_tile >= 1, "the first column tile of a row tile is not gated"
    return pl.pallas_call(
        functools.partial(_project_kernel, gate_tile=gate_tile, chunk=512),
        grid=(rows // tm, n_out // tn),
        in_specs=[
            pl.BlockSpec((tm, D_MODEL), lambda i, j: (i, 0)),
            pl.BlockSpec((1, D_MODEL), lambda i, j: (0, 0)),
            pl.BlockSpec((D_MODEL, tn), lambda i, j: (0, j)),
            pl.BlockSpec((1, tn), lambda i, j: (0, jnp.maximum(j - gate_tile, 0))),
        ],
        out_specs=pl.BlockSpec((tm, tn), lambda i, j: (i, j)),
        out_shape=jax.ShapeDtypeStruct((rows, n_out), BF16),
        scratch_shapes=[pltpu.VMEM((tm, D_MODEL), BF16)],
        compiler_params=_params("parallel", "arbitrary"),
        name="project",
    )(x2, g, w_cat, b_gate)


def _swap_major(x):
    return jnp.transpose(x, (1, 0, 2))


def _rows_fwd_kernel(lhs_ref, x_ref, o_ref):
    _, parts, k, sb, cb = x_ref.shape
    x = x_ref[0].astype(BF16).reshape(parts * k, sb, cb)
    xt = _swap_major(x)
    lhs = lhs_ref[...].astype(BF16)
    y = jnp.stack([_dot(lhs, xt[s]).astype(BF16) for s in range(sb)], axis=0)
    o_ref[0] = _swap_major(y).reshape(o_ref.shape[1:])


def _rows_fwd(lhs, x5, name, c=None):
    b, parts, k, n2, _ = x5.shape
    c = c or x5.shape[-1]
    n1 = lhs.shape[0] // 2
    sb, cb = _ROW_GROUP, 512
    return pl.pallas_call(
        _rows_fwd_kernel,
        grid=(b, n2 // sb, c // cb),
        in_specs=[_resident(lhs.shape),
                  pl.BlockSpec((1, parts, k, sb, cb), lambda bi, g, ci: (bi, 0, 0, g, ci))],
        out_specs=pl.BlockSpec((1, 2, n1, sb, cb), lambda bi, g, ci: (bi, 0, 0, g, ci)),
        out_shape=jax.ShapeDtypeStruct((b, 2, n1, n2, c), BF16),
        compiler_params=_params("parallel", "parallel", "parallel"),
        name=name,
    )(lhs, x5)


def _stage2_fwd(p_ref, q_ref, tw_ref, y_ref, i):
    g = (p_ref[...] * tw_ref[i, 0:1, :] + q_ref[...] * tw_ref[i, 1:2, :]).astype(BF16)
    ys = jnp.concatenate([y_ref[0, 0, i], y_ref[0, 1, i]], axis=0)
    return _dot(g, ys)


def _conv_mid_kernel(p_ref, q_ref, tw_ref, inv_ref, y_ref, kf_ref, o_ref, *, k1b, n2):
    for i in range(k1b):
        s = _stage2_fwd(p_ref, q_ref, tw_ref, y_ref, i)
        xr, xi = s[:n2], s[n2:]
        kr, ki = kf_ref[0, i].astype(F32), kf_ref[1, i].astype(F32)
        prod = jnp.concatenate([xr * kr - xi * ki, xr * ki + xi * kr], axis=0)
        z = _dot(inv_ref[...].astype(BF16), prod.astype(BF16))
        o_ref[0, 0, i] = z[:n2].astype(BF16)
        o_ref[0, 1, i] = z[n2:].astype(BF16)


def _filter_mid_kernel(p_ref, q_ref, tw_ref, y_ref, ss_ref, o_ref, *, k1b, n2):
    scale = lax.rsqrt(ss_ref[...] + EPS)
    for i in range(k1b):
        s = _stage2_fwd(p_ref, q_ref, tw_ref, y_ref, i)
        o_ref[0, i] = (s[:n2] * scale).astype(o_ref.dtype)
        o_ref[1, i] = (s[n2:] * scale).astype(o_ref.dtype)


def _fnet_mid_kernel(p_ref, q_ref, tw_ref, y_ref, cc_ref, sc_ref, o_ref, *, k1b, n2):
    cb = o_ref.shape[-1]
    cc = cc_ref[...].astype(BF16)
    sc = sc_ref[...].astype(BF16)
    urs, uis = [], []
    for i in range(k1b):
        s = _stage2_fwd(p_ref, q_ref, tw_ref, y_ref, i)
        urs.append(s[:n2].astype(BF16))
        uis.append(s[n2:].astype(BF16))
    ur = jnp.concatenate(urs, axis=0)
    ui = jnp.concatenate(uis, axis=0)
    groups = []
    for lo in range(0, cb, FNET_GROUP_DIM):
        hi = lo + FNET_GROUP_DIM
        groups.append((_dot(ur[:, lo:hi], cc) + _dot(ui[:, lo:hi], sc)).astype(BF16))
    mixed = jnp.concatenate(groups, axis=1).reshape(k1b, n2, cb)
    o_ref[0] = _swap_major(mixed)


def _stage2_specs(n2, c, k1b):
    return [
        _resident((2 * n2, 2 * n2)),
        _resident((2 * n2, 2 * n2)),
        pl.BlockSpec((k1b, 2, 2 * n2), lambda k, b, *_: (k, 0, 0)),
    ], pl.BlockSpec((1, 2, k1b, n2, c), lambda k, b, *_: (b, 0, k, 0, 0))


def _fnet(proj, b, l):
    n2 = _DFT_N2
    n1 = l // n2
    c = D_FNET
    t = _dft_tables(n1, n2)
    lhs = _const(t["rows_fwd"] / math.sqrt(l))
    y = _rows_fwd(lhs, proj.reshape(b, 1, n1, n2, proj.shape[-1]), "fnet_rows", c)
    cc, sc = _cos_sin(FNET_GROUP_DIM)
    norm = 1.0 / math.sqrt(FNET_GROUP_DIM)
    k1b, cb = _ROW_GROUP, 2 * FNET_GROUP_DIM
    head, _ = _stage2_specs(n2, c, k1b)
    out = pl.pallas_call(
        functools.partial(_fnet_mid_kernel, k1b=k1b, n2=n2),
        grid=(n1 // k1b, b, c // cb),
        in_specs=head + [pl.BlockSpec((1, 2, k1b, n2, cb), lambda k, bi, ci: (bi, 0, k, 0, ci)),
                         _resident((FNET_GROUP_DIM, FNET_GROUP_DIM)),
                         _resident((FNET_GROUP_DIM, FNET_GROUP_DIM))],
        out_specs=pl.BlockSpec((1, n2, k1b, cb), lambda k, bi, ci: (bi, 0, k, ci)),
        out_shape=jax.ShapeDtypeStruct((b, n2, n1, c), BF16),
        compiler_params=_params("parallel", "parallel", "parallel"),
        name="fnet_mid",
    )(_const(t["p"]), _const(t["q"]), _const(t["tw_k1"]), y,
      _const(cc * norm), _const(sc * norm))
    return out.reshape(b * l, c)


def _shortconv_kernel(*refs, r, nt):
    u_refs, prev_refs, next_refs = refs[0:3], refs[3:6], refs[6:9]
    w_ref, b_ref, vx_ref, x1_ref = refs[9:]
    i = pl.program_id(1)
    row = lax.broadcasted_iota(jnp.int32, (r, 1), 0)
    outs = []
    for part in range(3):
        cols = slice(part * D_HYENA, (part + 1) * D_HYENA)
        u = u_refs[part][0].astype(F32)
        prev_row = prev_refs[part][0].astype(F32)[_BF16_SUBLANES - 1:_BF16_SUBLANES]
        next_row = next_refs[part][0].astype(F32)[0:1]
        prev_row = jnp.where(i == 0, 0.0, prev_row)
        next_row = jnp.where(i == nt - 1, 0.0, next_row)
        u_prev = jnp.where(row == 0, prev_row, pltpu.roll(u, 1, axis=0))
        u_next = jnp.where(row == r - 1, next_row, pltpu.roll(u, r - 1, axis=0))
        outs.append(u_prev * w_ref[0:1, cols] + u * w_ref[1:2, cols]
                    + u_next * w_ref[2:3, cols] + b_ref[:, cols])
    x1, x2, v = outs
    vx_ref[0] = (v * x2).astype(BF16)
    x1_ref[0] = x1.astype(BF16)


def _shortconv(proj3, conv_w, conv_b):
    b, l, _ = proj3.shape
    c = D_HYENA
    r = 512
    nt = l // r
    hb = _BF16_SUBLANES
    halo = r // hb
    first_block = D_FNET // c
    tiles = [pl.BlockSpec((1, r, c), lambda bi, i, p=p: (bi, i, first_block + p))
             for p in range(3)]
    prevs = [pl.BlockSpec((1, hb, c),
                          lambda bi, i, p=p: (bi, jnp.maximum(i * halo - 1, 0), first_block + p))
             for p in range(3)]
    nexts = [pl.BlockSpec((1, hb, c),
                          lambda bi, i, p=p: (bi, jnp.minimum((i + 1) * halo, l // hb - 1),
                                              first_block + p))
             for p in range(3)]
    return pl.pallas_call(
        functools.partial(_shortconv_kernel, r=r, nt=nt),
        grid=(b, nt),
        in_specs=tiles + prevs + nexts + [_resident((3, 3 * c)), _resident((1, 3 * c))],
        out_specs=[pl.BlockSpec((1, r, c), lambda bi, i: (bi, i, 0))] * 2,
        out_shape=[jax.ShapeDtypeStruct((b, l, c), BF16)] * 2,
        compiler_params=_params("parallel", "parallel"),
        name="shortconv",
    )(*([proj3] * 9), conv_w, conv_b)


def _filter_kernel(bands_ref, w1t_ref, w1c_ref, w1s_ref, b1_ref, w2_ref, b2_ref,
                   w3_ref, b3_ref, fr_ref, w4_ref, deltas_ref, k_ref, ss_ref, *, r, l):
    i = pl.program_id(0)
    base = i * r

    def lags(shape, axis):
        row = base + lax.broadcasted_iota(jnp.int32, shape, axis)
        return row, jnp.where(row < l, row, 2 * l - row).astype(F32)

    _, lag = lags((1, r), 1)
    ang = bands_ref[...] * ((2.0 * math.pi / l) * lag)
    fr = fr_ref[...]
    pre = (w1t_ref[...] * (lag * (1.0 / (l - 1))) + _dot_hi(w1c_ref[...], jnp.cos(ang))
           - _dot_hi(w1s_ref[...], jnp.sin(ang)) + b1_ref[...])
    h = jnp.sin(fr * pre)
    h = jnp.sin(fr * (_dot_hi(w2_ref[...], h) + b2_ref[...]))
    h = jnp.sin(fr * (_dot_hi(w3_ref[...], h) + b3_ref[...]))
    row, lag_col = lags((r, 1), 0)
    decay = jnp.exp(-(lag_col * (1.0 / (l - 1))) * deltas_ref[...])
    k = jnp.where(row == l, 0.0, _dot_tn_bf16x3(h, w4_ref[...]) * decay)
    k_ref[...] = k.astype(k_ref.dtype)

    @pl.when(i == 0)
    def _():
        ss_ref[...] = jnp.zeros_like(ss_ref)

    ss_ref[...] += jnp.sum(k * k, axis=0, keepdims=True)


def _filter_spectrum(l, w1, b1, w2, b2, w3, b3, w4, freq):
    n = 2 * l
    n2 = _DFT_N2
    n1 = n // n2
    c = D_HYENA
    r = 1024
    tiles_per_half = l // r
    bands = np.linspace(1e-4, FILTER_BANDS - 1, FILTER_BANDS)[:, None]
    deltas = np.abs(np.linspace(math.log(DECAY_TARGET) / FAST_DECAY_PCT,
                                math.log(DECAY_TARGET) / SLOW_DECAY_PCT, D_HYENA))[None, :]
    small = [_const(bands), w1[0:1].T, w1[1:1 + FILTER_BANDS].T, w1[1 + FILTER_BANDS:].T,
             b1[:, None], w2.T, b2[:, None], w3.T, b3[:, None], freq[:, None]]
    k, ss = pl.pallas_call(
        functools.partial(_filter_kernel, r=r, l=l),
        grid=(n // r,),
        in_specs=[_resident(s.shape) for s in small] + [
            pl.BlockSpec((FILTER_HIDDEN, c), lambda i: (0, i // tiles_per_half)),
            _resident((1, c))],
        out_specs=[pl.BlockSpec((r, c), lambda i: (i, 0)),
                   pl.BlockSpec((1, c), lambda i: (0, 0))],
        out_shape=[jax.ShapeDtypeStruct((n, c), BF16),
                   jax.ShapeDtypeStruct((1, c), F32)],
        compiler_params=_params("arbitrary"),
        name="hyena_filter",
    )(*small, w4, _const(deltas))

    t = _dft_tables(n1, n2)
    y = _rows_fwd(_const(t["rows_fwd"]), k.reshape(1, 1, n1, n2, c), "filter_rows")
    k1b = 8
    head, y_spec = _stage2_specs(n2, c, k1b)
    return pl.pallas_call(
        functools.partial(_filter_mid_kernel, k1b=k1b, n2=n2),
        grid=(n1 // k1b, 1),
        in_specs=head + [y_spec, _resident((1, c))],
        out_specs=pl.BlockSpec((2, k1b, n2, c), lambda k, bi: (0, k, 0, 0)),
        out_shape=jax.ShapeDtypeStruct((2, n1, n2, c), BF16),
        compiler_params=_params("parallel", "parallel"),
        name="filter_mid",
    )(_const(t["p"]), _const(t["q"]), _const(t["tw_k1"]), y, ss)


def _conv_out_kernel(pc_ref, qc_ref, tw_ref, z_ref, vx_ref, x1_ref, skip_ref, o_ref):
    _, _, n1, sb, cb = z_ref.shape
    z = _swap_major(z_ref[0].reshape(2 * n1, sb, cb))
    ys = []
    for s in range(sb):
        lhs = (pc_ref[...] * tw_ref[s, 0:1, :] + qc_ref[...] * tw_ref[s, 1:2, :]).astype(BF16)
        ys.append(_dot(lhs, z[s]))
    y = _swap_major(jnp.stack(ys, axis=0)).reshape(o_ref.shape)
    y = y + vx_ref[...].astype(F32) * skip_ref[...]
    o_ref[...] = (y * x1_ref[...].astype(F32)).astype(BF16)


def _longconv(vx, x1, kf, skip):
    b, l, c = vx.shape
    assert b == 2, "batch pair is packed as (re, im)"
    n = 2 * l
    n2 = _DFT_N2
    n1 = n // n2
    half = n1 // 2
    t = _dft_tables(n1, n2)
    y = _rows_fwd(_const(t["rows_fwd_c"]), vx.reshape(1, 2, half, n2, c), "conv_rows")
    k1b = 8
    head, y_spec = _stage2_specs(n2, c, k1b)
    z = pl.pallas_call(
        functools.partial(_conv_mid_kernel, k1b=k1b, n2=n2),
        grid=(n1 // k1b, 1),
        in_specs=head + [_resident((2 * n2, 2 * n2)), y_spec,
                         pl.BlockSpec((2, k1b, n2, c), lambda k, bi: (0, k, 0, 0))],
        out_specs=y_spec,
        out_shape=jax.ShapeDtypeStruct((1, 2, n1, n2, c), BF16),
        compiler_params=_params("parallel", "parallel"),
        name="conv_mid",
    )(_const(t["p"]), _const(t["q"]), _const(t["tw_k1"]), _const(t["inv2"] / n), y, kf)

    sb, cb = _ROW_GROUP, 512
    seq_spec = pl.BlockSpec((2, half, sb, cb), lambda g, ci: (0, 0, g, ci))
    out = pl.pallas_call(
        _conv_out_kernel,
        grid=(n2 // sb, c // cb),
        in_specs=[_resident((n1, 2 * n1)), _resident((n1, 2 * n1)),
                  pl.BlockSpec((sb, 2, 2 * n1), lambda g, ci: (g, 0, 0)),
                  pl.BlockSpec((1, 2, n1, sb, cb), lambda g, ci: (0, 0, 0, g, ci)),
                  seq_spec, seq_spec,
                  pl.BlockSpec((1, cb), lambda g, ci: (0, ci))],
        out_specs=seq_spec,
        out_shape=jax.ShapeDtypeStruct((2, half, n2, c), BF16),
        compiler_params=_params("parallel", "parallel"),
        name="conv_out",
    )(_const(t["pc"]), _const(t["qc"]), _const(t["tw_n2"]), z,
      vx.reshape(2, half, n2, c), x1.reshape(2, half, n2, c), skip)
    return out.reshape(b * l, c)


def _merge_kernel(mx_ref, yh_ref, ga_ref, gb_ref, x_ref, wf_ref, wh_ref, wo_ref, g2_ref,
                  x1_ref, h2_ref, m_ref, *, chunk):
    d = x_ref.shape[1]
    for lo in range(0, d, chunk):
        cols = slice(lo, lo + chunk)
        ya = _dot(mx_ref[...], wf_ref[:, cols])
        yb = _dot(yh_ref[...], wh_ref[:, cols])
        m_ref[:, cols] = (ga_ref[:, cols].astype(F32) * ya
                          + gb_ref[:, cols].astype(F32) * yb).astype(BF16)
    ss = jnp.zeros((x_ref.shape[0], 1), F32)
    for lo in range(0, d, chunk):
        cols = slice(lo, lo + chunk)
        x1 = x_ref[:, cols] + _dot(m_ref[...], wo_ref[:, cols])
        x1_ref[:, cols] = x1
        ss = ss + jnp.sum(x1 * x1, axis=-1, keepdims=True)
    scale = lax.rsqrt(ss * (1.0 / d) + EPS)
    for lo in range(0, d, chunk):
        cols = slice(lo, lo + chunk)
        h2_ref[:, cols] = (x1_ref[:, cols] * scale * g2_ref[:, cols]).astype(BF16)


def _merge(mixed, yh, proj, x2, wf, wh, wo, g2):
    rows = x2.shape[0]
    tm = 512
    d = D_MODEL
    gate_block = (proj.shape[1] - 2 * d) // d
    return pl.pallas_call(
        functools.partial(_merge_kernel, chunk=512),
        grid=(rows // tm,),
        in_specs=[
            pl.BlockSpec((tm, D_FNET), lambda i: (i, 0)),
            pl.BlockSpec((tm, D_HYENA), lambda i: (i, 0)),
            pl.BlockSpec((tm, d), lambda i: (i, gate_block)),
            pl.BlockSpec((tm, d), lambda i: (i, gate_block + 1)),
            pl.BlockSpec((tm, d), lambda i: (i, 0)),
            _resident((D_FNET, d)), _resident((D_HYENA, d)), _resident((d, d)),
            _resident((1, d)),
        ],
        out_specs=[pl.BlockSpec((tm, d), lambda i: (i, 0))] * 2,
        out_shape=[jax.ShapeDtypeStruct((rows, d), F32),
                   jax.ShapeDtypeStruct((rows, d), BF16)],
        scratch_shapes=[pltpu.VMEM((tm, d), BF16)],
        compiler_params=_params("parallel"),
        name="merge",
    )(mixed, yh, proj, proj, x2, wf, wh, wo, g2)


def _mlp_kernel(h2_ref, x1_ref, w1_ref, w2_ref, gf_ref, o_ref, *, nj):
    j = pl.program_id(1)
    slice_rows = x1_ref.shape[0]

    def step(first):
        t = _dot(h2_ref[...], w1_ref[...])
        t = jnp.square(jnp.maximum(t, 0.0)).astype(BF16)
        part = _dot(t, w2_ref[...])
        if first:
            o_ref[...] = part
        else:
            o_ref[...] += part
        rows = pl.ds(pl.multiple_of(j * slice_rows, slice_rows), slice_rows)
        o_ref[rows, :] += x1_ref[...]

    @pl.when(j == 0)
    def _():
        step(True)

    @pl.when(j > 0)
    def _():
        step(False)

    @pl.when(j == nj - 1)
    def _():
        x = o_ref[...]
        ms = jnp.mean(x * x, axis=-1, keepdims=True)
        o_ref[...] = x * lax.rsqrt(ms + EPS) * gf_ref[...]


def _mlp(h2, x1, w1, w2, gf):
    rows = x1.shape[0]
    tm, tf = 1024, 1024
    d = D_MODEL
    nj = D_FF // tf
    return pl.pallas_call(
        functools.partial(_mlp_kernel, nj=nj),
        grid=(rows // tm, nj),
        in_specs=[
            pl.BlockSpec((tm, d), lambda i, j: (i, 0)),
            pl.BlockSpec((tm // nj, d), lambda i, j: (i * nj + j, 0)),
            pl.BlockSpec((d, tf), lambda i, j: (0, j)),
            pl.BlockSpec((tf, d), lambda i, j: (j, 0)),
            pl.BlockSpec((1, d), lambda i, j: (0, 0)),
        ],
        out_specs=pl.BlockSpec((tm, d), lambda i, j: (i, 0)),
        out_shape=jax.ShapeDtypeStruct((rows, d), F32),
        compiler_params=_params("parallel", "arbitrary"),
        name="mlp",
    )(h2, x1, w1, w2, gf)


def _layer_and_final_norm(x, w, filt):
    b, l, d = x.shape
    x2 = x.reshape(b * l, d)
    proj = _project(x2, w["norm1_g"], w["w_cat"], w["b_gate"])
    mixed = _fnet(proj, b, l)
    vx, x1h = _shortconv(proj.reshape(b, l, proj.shape[-1]), w["conv_w"], w["conv_b"])
    kf = _filter_spectrum(l, *filt)
    yh = _longconv(vx, x1h, kf, w["skip"])
    x1, h2 = _merge(mixed, yh, proj, x2, w["w_fnet_map"], w["w_hyena_out"], w["w_out"],
                    w["norm2_g"])
    out = _mlp(h2, x1, w["w_mlp1"], w["w_mlp2"], w["norm_f_g"])
    return out.reshape(b, l, d)


def kernel(x_prompt, x_sample, norm1_g, w_in, conv_w, conv_b, filt_w1, filt_b1, filt_w2, filt_b2, filt_w3, filt_b3, filt_w4, filt_freq, hyena_skip, w_fnet_map, w_hyena_out, w_gate, b_gate, w_out, norm2_g, w_mlp1, w_mlp2, norm_f_g):
    assert norm1_g.shape[0] == 1, "one layer"
    w = {
        "norm1_g": norm1_g[0][None],
        "w_cat": jnp.concatenate([w_in[0].astype(BF16), w_gate[0].astype(BF16)], axis=1),
        "b_gate": b_gate[0][None],
        "conv_w": conv_w[0],
        "conv_b": conv_b[0][None],
        "skip": hyena_skip[0][None],
        "w_fnet_map": w_fnet_map[0].astype(BF16),
        "w_hyena_out": w_hyena_out[0].astype(BF16),
        "w_out": w_out[0].astype(BF16),
        "norm2_g": norm2_g[0][None],
        "w_mlp1": w_mlp1[0].astype(BF16),
        "w_mlp2": w_mlp2[0].astype(BF16),
        "norm_f_g": norm_f_g[None],
    }
    filt = (filt_w1[0], filt_b1[0], filt_w2[0], filt_b2[0], filt_w3[0], filt_b3[0],
            filt_w4[0], filt_freq[0])
    return (_layer_and_final_norm(x_prompt, w, filt),
            _layer_and_final_norm(x_sample, w, filt))
```

```python
import functools
import math

import jax
import jax.numpy as jnp
import numpy as np
from jax import lax
from jax.experimental import pallas as pl
from jax.experimental.pallas import tpu as pltpu

D_MODEL = 2048
N_FNET_GROUPS = 4
FNET_GROUP_DIM = 256
D_FNET = N_FNET_GROUPS * FNET_GROUP_DIM
D_HYENA = 1024
FILTER_BANDS = 16
FILTER_HIDDEN = 64
DECAY_TARGET = 1e-2
FAST_DECAY_PCT = 0.3
SLOW_DECAY_PCT = 1.5
D_FF = 4 * D_MODEL
EPS = 1e-6

F32 = jnp.float32
BF16 = jnp.bfloat16

_VMEM_LIMIT_BYTES = 56 * 1024 * 1024
_DFT_N2 = 128
_BF16_SUBLANES = 16
_ROW_GROUP = _BF16_SUBLANES

_dot = functools.partial(jnp.dot, preferred_element_type=F32)
_dot_hi = functools.partial(jnp.dot, preferred_element_type=F32,
                            precision=lax.Precision.HIGHEST)


def _dot_tn_bf16x3(a, b):
    dims = (((0,), (0,)), ((), ()))
    a_hi, b_hi = a.astype(BF16), b.astype(BF16)
    a_lo = (a - a_hi.astype(F32)).astype(BF16)
    b_lo = (b - b_hi.astype(F32)).astype(BF16)
    dg = functools.partial(lax.dot_general, dimension_numbers=dims, preferred_element_type=F32)
    return dg(a_hi, b_hi) + dg(a_lo, b_hi) + dg(a_hi, b_lo)


def _params(*sem):
    return pltpu.CompilerParams(dimension_semantics=sem,
                                vmem_limit_bytes=_VMEM_LIMIT_BYTES)


def _resident(shape):
    zeros = (0,) * len(shape)
    return pl.BlockSpec(shape, lambda *_: zeros, pipeline_mode=pl.Buffered(1))


def _cos_sin(n):
    jk = (np.arange(n)[:, None] * np.arange(n)[None, :]) % n
    ang = 2.0 * np.pi * jk / n
    return np.cos(ang), np.sin(ang)


@functools.lru_cache(maxsize=None)
def _dft_tables(n1, n2):
    n = n1 * n2
    c1, s1 = _cos_sin(n1)
    c2, s2 = _cos_sin(n2)
    fr1, fi1 = c1, -s1
    fr2, fi2 = c2, -s2
    ang = 2.0 * np.pi * ((np.arange(n1)[:, None] * np.arange(n2)[None, :]) % n) / n
    twr, twi = np.cos(ang), -np.sin(ang)
    t = {}
    h = n1 // 2
    t["rows_fwd"] = np.concatenate([fr1, fi1], axis=0)
    t["rows_fwd_c"] = np.block([[fr1[:, :h], -fi1[:, :h]], [fi1[:, :h], fr1[:, :h]]])
    t["p"] = np.block([[fr2, -fi2], [fi2, fr2]])
    t["q"] = np.block([[-fi2, -fr2], [fr2, -fi2]])
    t["tw_k1"] = np.stack([np.concatenate([twr, twr], axis=1),
                           np.concatenate([twi, twi], axis=1)], axis=1)
    t["inv2"] = np.block([[fr2, fi2], [-fi2, fr2]])
    t["pc"] = np.block([[fr1[:h], fi1[:h]], [-fi1[:h], fr1[:h]]])
    t["qc"] = np.block([[-fi1[:h], fr1[:h]], [-fr1[:h], -fi1[:h]]])
    t["tw_n2"] = np.stack([np.concatenate([twr.T, twr.T], axis=1),
                           np.concatenate([twi.T, twi.T], axis=1)], axis=1)
    return t


def _const(a, dtype=F32):
    return jnp.asarray(np.asarray(a, np.float32), dtype)


def _cast_concat_kernel(*refs):
    o_ref = refs[-1]
    lo = 0
    for w_ref in refs[:-1]:
        o_ref[:, lo:lo + w_ref.shape[1]] = w_ref[...].astype(BF16)
        lo += w_ref.shape[1]


def _cast_concat(ws):
    k = ws[0].shape[0]
    n = sum(w.shape[1] for w in ws)
    rb = 128
    return pl.pallas_call(
        _cast_concat_kernel,
        grid=(k // rb,),
        in_specs=[pl.BlockSpec((rb, w.shape[1]), lambda i: (i, 0)) for w in ws],
        out_specs=pl.BlockSpec((rb, n), lambda i: (i, 0)),
        out_shape=jax.ShapeDtypeStruct((k, n), BF16),
        compiler_params=_params("parallel"),
        name="cast_concat",
    )(*ws)


def _project_kernel(x_ref, g_ref, w_ref, b_ref, o_ref, h_ref, *, gate_tile, chunk):
    j = pl.program_id(1)
    tm = x_ref.shape[0]

    def emit(gated, rows):
        for lo in range(0, o_ref.shape[1], chunk):
            acc = _dot(h_ref[rows, :], w_ref[:, lo:lo + chunk])
            if gated:
                acc = jax.nn.sigmoid(acc + b_ref[:, lo:lo + chunk])
            o_ref[rows, lo:lo + chunk] = acc.astype(BF16)

    @pl.when(j == 0)
    def _():
        for r0 in range(0, tm, tm // 2):
            rows = slice(r0, r0 + tm // 2)
            x = x_ref[rows, :]
            ms = jnp.mean(x * x, axis=-1, keepdims=True)
            h_ref[rows, :] = (x * lax.rsqrt(ms + EPS) * g_ref[...]).astype(BF16)
            emit(False, rows)

    @pl.when(jnp.logical_and(j > 0, j < gate_tile))
    def _():
        emit(False, slice(None))

    @pl.when(j >= gate_tile)
    def _():
        emit(True, slice(None))


def _project(x2, g, w_cat, b_gate):
    rows = x2.shape[0]
    n_out = w_cat.shape[1]
    tm, tn = 1024, 2048
    gate_tile = (n_out - b_gate.shape[1]) // tn
    assert gate_tile >= 1, "the first column tile of a row tile is not gated"
    return pl.pallas_call(
        functools.partial(_project_kernel, gate_tile=gate_tile, chunk=512),
        grid=(rows // tm, n_out // tn),
        in_specs=[
            pl.BlockSpec((tm, D_MODEL), lambda i, j: (i, 0)),
            pl.BlockSpec((1, D_MODEL), lambda i, j: (0, 0)),
            pl.BlockSpec((D_MODEL, tn), lambda i, j: (0, j)),
            pl.BlockSpec((1, tn), lambda i, j: (0, jnp.maximum(j - gate_tile, 0))),
        ],
        out_specs=pl.BlockSpec((tm, tn), lambda i, j: (i, j)),
        out_shape=jax.ShapeDtypeStruct((rows, n_out), BF16),
        scratch_shapes=[pltpu.VMEM((tm, D_MODEL), BF16)],
        compiler_params=_params("parallel", "arbitrary"),
        name="project",
    )(x2, g, w_cat, b_gate)


def _swap_major(x):
    return jnp.transpose(x, (1, 0, 2))


def _rows_fwd_kernel(lhs_ref, x_ref, o_ref):
    _, parts, k, sb, cb = x_ref.shape
    x = x_ref[0].astype(BF16).reshape(parts * k, sb, cb)
    xt = _swap_major(x)
    lhs = lhs_ref[...].astype(BF16)
    y = jnp.stack([_dot(lhs, xt[s]).astype(BF16) for s in range(sb)], axis=0)
    o_ref[0] = _swap_major(y).reshape(o_ref.shape[1:])


def _rows_fwd(lhs, x5, name, c=None):
    b, parts, k, n2, _ = x5.shape
    c = c or x5.shape[-1]
    n1 = lhs.shape[0] // 2
    sb, cb = _ROW_GROUP, 512
    return pl.pallas_call(
        _rows_fwd_kernel,
        grid=(b, n2 // sb, c // cb),
        in_specs=[_resident(lhs.shape),
                  pl.BlockSpec((1, parts, k, sb, cb), lambda bi, g, ci: (bi, 0, 0, g, ci))],
        out_specs=pl.BlockSpec((1, 2, n1, sb, cb), lambda bi, g, ci: (bi, 0, 0, g, ci)),
        out_shape=jax.ShapeDtypeStruct((b, 2, n1, n2, c), BF16),
        compiler_params=_params("parallel", "parallel", "parallel"),
        name=name,
    )(lhs, x5)


def _stage2_fwd(p_ref, q_ref, tw_ref, y_ref, i):
    g = (p_ref[...] * tw_ref[i, 0:1, :] + q_ref[...] * tw_ref[i, 1:2, :]).astype(BF16)
    ys = jnp.concatenate([y_ref[0, 0, i], y_ref[0, 1, i]], axis=0)
    return _dot(g, ys)


def _conv_mid_kernel(p_ref, q_ref, tw_ref, inv_ref, y_ref, kf_ref, o_ref, *, k1b, n2):
    for i in range(k1b):
        s = _stage2_fwd(p_ref, q_ref, tw_ref, y_ref, i)
        xr, xi = s[:n2], s[n2:]
        kr, ki = kf_ref[0, i].astype(F32), kf_ref[1, i].astype(F32)
        prod = jnp.concatenate([xr * kr - xi * ki, xr * ki + xi * kr], axis=0)
        z = _dot(inv_ref[...].astype(BF16), prod.astype(BF16))
        o_ref[0, 0, i] = z[:n2].astype(BF16)
        o_ref[0, 1, i] = z[n2:].astype(BF16)


def _filter_mid_kernel(p_ref, q_ref, tw_ref, y_ref, ss_ref, o_ref, *, k1b, n2):
    scale = lax.rsqrt(ss_ref[...] + EPS)
    for i in range(k1b):
        s = _stage2_fwd(p_ref, q_ref, tw_ref, y_ref, i)
        o_ref[0, i] = (s[:n2] * scale).astype(o_ref.dtype)
        o_ref[1, i] = (s[n2:] * scale).astype(o_ref.dtype)


def _fnet_mid_kernel(p_ref, q_ref, tw_ref, y_ref, cc_ref, sc_ref, o_ref, *, k1b, n2):
    cb = o_ref.shape[-1]
    cc = cc_ref[...].astype(BF16)
    sc = sc_ref[...].astype(BF16)
    urs, uis = [], []
    for i in range(k1b):
        s = _stage2_fwd(p_ref, q_ref, tw_ref, y_ref, i)
        urs.append(s[:n2].astype(BF16))
        uis.append(s[n2:].astype(BF16))
    ur = jnp.concatenate(urs, axis=0)
    ui = jnp.concatenate(uis, axis=0)
    groups = []
    for lo in range(0, cb, FNET_GROUP_DIM):
        hi = lo + FNET_GROUP_DIM
        groups.append((_dot(ur[:, lo:hi], cc) + _dot(ui[:, lo:hi], sc)).astype(BF16))
    mixed = jnp.concatenate(groups, axis=1).reshape(k1b, n2, cb)
    o_ref[0] = _swap_major(mixed)


def _stage2_specs(n2, c, k1b):
    return [
        _resident((2 * n2, 2 * n2)),
        _resident((2 * n2, 2 * n2)),
        pl.BlockSpec((k1b, 2, 2 * n2), lambda k, b, *_: (k, 0, 0)),
    ], pl.BlockSpec((1, 2, k1b, n2, c), lambda k, b, *_: (b, 0, k, 0, 0))


def _fnet(proj, b, l):
    n2 = _DFT_N2
    n1 = l // n2
    c = D_FNET
    t = _dft_tables(n1, n2)
    lhs = _const(t["rows_fwd"] / math.sqrt(l))
    y = _rows_fwd(lhs, proj.reshape(b, 1, n1, n2, proj.shape[-1]), "fnet_rows", c)
    cc, sc = _cos_sin(FNET_GROUP_DIM)
    norm = 1.0 / math.sqrt(FNET_GROUP_DIM)
    k1b, cb = _ROW_GROUP, 2 * FNET_GROUP_DIM
    head, _ = _stage2_specs(n2, c, k1b)
    out = pl.pallas_call(
        functools.partial(_fnet_mid_kernel, k1b=k1b, n2=n2),
        grid=(n1 // k1b, b, c // cb),
        in_specs=head + [pl.BlockSpec((1, 2, k1b, n2, cb), lambda k, bi, ci: (bi, 0, k, 0, ci)),
                         _resident((FNET_GROUP_DIM, FNET_GROUP_DIM)),
                         _resident((FNET_GROUP_DIM, FNET_GROUP_DIM))],
        out_specs=pl.BlockSpec((1, n2, k1b, cb), lambda k, bi, ci: (bi, 0, k, ci)),
        out_shape=jax.ShapeDtypeStruct((b, n2, n1, c), BF16),
        compiler_params=_params("parallel", "parallel", "parallel"),
        name="fnet_mid",
    )(_const(t["p"]), _const(t["q"]), _const(t["tw_k1"]), y,
      _const(cc * norm), _const(sc * norm))
    return out.reshape(b * l, c)


def _shortconv_kernel(*refs, r, nt):
    u_refs, prev_refs, next_refs = refs[0:3], refs[3:6], refs[6:9]
    w_ref, b_ref, vx_ref, x1_ref = refs[9:]
    i = pl.program_id(1)
    row = lax.broadcasted_iota(jnp.int32, (r, 1), 0)
    outs = []
    for part in range(3):
        cols = slice(part * D_HYENA, (part + 1) * D_HYENA)
        u = u_refs[part][0].astype(F32)
        prev_row = prev_refs[part][0].astype(F32)[_BF16_SUBLANES - 1:_BF16_SUBLANES]
        next_row = next_refs[part][0].astype(F32)[0:1]
        prev_row = jnp.where(i == 0, 0.0, prev_row)
        next_row = jnp.where(i == nt - 1, 0.0, next_row)
        u_prev = jnp.where(row == 0, prev_row, pltpu.roll(u, 1, axis=0))
        u_next = jnp.where(row == r - 1, next_row, pltpu.roll(u, r - 1, axis=0))
        outs.append(u_prev * w_ref[0:1, cols] + u * w_ref[1:2, cols]
                    + u_next * w_ref[2:3, cols] + b_ref[:, cols])
    x1, x2, v = outs
    vx_ref[0] = (v * x2).astype(BF16)
    x1_ref[0] = x1.astype(BF16)


def _shortconv(proj3, conv_w, conv_b):
    b, l, _ = proj3.shape
    c = D_HYENA
    r = 512
    nt = l // r
    hb = _BF16_SUBLANES
    halo = r // hb
    first_block = D_FNET // c
    tiles = [pl.BlockSpec((1, r, c), lambda bi, i, p=p: (bi, i, first_block + p))
             for p in range(3)]
    prevs = [pl.BlockSpec((1, hb, c),
                          lambda bi, i, p=p: (bi, jnp.maximum(i * halo - 1, 0), first_block + p))
             for p in range(3)]
    nexts = [pl.BlockSpec((1, hb, c),
                          lambda bi, i, p=p: (bi, jnp.minimum((i + 1) * halo, l // hb - 1),
                                              first_block + p))
             for p in range(3)]
    return pl.pallas_call(
        functools.partial(_shortconv_kernel, r=r, nt=nt),
        grid=(b, nt),
        in_specs=tiles + prevs + nexts + [_resident((3, 3 * c)), _resident((1, 3 * c))],
        out_specs=[pl.BlockSpec((1, r, c), lambda bi, i: (bi, i, 0))] * 2,
        out_shape=[jax.ShapeDtypeStruct((b, l, c), BF16)] * 2,
        compiler_params=_params("parallel", "parallel"),
        name="shortconv",
    )(*([proj3] * 9), conv_w, conv_b)


def _filter_kernel(bands_ref, w1t_ref, w1c_ref, w1s_ref, b1_ref, w2_ref, b2_ref,
                   w3_ref, b3_ref, fr_ref, w4_ref, deltas_ref, k_ref, ss_ref, decay_ref,
                   *, r, l):
    i = pl.program_id(0)
    base = i * r
    lane_row = base + lax.broadcasted_iota(jnp.int32, (1, r), 1)
    lag = jnp.where(lane_row < l, lane_row, 2 * l - lane_row).astype(F32)
    ang = bands_ref[...] * ((2.0 * math.pi / l) * lag)
    fr = fr_ref[...]
    pre = (w1t_ref[...] * (lag * (1.0 / (l - 1))) + _dot_hi(w1c_ref[...], jnp.cos(ang))
           - _dot_hi(w1s_ref[...], jnp.sin(ang)) + b1_ref[...])
    h = jnp.sin(fr * pre)
    h = jnp.sin(fr * (_dot_hi(w2_ref[...], h) + b2_ref[...]))
    h = jnp.sin(fr * (_dot_hi(w3_ref[...], h) + b3_ref[...]))

    @pl.when(i == 0)
    def _():
        ss_ref[...] = jnp.zeros_like(ss_ref)
        step = lax.broadcasted_iota(jnp.int32, (r, 1), 0).astype(F32) * (1.0 / (l - 1))
        decay_ref[0] = jnp.exp(-step * deltas_ref[...])
        decay_ref[1] = jnp.exp(step * deltas_ref[...])

    backward = base >= l
    lag0 = jnp.where(backward, 2 * l - base, base).astype(F32)
    tile_decay = jnp.exp(-(lag0 * (1.0 / (l - 1))) * deltas_ref[...])
    decay = decay_ref[backward.astype(jnp.int32)] * tile_decay
    row = base + lax.broadcasted_iota(jnp.int32, (r, 1), 0)
    k = jnp.where(row == l, 0.0, _dot_tn_bf16x3(h, w4_ref[...]) * decay)
    k_ref[...] = k.astype(k_ref.dtype)
    ss_ref[...] += jnp.sum(k * k, axis=0, keepdims=True)


def _filter_spectrum(l, w1, b1, w2, b2, w3, b3, w4, freq):
    n = 2 * l
    n2 = _DFT_N2
    n1 = n // n2
    c = D_HYENA
    r = 1024
    tiles_per_half = l // r
    bands = np.linspace(1e-4, FILTER_BANDS - 1, FILTER_BANDS)[:, None]
    deltas = np.abs(np.linspace(math.log(DECAY_TARGET) / FAST_DECAY_PCT,
                                math.log(DECAY_TARGET) / SLOW_DECAY_PCT, D_HYENA))[None, :]
    small = [_const(bands), w1[0:1].T, w1[1:1 + FILTER_BANDS].T, w1[1 + FILTER_BANDS:].T,
             b1[:, None], w2.T, b2[:, None], w3.T, b3[:, None], freq[:, None]]
    k, ss = pl.pallas_call(
        functools.partial(_filter_kernel, r=r, l=l),
        grid=(n // r,),
        in_specs=[_resident(s.shape) for s in small] + [
            pl.BlockSpec((FILTER_HIDDEN, c), lambda i: (0, i // tiles_per_half)),
            _resident((1, c))],
        out_specs=[pl.BlockSpec((r, c), lambda i: (i, 0)),
                   pl.BlockSpec((1, c), lambda i: (0, 0))],
        out_shape=[jax.ShapeDtypeStruct((n, c), BF16),
                   jax.ShapeDtypeStruct((1, c), F32)],
        scratch_shapes=[pltpu.VMEM((2, r, c), F32)],
        compiler_params=_params("arbitrary"),
        name="hyena_filter",
    )(*small, w4, _const(deltas))

    t = _dft_tables(n1, n2)
    y = _rows_fwd(_const(t["rows_fwd"]), k.reshape(1, 1, n1, n2, c), "filter_rows")
    k1b = 8
    head, y_spec = _stage2_specs(n2, c, k1b)
    return pl.pallas_call(
        functools.partial(_filter_mid_kernel, k1b=k1b, n2=n2),
        grid=(n1 // k1b, 1),
        in_specs=head + [y_spec, _resident((1, c))],
        out_specs=pl.BlockSpec((2, k1b, n2, c), lambda k, bi: (0, k, 0, 0)),
        out_shape=jax.ShapeDtypeStruct((2, n1, n2, c), BF16),
        compiler_params=_params("parallel", "parallel"),
        name="filter_mid",
    )(_const(t["p"]), _const(t["q"]), _const(t["tw_k1"]), y, ss)


def _conv_out_kernel(pc_ref, qc_ref, tw_ref, z_ref, vx_ref, x1_ref, skip_ref, o_ref):
    _, _, n1, sb, cb = z_ref.shape
    z = _swap_major(z_ref[0].reshape(2 * n1, sb, cb))
    ys = []
    for s in range(sb):
        lhs = (pc_ref[...] * tw_ref[s, 0:1, :] + qc_ref[...] * tw_ref[s, 1:2, :]).astype(BF16)
        ys.append(_dot(lhs, z[s]))
    y = _swap_major(jnp.stack(ys, axis=0)).reshape(o_ref.shape)
    y = y + vx_ref[...].astype(F32) * skip_ref[...]
    o_ref[...] = (y * x1_ref[...].astype(F32)).astype(BF16)


def _longconv(vx, x1, kf, skip):
    b, l, c = vx.shape
    assert b == 2, "batch pair is packed as (re, im)"
    n = 2 * l
    n2 = _DFT_N2
    n1 = n // n2
    half = n1 // 2
    t = _dft_tables(n1, n2)
    y = _rows_fwd(_const(t["rows_fwd_c"]), vx.reshape(1, 2, half, n2, c), "conv_rows")
    k1b = 8
    head, y_spec = _stage2_specs(n2, c, k1b)
    z = pl.pallas_call(
        functools.partial(_conv_mid_kernel, k1b=k1b, n2=n2),
        grid=(n1 // k1b, 1),
        in_specs=head + [_resident((2 * n2, 2 * n2)), y_spec,
                         pl.BlockSpec((2, k1b, n2, c), lambda k, bi: (0, k, 0, 0))],
        out_specs=y_spec,
        out_shape=jax.ShapeDtypeStruct((1, 2, n1, n2, c), BF16),
        compiler_params=_params("parallel", "parallel"),
        name="conv_mid",
    )(_const(t["p"]), _const(t["q"]), _const(t["tw_k1"]), _const(t["inv2"] / n), y, kf)

    sb, cb = _ROW_GROUP, 512
    seq_spec = pl.BlockSpec((2, half, sb, cb), lambda g, ci: (0, 0, g, ci))
    out = pl.pallas_call(
        _conv_out_kernel,
        grid=(n2 // sb, c // cb),
        in_specs=[_resident((n1, 2 * n1)), _resident((n1, 2 * n1)),
                  pl.BlockSpec((sb, 2, 2 * n1), lambda g, ci: (g, 0, 0)),
                  pl.BlockSpec((1, 2, n1, sb, cb), lambda g, ci: (0, 0, 0, g, ci)),
                  seq_spec, seq_spec,
                  pl.BlockSpec((1, cb), lambda g, ci: (0, ci))],
        out_specs=seq_spec,
        out_shape=jax.ShapeDtypeStruct((2, half, n2, c), BF16),
        compiler_params=_params("parallel", "parallel"),
        name="conv_out",
    )(_const(t["pc"]), _const(t["qc"]), _const(t["tw_n2"]), z,
      vx.reshape(2, half, n2, c), x1.reshape(2, half, n2, c), skip)
    return out.reshape(b * l, c)


def _merge_kernel(mx_ref, yh_ref, ga_ref, gb_ref, x_ref, wf_ref, wh_ref, wo_ref, g2_ref,
                  x1_ref, h2_ref, m_ref, *, chunk):
    d = x_ref.shape[1]
    for lo in range(0, d, chunk):
        cols = slice(lo, lo + chunk)
        ya = _dot(mx_ref[...], wf_ref[:, cols])
        yb = _dot(yh_ref[...], wh_ref[:, cols])
        m_ref[:, cols] = (ga_ref[:, cols].astype(F32) * ya
                          + gb_ref[:, cols].astype(F32) * yb).astype(BF16)
    ss = jnp.zeros((x_ref.shape[0], 1), F32)
    for lo in range(0, d, chunk):
        cols = slice(lo, lo + chunk)
        x1 = x_ref[:, cols] + _dot(m_ref[...], wo_ref[:, cols])
        x1_ref[:, cols] = x1
        ss = ss + jnp.sum(x1 * x1, axis=-1, keepdims=True)
    scale = lax.rsqrt(ss * (1.0 / d) + EPS)
    for lo in range(0, d, chunk):
        cols = slice(lo, lo + chunk)
        h2_ref[:, cols] = (x1_ref[:, cols] * scale * g2_ref[:, cols]).astype(BF16)


def _merge(mixed, yh, proj, x2, wf, wh, wo, g2):
    rows = x2.shape[0]
    tm = 512
    d = D_MODEL
    gate_block = (proj.shape[1] - 2 * d) // d
    return pl.pallas_call(
        functools.partial(_merge_kernel, chunk=512),
        grid=(rows // tm,),
        in_specs=[
            pl.BlockSpec((tm, D_FNET), lambda i: (i, 0)),
            pl.BlockSpec((tm, D_HYENA), lambda i: (i, 0)),
            pl.BlockSpec((tm, d), lambda i: (i, gate_block)),
            pl.BlockSpec((tm, d), lambda i: (i, gate_block + 1)),
            pl.BlockSpec((tm, d), lambda i: (i, 0)),
            _resident((D_FNET, d)), _resident((D_HYENA, d)), _resident((d, d)),
            _resident((1, d)),
        ],
        out_specs=[pl.BlockSpec((tm, d), lambda i: (i, 0))] * 2,
        out_shape=[jax.ShapeDtypeStruct((rows, d), F32),
                   jax.ShapeDtypeStruct((rows, d), BF16)],
        scratch_shapes=[pltpu.VMEM((tm, d), BF16)],
        compiler_params=_params("parallel"),
        name="merge",
    )(mixed, yh, proj, proj, x2, wf, wh, wo, g2)


def _mlp_kernel(h2_ref, x1_ref, w1_ref, w2_ref, gf_ref, o_ref, *, nj):
    j = pl.program_id(1)
    slice_rows = x1_ref.shape[0]

    def step(first):
        t = _dot(h2_ref[...], w1_ref[...])
        t = jnp.square(jnp.maximum(t, 0.0)).astype(BF16)
        part = _dot(t, w2_ref[...])
        if first:
            o_ref[...] = part
        else:
            o_ref[...] += part
        rows = pl.ds(pl.multiple_of(j * slice_rows, slice_rows), slice_rows)
        o_ref[rows, :] += x1_ref[...]

    @pl.when(j == 0)
    def _():
        step(True)

    @pl.when(j > 0)
    def _():
        step(False)

    @pl.when(j == nj - 1)
    def _():
        x = o_ref[...]
        ms = jnp.mean(x * x, axis=-1, keepdims=True)
        o_ref[...] = x * lax.rsqrt(ms + EPS) * gf_ref[...]


def _mlp(h2, x1, w1, w2, gf):
    rows = x1.shape[0]
    tm, tf = 1024, 1024
    d = D_MODEL
    nj = D_FF // tf
    return pl.pallas_call(
        functools.partial(_mlp_kernel, nj=nj),
        grid=(rows // tm, nj),
        in_specs=[
            pl.BlockSpec((tm, d), lambda i, j: (i, 0)),
            pl.BlockSpec((tm // nj, d), lambda i, j: (i * nj + j, 0)),
            pl.BlockSpec((d, tf), lambda i, j: (0, j)),
            pl.BlockSpec((tf, d), lambda i, j: (j, 0)),
            pl.BlockSpec((1, d), lambda i, j: (0, 0)),
        ],
        out_specs=pl.BlockSpec((tm, d), lambda i, j: (i, 0)),
        out_shape=jax.ShapeDtypeStruct((rows, d), F32),
        compiler_params=_params("parallel", "arbitrary"),
        name="mlp",
    )(h2, x1, w1, w2, gf)


def _layer_and_final_norm(x, w, filt):
    b, l, d = x.shape
    x2 = x.reshape(b * l, d)
    proj = _project(x2, w["norm1_g"], w["w_cat"], w["b_gate"])
    mixed = _fnet(proj, b, l)
    vx, x1h = _shortconv(proj.reshape(b, l, proj.shape[-1]), w["conv_w"], w["conv_b"])
    kf = _filter_spectrum(l, *filt)
    yh = _longconv(vx, x1h, kf, w["skip"])
    x1, h2 = _merge(mixed, yh, proj, x2, w["w_fnet_map"], w["w_hyena_out"], w["w_out"],
                    w["norm2_g"])
    out = _mlp(h2, x1, w["w_mlp1"], w["w_mlp2"], w["norm_f_g"])
    return out.reshape(b, l, d)


def kernel(x_prompt, x_sample, norm1_g, w_in, conv_w, conv_b, filt_w1, filt_b1, filt_w2, filt_b2, filt_w3, filt_b3, filt_w4, filt_freq, hyena_skip, w_fnet_map, w_hyena_out, w_gate, b_gate, w_out, norm2_g, w_mlp1, w_mlp2, norm_f_g):
    assert norm1_g.shape[0] == 1, "one layer"
    w = {
        "norm1_g": norm1_g[0][None],
        "w_cat": _cast_concat([w_in[0], w_gate[0]]),
        "b_gate": b_gate[0][None],
        "conv_w": conv_w[0],
        "conv_b": conv_b[0][None],
        "skip": hyena_skip[0][None],
        "w_fnet_map": w_fnet_map[0].astype(BF16),
        "w_hyena_out": w_hyena_out[0].astype(BF16),
        "w_out": w_out[0].astype(BF16),
        "norm2_g": norm2_g[0][None],
        "w_mlp1": w_mlp1[0].astype(BF16),
        "w_mlp2": w_mlp2[0].astype(BF16),
        "norm_f_g": norm_f_g[None],
    }
    filt = (filt_w1[0], filt_b1[0], filt_w2[0], filt_b2[0], filt_w3[0], filt_b3[0],
            filt_w4[0], filt_freq[0])
    return (_layer_and_final_norm(x_prompt, w, filt),
            _layer_and_final_norm(x_sample, w, filt))
```

```python
import functools
import math

import jax
import jax.numpy as jnp
import numpy as np
from jax import lax
from jax.experimental import pallas as pl
from jax.experimental.pallas import tpu as pltpu

D_MODEL = 2048
N_FNET_GROUPS = 4
FNET_GROUP_DIM = 256
D_FNET = N_FNET_GROUPS * FNET_GROUP_DIM
D_HYENA = 1024
FILTER_BANDS = 16
FILTER_HIDDEN = 64
DECAY_TARGET = 1e-2
FAST_DECAY_PCT = 0.3
SLOW_DECAY_PCT = 1.5
D_FF = 4 * D_MODEL
EPS = 1e-6

F32 = jnp.float32
BF16 = jnp.bfloat16

_VMEM_LIMIT_BYTES = 56 * 1024 * 1024
_DFT_N2 = 128
_BF16_SUBLANES = 16
_ROW_GROUP = _BF16_SUBLANES

_dot = functools.partial(jnp.dot, preferred_element_type=F32)
_dot_hi = functools.partial(jnp.dot, preferred_element_type=F32,
                            precision=lax.Precision.HIGHEST)


def _dot_tn_bf16x3(a, b):
    dims = (((0,), (0,)), ((), ()))
    a_hi, b_hi = a.astype(BF16), b.astype(BF16)
    a_lo = (a - a_hi.astype(F32)).astype(BF16)
    b_lo = (b - b_hi.astype(F32)).astype(BF16)
    dg = functools.partial(lax.dot_general, dimension_numbers=dims, preferred_element_type=F32)
    return dg(a_hi, b_hi) + dg(a_lo, b_hi) + dg(a_hi, b_lo)


def _params(*sem):
    return pltpu.CompilerParams(dimension_semantics=sem,
                                vmem_limit_bytes=_VMEM_LIMIT_BYTES)


def _resident(shape):
    zeros = (0,) * len(shape)
    return pl.BlockSpec(shape, lambda *_: zeros, pipeline_mode=pl.Buffered(1))


def _cos_sin(n):
    jk = (np.arange(n)[:, None] * np.arange(n)[None, :]) % n
    ang = 2.0 * np.pi * jk / n
    return np.cos(ang), np.sin(ang)


@functools.lru_cache(maxsize=None)
def _dft_tables(n1, n2):
    n = n1 * n2
    c1, s1 = _cos_sin(n1)
    c2, s2 = _cos_sin(n2)
    fr1, fi1 = c1, -s1
    fr2, fi2 = c2, -s2
    ang = 2.0 * np.pi * ((np.arange(n1)[:, None] * np.arange(n2)[None, :]) % n) / n
    twr, twi = np.cos(ang), -np.sin(ang)
    t = {}
    h = n1 // 2
    t["rows_fwd"] = np.concatenate([fr1, fi1], axis=0)
    t["rows_fwd_c"] = np.block([[fr1[:, :h], -fi1[:, :h]], [fi1[:, :h], fr1[:, :h]]])
    t["p"] = np.block([[fr2, -fi2], [fi2, fr2]])
    t["q"] = np.block([[-fi2, -fr2], [fr2, -fi2]])
    t["tw_k1"] = np.stack([np.concatenate([twr, twr], axis=1),
                           np.concatenate([twi, twi], axis=1)], axis=1)
    t["inv2"] = np.block([[fr2, fi2], [-fi2, fr2]])
    t["pc"] = np.block([[fr1[:h], fi1[:h]], [-fi1[:h], fr1[:h]]])
    t["qc"] = np.block([[-fi1[:h], fr1[:h]], [-fr1[:h], -fi1[:h]]])
    t["tw_n2"] = np.stack([np.concatenate([twr.T, twr.T], axis=1),
                           np.concatenate([twi.T, twi.T], axis=1)], axis=1)
    return t


def _const(a, dtype=F32):
    return jnp.asarray(np.asarray(a, np.float32), dtype)


def _cast_concat_kernel(*refs):
    o_ref = refs[-1]
    lo = 0
    for w_ref in refs[:-1]:
        o_ref[:, lo:lo + w_ref.shape[1]] = w_ref[...].astype(BF16)
        lo += w_ref.shape[1]


def _cast_concat(ws):
    k = ws[0].shape[0]
    n = sum(w.shape[1] for w in ws)
    rb = 128
    return pl.pallas_call(
        _cast_concat_kernel,
        grid=(k // rb,),
        in_specs=[pl.BlockSpec((rb, w.shape[1]), lambda i: (i, 0)) for w in ws],
        out_specs=pl.BlockSpec((rb, n), lambda i: (i, 0)),
        out_shape=jax.ShapeDtypeStruct((k, n), BF16),
        compiler_params=_params("parallel"),
        name="cast_concat",
    )(*ws)


def _project_kernel(x_ref, g_ref, w_ref, b_ref, o_ref, h_ref, *, gate_tile, chunk):
    j = pl.program_id(1)
    tm = x_ref.shape[0]

    def emit(gated, rows):
        for lo in range(0, o_ref.shape[1], chunk):
            acc = _dot(h_ref[rows, :], w_ref[:, lo:lo + chunk])
            if gated:
                acc = jax.nn.sigmoid(acc + b_ref[:, lo:lo + chunk])
            o_ref[rows, lo:lo + chunk] = acc.astype(BF16)

    @pl.when(j == 0)
    def _():
        for r0 in range(0, tm, tm // 2):
            rows = slice(r0, r0 + tm // 2)
            x = x_ref[rows, :]
            ms = jnp.mean(x * x, axis=-1, keepdims=True)
            h_ref[rows, :] = (x * lax.rsqrt(ms + EPS) * g_ref[...]).astype(BF16)
            emit(False, rows)

    @pl.when(jnp.logical_and(j > 0, j < gate_tile))
    def _():
        emit(False, slice(None))

    @pl.when(j >= gate_tile)
    def _():
        emit(True, slice(None))


def _project(x2, g, w_cat, b_gate):
    rows = x2.shape[0]
    n_out = w_cat.shape[1]
    tm, tn = 1024, 2048
    gate_tile = (n_out - b_gate.shape[1]) // tn
    assert gate_tile >= 1, "the first column tile of a row tile is not gated"
    return pl.pallas_call(
        functools.partial(_project_kernel, gate_tile=gate_tile, chunk=512),
        grid=(rows // tm, n_out // tn),
        in_specs=[
            pl.BlockSpec((tm, D_MODEL), lambda i, j: (i, 0)),
            pl.BlockSpec((1, D_MODEL), lambda i, j: (0, 0)),
            pl.BlockSpec((D_MODEL, tn), lambda i, j: (0, j)),
            pl.BlockSpec((1, tn), lambda i, j: (0, jnp.maximum(j - gate_tile, 0))),
        ],
        out_specs=pl.BlockSpec((tm, tn), lambda i, j: (i, j)),
        out_shape=jax.ShapeDtypeStruct((rows, n_out), BF16),
        scratch_shapes=[pltpu.VMEM((tm, D_MODEL), BF16)],
        compiler_params=_params("parallel", "arbitrary"),
        name="project",
    )(x2, g, w_cat, b_gate)


def _swap_major(x):
    return jnp.transpose(x, (1, 0, 2))


def _rows_fwd_kernel(lhs_ref, x_ref, o_ref):
    _, parts, k, sb, cb = x_ref.shape
    x = x_ref[0].astype(BF16).reshape(parts * k, sb, cb)
    xt = _swap_major(x)
    lhs = lhs_ref[...].astype(BF16)
    y = jnp.stack([_dot(lhs, xt[s]).astype(BF16) for s in range(sb)], axis=0)
    o_ref[0] = _swap_major(y).reshape(o_ref.shape[1:])


def _rows_fwd(lhs, x5, name, c=None):
    b, parts, k, n2, _ = x5.shape
    c = c or x5.shape[-1]
    n1 = lhs.shape[0] // 2
    sb, cb = _ROW_GROUP, 512
    return pl.pallas_call(
        _rows_fwd_kernel,
        grid=(b, n2 // sb, c // cb),
        in_specs=[_resident(lhs.shape),
                  pl.BlockSpec((1, parts, k, sb, cb), lambda bi, g, ci: (bi, 0, 0, g, ci))],
        out_specs=pl.BlockSpec((1, 2, n1, sb, cb), lambda bi, g, ci: (bi, 0, 0, g, ci)),
        out_shape=jax.ShapeDtypeStruct((b, 2, n1, n2, c), BF16),
        compiler_params=_params("parallel", "parallel", "parallel"),
        name=name,
    )(lhs, x5)


def _stage2_fwd(p_ref, q_ref, tw_ref, y_ref, i):
    g = (p_ref[...] * tw_ref[i, 0:1, :] + q_ref[...] * tw_ref[i, 1:2, :]).astype(BF16)
    ys = jnp.concatenate([y_ref[0, 0, i], y_ref[0, 1, i]], axis=0)
    return _dot(g, ys)


def _conv_mid_kernel(p_ref, q_ref, tw_ref, inv_ref, y_ref, kf_ref, o_ref, *, k1b, n2):
    for i in range(k1b):
        s = _stage2_fwd(p_ref, q_ref, tw_ref, y_ref, i)
        xr, xi = s[:n2], s[n2:]
        kr, ki = kf_ref[0, i].astype(F32), kf_ref[1, i].astype(F32)
        prod = jnp.concatenate([xr * kr - xi * ki, xr * ki + xi * kr], axis=0)
        z = _dot(inv_ref[...].astype(BF16), prod.astype(BF16))
        o_ref[0, 0, i] = z[:n2].astype(BF16)
        o_ref[0, 1, i] = z[n2:].astype(BF16)


def _filter_mid_kernel(p_ref, q_ref, tw_ref, y_ref, ss_ref, o_ref, *, k1b, n2):
    scale = lax.rsqrt(ss_ref[...] + EPS)
    for i in range(k1b):
        s = _stage2_fwd(p_ref, q_ref, tw_ref, y_ref, i)
        o_ref[0, i] = (s[:n2] * scale).astype(o_ref.dtype)
        o_ref[1, i] = (s[n2:] * scale).astype(o_ref.dtype)


def _fnet_mid_kernel(p_ref, q_ref, tw_ref, y_ref, cc_ref, sc_ref, o_ref, *, k1b, n2):
    cb = o_ref.shape[-1]
    cc = cc_ref[...].astype(BF16)
    sc = sc_ref[...].astype(BF16)
    urs, uis = [], []
    for i in range(k1b):
        s = _stage2_fwd(p_ref, q_ref, tw_ref, y_ref, i)
        urs.append(s[:n2].astype(BF16))
        uis.append(s[n2:].astype(BF16))
    ur = jnp.concatenate(urs, axis=0)
    ui = jnp.concatenate(uis, axis=0)
    groups = []
    for lo in range(0, cb, FNET_GROUP_DIM):
        hi = lo + FNET_GROUP_DIM
        groups.append((_dot(ur[:, lo:hi], cc) + _dot(ui[:, lo:hi], sc)).astype(BF16))
    mixed = jnp.concatenate(groups, axis=1).reshape(k1b, n2, cb)
    o_ref[0] = _swap_major(mixed)


def _stage2_specs(n2, c, k1b):
    return [
        _resident((2 * n2, 2 * n2)),
        _resident((2 * n2, 2 * n2)),
        pl.BlockSpec((k1b, 2, 2 * n2), lambda k, b, *_: (k, 0, 0)),
    ], pl.BlockSpec((1, 2, k1b, n2, c), lambda k, b, *_: (b, 0, k, 0, 0))


def _fnet(proj, b, l):
    n2 = _DFT_N2
    n1 = l // n2
    c = D_FNET
    t = _dft_tables(n1, n2)
    lhs = _const(t["rows_fwd"] / math.sqrt(l))
    y = _rows_fwd(lhs, proj.reshape(b, 1, n1, n2, proj.shape[-1]), "fnet_rows", c)
    cc, sc = _cos_sin(FNET_GROUP_DIM)
    norm = 1.0 / math.sqrt(FNET_GROUP_DIM)
    k1b, cb = _ROW_GROUP, 2 * FNET_GROUP_DIM
    head, _ = _stage2_specs(n2, c, k1b)
    out = pl.pallas_call(
        functools.partial(_fnet_mid_kernel, k1b=k1b, n2=n2),
        grid=(n1 // k1b, b, c // cb),
        in_specs=head + [pl.BlockSpec((1, 2, k1b, n2, cb), lambda k, bi, ci: (bi, 0, k, 0, ci)),
                         _resident((FNET_GROUP_DIM, FNET_GROUP_DIM)),
                         _resident((FNET_GROUP_DIM, FNET_GROUP_DIM))],
        out_specs=pl.BlockSpec((1, n2, k1b, cb), lambda k, bi, ci: (bi, 0, k, ci)),
        out_shape=jax.ShapeDtypeStruct((b, n2, n1, c), BF16),
        compiler_params=_params("parallel", "parallel", "parallel"),
        name="fnet_mid",
    )(_const(t["p"]), _const(t["q"]), _const(t["tw_k1"]), y,
      _const(cc * norm), _const(sc * norm))
    return out.reshape(b * l, c)


def _shortconv_kernel(*refs, r, nt):
    u_refs, prev_refs, next_refs = refs[0:3], refs[3:6], refs[6:9]
    w_ref, b_ref, vx_ref, x1_ref = refs[9:]
    i = pl.program_id(1)
    row = lax.broadcasted_iota(jnp.int32, (r, 1), 0)
    outs = []
    for part in range(3):
        cols = slice(part * D_HYENA, (part + 1) * D_HYENA)
        u = u_refs[part][0].astype(F32)
        prev_row = prev_refs[part][0].astype(F32)[_BF16_SUBLANES - 1:_BF16_SUBLANES]
        next_row = next_refs[part][0].astype(F32)[0:1]
        prev_row = jnp.where(i == 0, 0.0, prev_row)
        next_row = jnp.where(i == nt - 1, 0.0, next_row)
        u_prev = jnp.where(row == 0, prev_row, pltpu.roll(u, 1, axis=0))
        u_next = jnp.where(row == r - 1, next_row, pltpu.roll(u, r - 1, axis=0))
        outs.append(u_prev * w_ref[0:1, cols] + u * w_ref[1:2, cols]
                    + u_next * w_ref[2:3, cols] + b_ref[:, cols])
    x1, x2, v = outs
    vx_ref[0] = (v * x2).astype(BF16)
    x1_ref[0] = x1.astype(BF16)


def _shortconv(proj3, conv_w, conv_b):
    b, l, _ = proj3.shape
    c = D_HYENA
    r = 512
    nt = l // r
    hb = _BF16_SUBLANES
    halo = r // hb
    first_block = D_FNET // c
    tiles = [pl.BlockSpec((1, r, c), lambda bi, i, p=p: (bi, i, first_block + p))
             for p in range(3)]
    prevs = [pl.BlockSpec((1, hb, c),
                          lambda bi, i, p=p: (bi, jnp.maximum(i * halo - 1, 0), first_block + p))
             for p in range(3)]
    nexts = [pl.BlockSpec((1, hb, c),
                          lambda bi, i, p=p: (bi, jnp.minimum((i + 1) * halo, l // hb - 1),
                                              first_block + p))
             for p in range(3)]
    return pl.pallas_call(
        functools.partial(_shortconv_kernel, r=r, nt=nt),
        grid=(b, nt),
        in_specs=tiles + prevs + nexts + [_resident((3, 3 * c)), _resident((1, 3 * c))],
        out_specs=[pl.BlockSpec((1, r, c), lambda bi, i: (bi, i, 0))] * 2,
        out_shape=[jax.ShapeDtypeStruct((b, l, c), BF16)] * 2,
        compiler_params=_params("parallel", "parallel"),
        name="shortconv",
    )(*([proj3] * 9), conv_w, conv_b)


def _filter_rows_kernel(bands_ref, w1t_ref, w1c_ref, w1s_ref, b1_ref, w2_ref, b2_ref,
                        w3_ref, b3_ref, fr_ref, w4_ref, deltas_ref, lhs_ref, y_ref, ss_ref,
                        decay_ref, *, l, n2, sb, chunk):
    g = pl.program_id(0)
    n1 = lhs_ref.shape[1]
    half = n1 // 2
    lanes = sb * half
    c = deltas_ref.shape[1]

    def offsets(shape, axis):
        idx = lax.broadcasted_iota(jnp.int32, shape, axis)
        back = idx >= lanes
        idx = jnp.where(back, idx - lanes, idx)
        s = lax.shift_right_logical(idx, half.bit_length() - 1)
        return back, n2 * (idx & (half - 1)) + s

    back, off = offsets((1, 2 * lanes), 1)
    lag = jnp.where(back, l - off - sb * g, off + sb * g).astype(F32)
    ang = bands_ref[...] * ((2.0 * math.pi / l) * lag)
    fr = fr_ref[...]
    pre = (w1t_ref[...] * (lag * (1.0 / (l - 1))) + _dot_hi(w1c_ref[...], jnp.cos(ang))
           - _dot_hi(w1s_ref[...], jnp.sin(ang)) + b1_ref[...])
    h = jnp.sin(fr * pre)
    h = jnp.sin(fr * (_dot_hi(w2_ref[...], h) + b2_ref[...]))
    h = jnp.sin(fr * (_dot_hi(w3_ref[...], h) + b3_ref[...]))

    @pl.when(g == 0)
    def _():
        ss_ref[...] = jnp.zeros_like(ss_ref)
        _, off_col = offsets((lanes, 1), 0)
        off_col = off_col.astype(F32)
        decay_ref[0] = jnp.exp(-(off_col * (1.0 / (l - 1))) * deltas_ref[...])
        decay_ref[1] = jnp.exp(-((l - off_col) * (1.0 / (l - 1))) * deltas_ref[...])

    shift = (sb * g).astype(F32) * (1.0 / (l - 1))
    row0 = lax.broadcasted_iota(jnp.int32, (lanes, 1), 0) == 0
    lhs = lhs_ref[...].astype(BF16)
    for lo in range(0, c, chunk):
        cols = slice(lo, lo + chunk)
        dl = deltas_ref[:, cols]
        k_fwd = (_dot_tn_bf16x3(h[:, :lanes], w4_ref[:, lo:lo + chunk])
                 * (decay_ref[0, :, cols] * jnp.exp(-shift * dl)))
        k_bwd = (_dot_tn_bf16x3(h[:, lanes:], w4_ref[:, c + lo:c + lo + chunk])
                 * (decay_ref[1, :, cols] * jnp.exp(shift * dl)))
        k_bwd = jnp.where(jnp.logical_and(row0, g == 0), 0.0, k_bwd)
        ss_ref[:, cols] += jnp.sum(k_fwd * k_fwd + k_bwd * k_bwd, axis=0, keepdims=True)
        ys = []
        for s in range(sb):
            rows = slice(s * half, (s + 1) * half)
            xs = jnp.concatenate([k_fwd[rows], k_bwd[rows]], axis=0).astype(BF16)
            ys.append(_dot(lhs, xs).astype(BF16))
        y = _swap_major(jnp.stack(ys, axis=0))
        y_ref[0, :, :, :, cols] = y.reshape(2, n1, sb, chunk)


def _filter_spectrum(l, w1, b1, w2, b2, w3, b3, w4, freq):
    n = 2 * l
    n2 = _DFT_N2
    n1 = n // n2
    c = D_HYENA
    sb = _ROW_GROUP
    t = _dft_tables(n1, n2)
    bands = np.linspace(1e-4, FILTER_BANDS - 1, FILTER_BANDS)[:, None]
    deltas = np.abs(np.linspace(math.log(DECAY_TARGET) / FAST_DECAY_PCT,
                                math.log(DECAY_TARGET) / SLOW_DECAY_PCT, D_HYENA))[None, :]
    ins = [_const(bands), w1[0:1].T, w1[1:1 + FILTER_BANDS].T, w1[1 + FILTER_BANDS:].T,
           b1[:, None], w2.T, b2[:, None], w3.T, b3[:, None], freq[:, None], w4,
           _const(deltas), _const(t["rows_fwd"])]
    y, ss = pl.pallas_call(
        functools.partial(_filter_rows_kernel, l=l, n2=n2, sb=sb, chunk=512),
        grid=(n2 // sb,),
        in_specs=[_resident(a.shape) for a in ins],
        out_specs=[pl.BlockSpec((1, 2, n1, sb, c), lambda g: (0, 0, 0, g, 0)),
                   pl.BlockSpec((1, c), lambda g: (0, 0))],
        out_shape=[jax.ShapeDtypeStruct((1, 2, n1, n2, c), BF16),
                   jax.ShapeDtypeStruct((1, c), F32)],
        scratch_shapes=[pltpu.VMEM((2, sb * (n1 // 2), c), F32)],
        compiler_params=_params("arbitrary"),
        name="filter_rows",
    )(*ins)

    k1b = 8
    head, y_spec = _stage2_specs(n2, c, k1b)
    return pl.pallas_call(
        functools.partial(_filter_mid_kernel, k1b=k1b, n2=n2),
        grid=(n1 // k1b, 1),
        in_specs=head + [y_spec, _resident((1, c))],
        out_specs=pl.BlockSpec((2, k1b, n2, c), lambda k, bi: (0, k, 0, 0)),
        out_shape=jax.ShapeDtypeStruct((2, n1, n2, c), BF16),
        compiler_params=_params("parallel", "parallel"),
        name="filter_mid",
    )(_const(t["p"]), _const(t["q"]), _const(t["tw_k1"]), y, ss)


def _conv_out_kernel(pc_ref, qc_ref, tw_ref, z_ref, vx_ref, x1_ref, skip_ref, o_ref):
    _, _, n1, sb, cb = z_ref.shape
    z = _swap_major(z_ref[0].reshape(2 * n1, sb, cb))
    ys = []
    for s in range(sb):
        lhs = (pc_ref[...] * tw_ref[s, 0:1, :] + qc_ref[...] * tw_ref[s, 1:2, :]).astype(BF16)
        ys.append(_dot(lhs, z[s]))
    y = _swap_major(jnp.stack(ys, axis=0)).reshape(o_ref.shape)
    y = y + vx_ref[...].astype(F32) * skip_ref[...]
    o_ref[...] = (y * x1_ref[...].astype(F32)).astype(BF16)


def _longconv(vx, x1, kf, skip):
    b, l, c = vx.shape
    assert b == 2, "batch pair is packed as (re, im)"
    n = 2 * l
    n2 = _DFT_N2
    n1 = n // n2
    half = n1 // 2
    t = _dft_tables(n1, n2)
    y = _rows_fwd(_const(t["rows_fwd_c"]), vx.reshape(1, 2, half, n2, c), "conv_rows")
    k1b = 8
    head, y_spec = _stage2_specs(n2, c, k1b)
    z = pl.pallas_call(
        functools.partial(_conv_mid_kernel, k1b=k1b, n2=n2),
        grid=(n1 // k1b, 1),
        in_specs=head + [_resident((2 * n2, 2 * n2)), y_spec,
                         pl.BlockSpec((2, k1b, n2, c), lambda k, bi: (0, k, 0, 0))],
        out_specs=y_spec,
        out_shape=jax.ShapeDtypeStruct((1, 2, n1, n2, c), BF16),
        compiler_params=_params("parallel", "parallel"),
        name="conv_mid",
    )(_const(t["p"]), _const(t["q"]), _const(t["tw_k1"]), _const(t["inv2"] / n), y, kf)

    sb, cb = _ROW_GROUP, 512
    seq_spec = pl.BlockSpec((2, half, sb, cb), lambda g, ci: (0, 0, g, ci))
    out = pl.pallas_call(
        _conv_out_kernel,
        grid=(n2 // sb, c // cb),
        in_specs=[_resident((n1, 2 * n1)), _resident((n1, 2 * n1)),
                  pl.BlockSpec((sb, 2, 2 * n1), lambda g, ci: (g, 0, 0)),
                  pl.BlockSpec((1, 2, n1, sb, cb), lambda g, ci: (0, 0, 0, g, ci)),
                  seq_spec, seq_spec,
                  pl.BlockSpec((1, cb), lambda g, ci: (0, ci))],
        out_specs=seq_spec,
        out_shape=jax.ShapeDtypeStruct((2, half, n2, c), BF16),
        compiler_params=_params("parallel", "parallel"),
        name="conv_out",
    )(_const(t["pc"]), _const(t["qc"]), _const(t["tw_n2"]), z,
      vx.reshape(2, half, n2, c), x1.reshape(2, half, n2, c), skip)
    return out.reshape(b * l, c)


def _merge_kernel(mx_ref, yh_ref, ga_ref, gb_ref, x_ref, wf_ref, wh_ref, wo_ref, g2_ref,
                  x1_ref, h2_ref, m_ref, *, chunk):
    d = x_ref.shape[1]
    for lo in range(0, d, chunk):
        cols = slice(lo, lo + chunk)
        ya = _dot(mx_ref[...], wf_ref[:, cols])
        yb = _dot(yh_ref[...], wh_ref[:, cols])
        m_ref[:, cols] = (ga_ref[:, cols].astype(F32) * ya
                          + gb_ref[:, cols].astype(F32) * yb).astype(BF16)
    ss = jnp.zeros((x_ref.shape[0], 1), F32)
    for lo in range(0, d, chunk):
        cols = slice(lo, lo + chunk)
        x1 = x_ref[:, cols] + _dot(m_ref[...], wo_ref[:, cols])
        x1_ref[:, cols] = x1
        ss = ss + jnp.sum(x1 * x1, axis=-1, keepdims=True)
    scale = lax.rsqrt(ss * (1.0 / d) + EPS)
    for lo in range(0, d, chunk):
        cols = slice(lo, lo + chunk)
        h2_ref[:, cols] = (x1_ref[:, cols] * scale * g2_ref[:, cols]).astype(BF16)


def _merge(mixed, yh, proj, x2, wf, wh, wo, g2):
    rows = x2.shape[0]
    tm = 512
    d = D_MODEL
    gate_block = (proj.shape[1] - 2 * d) // d
    return pl.pallas_call(
        functools.partial(_merge_kernel, chunk=512),
        grid=(rows // tm,),
        in_specs=[
            pl.BlockSpec((tm, D_FNET), lambda i: (i, 0)),
            pl.BlockSpec((tm, D_HYENA), lambda i: (i, 0)),
            pl.BlockSpec((tm, d), lambda i: (i, gate_block)),
            pl.BlockSpec((tm, d), lambda i: (i, gate_block + 1)),
            pl.BlockSpec((tm, d), lambda i: (i, 0)),
            _resident((D_FNET, d)), _resident((D_HYENA, d)), _resident((d, d)),
            _resident((1, d)),
        ],
        out_specs=[pl.BlockSpec((tm, d), lambda i: (i, 0))] * 2,
        out_shape=[jax.ShapeDtypeStruct((rows, d), F32),
                   jax.ShapeDtypeStruct((rows, d), BF16)],
        scratch_shapes=[pltpu.VMEM((tm, d), BF16)],
        compiler_params=_params("parallel"),
        name="merge",
    )(mixed, yh, proj, proj, x2, wf, wh, wo, g2)


def _mlp_kernel(h2_ref, x1_ref, w1_ref, w2_ref, gf_ref, o_ref, *, nj):
    j = pl.program_id(1)
    slice_rows = x1_ref.shape[0]

    def step(first):
        t = _dot(h2_ref[...], w1_ref[...])
        t = jnp.square(jnp.maximum(t, 0.0)).astype(BF16)
        part = _dot(t, w2_ref[...])
        if first:
            o_ref[...] = part
        else:
            o_ref[...] += part
        rows = pl.ds(pl.multiple_of(j * slice_rows, slice_rows), slice_rows)
        o_ref[rows, :] += x1_ref[...]

    @pl.when(j == 0)
    def _():
        step(True)

    @pl.when(j > 0)
    def _():
        step(False)

    @pl.when(j == nj - 1)
    def _():
        x = o_ref[...]
        ms = jnp.mean(x * x, axis=-1, keepdims=True)
        o_ref[...] = x * lax.rsqrt(ms + EPS) * gf_ref[...]


def _mlp(h2, x1, w1, w2, gf):
    rows = x1.shape[0]
    tm, tf = 1024, 1024
    d = D_MODEL
    nj = D_FF // tf
    return pl.pallas_call(
        functools.partial(_mlp_kernel, nj=nj),
        grid=(rows // tm, nj),
        in_specs=[
            pl.BlockSpec((tm, d), lambda i, j: (i, 0)),
            pl.BlockSpec((tm // nj, d), lambda i, j: (i * nj + j, 0)),
            pl.BlockSpec((d, tf), lambda i, j: (0, j)),
            pl.BlockSpec((tf, d), lambda i, j: (j, 0)),
            pl.BlockSpec((1, d), lambda i, j: (0, 0)),
        ],
        out_specs=pl.BlockSpec((tm, d), lambda i, j: (i, 0)),
        out_shape=jax.ShapeDtypeStruct((rows, d), F32),
        compiler_params=_params("parallel", "arbitrary"),
        name="mlp",
    )(h2, x1, w1, w2, gf)


def _layer_and_final_norm(x, w, filt):
    b, l, d = x.shape
    x2 = x.reshape(b * l, d)
    proj = _project(x2, w["norm1_g"], w["w_cat"], w["b_gate"])
    mixed = _fnet(proj, b, l)
    vx, x1h = _shortconv(proj.reshape(b, l, proj.shape[-1]), w["conv_w"], w["conv_b"])
    kf = _filter_spectrum(l, *filt)
    yh = _longconv(vx, x1h, kf, w["skip"])
    x1, h2 = _merge(mixed, yh, proj, x2, w["w_fnet_map"], w["w_hyena_out"], w["w_out"],
                    w["norm2_g"])
    out = _mlp(h2, x1, w["w_mlp1"], w["w_mlp2"], w["norm_f_g"])
    return out.reshape(b, l, d)


def kernel(x_prompt, x_sample, norm1_g, w_in, conv_w, conv_b, filt_w1, filt_b1, filt_w2, filt_b2, filt_w3, filt_b3, filt_w4, filt_freq, hyena_skip, w_fnet_map, w_hyena_out, w_gate, b_gate, w_out, norm2_g, w_mlp1, w_mlp2, norm_f_g):
    assert norm1_g.shape[0] == 1, "one layer"
    w = {
        "norm1_g": norm1_g[0][None],
        "w_cat": _cast_concat([w_in[0], w_gate[0]]),
        "b_gate": b_gate[0][None],
        "conv_w": conv_w[0],
        "conv_b": conv_b[0][None],
        "skip": hyena_skip[0][None],
        "w_fnet_map": w_fnet_map[0].astype(BF16),
        "w_hyena_out": w_hyena_out[0].astype(BF16),
        "w_out": w_out[0].astype(BF16),
        "norm2_g": norm2_g[0][None],
        "w_mlp1": w_mlp1[0].astype(BF16),
        "w_mlp2": w_mlp2[0].astype(BF16),
        "norm_f_g": norm_f_g[None],
    }
    filt = (filt_w1[0], filt_b1[0], filt_w2[0], filt_b2[0], filt_w3[0], filt_b3[0],
            filt_w4[0], filt_freq[0])
    return (_layer_and_final_norm(x_prompt, w, filt),
            _layer_and_final_norm(x_sample, w, filt))
```

```python
import functools
import math

import jax
import jax.numpy as jnp
import numpy as np
from jax import lax
from jax.experimental import pallas as pl
from jax.experimental.pallas import tpu as pltpu

D_MODEL = 2048
N_FNET_GROUPS = 4
FNET_GROUP_DIM = 256
D_FNET = N_FNET_GROUPS * FNET_GROUP_DIM
D_HYENA = 1024
FILTER_BANDS = 16
FILTER_HIDDEN = 64
DECAY_TARGET = 1e-2
FAST_DECAY_PCT = 0.3
SLOW_DECAY_PCT = 1.5
D_FF = 4 * D_MODEL
EPS = 1e-6

F32 = jnp.float32
BF16 = jnp.bfloat16

_VMEM_LIMIT_BYTES = 56 * 1024 * 1024
_DFT_N2 = 128
_BF16_SUBLANES = 16
_ROW_GROUP = _BF16_SUBLANES

_dot = functools.partial(jnp.dot, preferred_element_type=F32)
_dot_hi = functools.partial(jnp.dot, preferred_element_type=F32,
                            precision=lax.Precision.HIGHEST)


def _dot_tn_bf16x3(a, b):
    dims = (((0,), (0,)), ((), ()))
    a_hi, b_hi = a.astype(BF16), b.astype(BF16)
    a_lo = (a - a_hi.astype(F32)).astype(BF16)
    b_lo = (b - b_hi.astype(F32)).astype(BF16)
    dg = functools.partial(lax.dot_general, dimension_numbers=dims, preferred_element_type=F32)
    return dg(a_hi, b_hi) + dg(a_lo, b_hi) + dg(a_hi, b_lo)


def _params(*sem):
    return pltpu.CompilerParams(dimension_semantics=sem,
                                vmem_limit_bytes=_VMEM_LIMIT_BYTES)


def _resident(shape):
    zeros = (0,) * len(shape)
    return pl.BlockSpec(shape, lambda *_: zeros, pipeline_mode=pl.Buffered(1))


def _cos_sin(n):
    jk = (np.arange(n)[:, None] * np.arange(n)[None, :]) % n
    ang = 2.0 * np.pi * jk / n
    return np.cos(ang), np.sin(ang)


@functools.lru_cache(maxsize=None)
def _dft_tables(n1, n2):
    n = n1 * n2
    c1, s1 = _cos_sin(n1)
    c2, s2 = _cos_sin(n2)
    fr1, fi1 = c1, -s1
    fr2, fi2 = c2, -s2
    ang = 2.0 * np.pi * ((np.arange(n1)[:, None] * np.arange(n2)[None, :]) % n) / n
    twr, twi = np.cos(ang), -np.sin(ang)
    t = {}
    h = n1 // 2
    t["rows_fwd"] = np.concatenate([fr1, fi1], axis=0)
    t["rows_fwd_c"] = np.block([[fr1[:, :h], -fi1[:, :h]], [fi1[:, :h], fr1[:, :h]]])
    t["p"] = np.block([[fr2, -fi2], [fi2, fr2]])
    t["q"] = np.block([[-fi2, -fr2], [fr2, -fi2]])
    t["tw_k1"] = np.stack([np.concatenate([twr, twr], axis=1),
                           np.concatenate([twi, twi], axis=1)], axis=1)
    t["inv2"] = np.block([[fr2, fi2], [-fi2, fr2]])
    t["pc"] = np.block([[fr1[:h], fi1[:h]], [-fi1[:h], fr1[:h]]])
    t["qc"] = np.block([[-fi1[:h], fr1[:h]], [-fr1[:h], -fi1[:h]]])
    t["tw_n2"] = np.stack([np.concatenate([twr.T, twr.T], axis=1),
                           np.concatenate([twi.T, twi.T], axis=1)], axis=1)
    return t


def _const(a, dtype=F32):
    return jnp.asarray(np.asarray(a, np.float32), dtype)


def _cast_concat_kernel(*refs):
    o_ref = refs[-1]
    lo = 0
    for w_ref in refs[:-1]:
        o_ref[:, lo:lo + w_ref.shape[1]] = w_ref[...].astype(BF16)
        lo += w_ref.shape[1]


def _cast_concat(ws):
    k = ws[0].shape[0]
    n = sum(w.shape[1] for w in ws)
    rb = 128
    return pl.pallas_call(
        _cast_concat_kernel,
        grid=(k // rb,),
        in_specs=[pl.BlockSpec((rb, w.shape[1]), lambda i: (i, 0)) for w in ws],
        out_specs=pl.BlockSpec((rb, n), lambda i: (i, 0)),
        out_shape=jax.ShapeDtypeStruct((k, n), BF16),
        compiler_params=_params("parallel"),
        name="cast_concat",
    )(*ws)


def _project_kernel(x_ref, g_ref, w_ref, b_ref, o_ref, h_ref, *, gate_tile, chunk):
    j = pl.program_id(1)
    tm = x_ref.shape[0]

    def emit(gated, rows):
        for lo in range(0, o_ref.shape[1], chunk):
            acc = _dot(h_ref[rows, :], w_ref[:, lo:lo + chunk])
            if gated:
                acc = jax.nn.sigmoid(acc + b_ref[:, lo:lo + chunk])
            o_ref[rows, lo:lo + chunk] = acc.astype(BF16)

    @pl.when(j == 0)
    def _():
        for r0 in range(0, tm, tm // 2):
            rows = slice(r0, r0 + tm // 2)
            x = x_ref[rows, :]
            ms = jnp.mean(x * x, axis=-1, keepdims=True)
            h_ref[rows, :] = (x * lax.rsqrt(ms + EPS) * g_ref[...]).astype(BF16)
            emit(False, rows)

    @pl.when(jnp.logical_and(j > 0, j < gate_tile))
    def _():
        emit(False, slice(None))

    @pl.when(j >= gate_tile)
    def _():
        emit(True, slice(None))


def _project(x2, g, w_cat, b_gate):
    rows = x2.shape[0]
    n_out = w_cat.shape[1]
    tm, tn = 1024, 2048
    gate_tile = (n_out - b_gate.shape[1]) // tn
    assert gate_tile >= 1, "the first column tile of a row tile is not gated"
    return pl.pallas_call(
        functools.partial(_project_kernel, gate_tile=gate_tile, chunk=512),
        grid=(rows // tm, n_out // tn),
        in_specs=[
            pl.BlockSpec((tm, D_MODEL), lambda i, j: (i, 0)),
            pl.BlockSpec((1, D_MODEL), lambda i, j: (0, 0)),
            pl.BlockSpec((D_MODEL, tn), lambda i, j: (0, j)),
            pl.BlockSpec((1, tn), lambda i, j: (0, jnp.maximum(j - gate_tile, 0))),
        ],
        out_specs=pl.BlockSpec((tm, tn), lambda i, j: (i, j)),
        out_shape=jax.ShapeDtypeStruct((rows, n_out), BF16),
        scratch_shapes=[pltpu.VMEM((tm, D_MODEL), BF16)],
        compiler_params=_params("parallel", "arbitrary"),
        name="project",
    )(x2, g, w_cat, b_gate)


def _swap_major(x):
    return jnp.transpose(x, (1, 0, 2))


def _rows_fwd_kernel(lhs_ref, x_ref, o_ref):
    _, parts, k, sb, cb = x_ref.shape
    x = x_ref[0].astype(BF16).reshape(parts * k, sb, cb)
    xt = _swap_major(x)
    lhs = lhs_ref[...].astype(BF16)
    y = jnp.stack([_dot(lhs, xt[s]).astype(BF16) for s in range(sb)], axis=0)
    o_ref[0] = _swap_major(y).reshape(o_ref.shape[1:])


def _rows_fwd(lhs, x5, name, c=None):
    b, parts, k, n2, _ = x5.shape
    c = c or x5.shape[-1]
    n1 = lhs.shape[0] // 2
    sb, cb = _ROW_GROUP, 1024
    return pl.pallas_call(
        _rows_fwd_kernel,
        grid=(b, n2 // sb, c // cb),
        in_specs=[_resident(lhs.shape),
                  pl.BlockSpec((1, parts, k, sb, cb), lambda bi, g, ci: (bi, 0, 0, g, ci))],
        out_specs=pl.BlockSpec((1, 2, n1, sb, cb), lambda bi, g, ci: (bi, 0, 0, g, ci)),
        out_shape=jax.ShapeDtypeStruct((b, 2, n1, n2, c), BF16),
        compiler_params=_params("parallel", "parallel", "parallel"),
        name=name,
    )(lhs, x5)


def _stage2_fwd(p_ref, q_ref, tw_ref, y_ref, i):
    g = (p_ref[...] * tw_ref[i, 0:1, :] + q_ref[...] * tw_ref[i, 1:2, :]).astype(BF16)
    ys = jnp.concatenate([y_ref[0, 0, i], y_ref[0, 1, i]], axis=0)
    return _dot(g, ys)


def _conv_mid_kernel(p_ref, q_ref, tw_ref, inv_ref, y_ref, kf_ref, o_ref, *, k1b, n2):
    for i in range(k1b):
        s = _stage2_fwd(p_ref, q_ref, tw_ref, y_ref, i)
        xr, xi = s[:n2], s[n2:]
        kr, ki = kf_ref[0, i].astype(F32), kf_ref[1, i].astype(F32)
        prod = jnp.concatenate([xr * kr - xi * ki, xr * ki + xi * kr], axis=0)
        z = _dot(inv_ref[...].astype(BF16), prod.astype(BF16))
        o_ref[0, 0, i] = z[:n2].astype(BF16)
        o_ref[0, 1, i] = z[n2:].astype(BF16)


def _filter_mid_kernel(p_ref, q_ref, tw_ref, y_ref, ss_ref, o_ref, *, k1b, n2):
    scale = lax.rsqrt(ss_ref[...] + EPS)
    for i in range(k1b):
        s = _stage2_fwd(p_ref, q_ref, tw_ref, y_ref, i)
        o_ref[0, i] = (s[:n2] * scale).astype(o_ref.dtype)
        o_ref[1, i] = (s[n2:] * scale).astype(o_ref.dtype)


def _fnet_mid_kernel(p_ref, q_ref, tw_ref, y_ref, cc_ref, sc_ref, o_ref, *, k1b, n2):
    cb = o_ref.shape[-1]
    cc = cc_ref[...].astype(BF16)
    sc = sc_ref[...].astype(BF16)
    urs, uis = [], []
    for i in range(k1b):
        s = _stage2_fwd(p_ref, q_ref, tw_ref, y_ref, i)
        urs.append(s[:n2].astype(BF16))
        uis.append(s[n2:].astype(BF16))
    ur = jnp.concatenate(urs, axis=0)
    ui = jnp.concatenate(uis, axis=0)
    groups = []
    for lo in range(0, cb, FNET_GROUP_DIM):
        hi = lo + FNET_GROUP_DIM
        groups.append((_dot(ur[:, lo:hi], cc) + _dot(ui[:, lo:hi], sc)).astype(BF16))
    mixed = jnp.concatenate(groups, axis=1).reshape(k1b, n2, cb)
    o_ref[0] = _swap_major(mixed)


def _stage2_specs(n2, c, k1b):
    return [
        _resident((2 * n2, 2 * n2)),
        _resident((2 * n2, 2 * n2)),
        pl.BlockSpec((k1b, 2, 2 * n2), lambda k, b, *_: (k, 0, 0)),
    ], pl.BlockSpec((1, 2, k1b, n2, c), lambda k, b, *_: (b, 0, k, 0, 0))


def _fnet(proj, b, l):
    n2 = _DFT_N2
    n1 = l // n2
    c = D_FNET
    t = _dft_tables(n1, n2)
    lhs = _const(t["rows_fwd"] / math.sqrt(l))
    y = _rows_fwd(lhs, proj.reshape(b, 1, n1, n2, proj.shape[-1]), "fnet_rows", c)
    cc, sc = _cos_sin(FNET_GROUP_DIM)
    norm = 1.0 / math.sqrt(FNET_GROUP_DIM)
    k1b, cb = _ROW_GROUP, 2 * FNET_GROUP_DIM
    head, _ = _stage2_specs(n2, c, k1b)
    out = pl.pallas_call(
        functools.partial(_fnet_mid_kernel, k1b=k1b, n2=n2),
        grid=(n1 // k1b, b, c // cb),
        in_specs=head + [pl.BlockSpec((1, 2, k1b, n2, cb), lambda k, bi, ci: (bi, 0, k, 0, ci)),
                         _resident((FNET_GROUP_DIM, FNET_GROUP_DIM)),
                         _resident((FNET_GROUP_DIM, FNET_GROUP_DIM))],
        out_specs=pl.BlockSpec((1, n2, k1b, cb), lambda k, bi, ci: (bi, 0, k, ci)),
        out_shape=jax.ShapeDtypeStruct((b, n2, n1, c), BF16),
        compiler_params=_params("parallel", "parallel", "parallel"),
        name="fnet_mid",
    )(_const(t["p"]), _const(t["q"]), _const(t["tw_k1"]), y,
      _const(cc * norm), _const(sc * norm))
    return out.reshape(b * l, c)


def _shortconv_kernel(*refs, r, nt):
    u_refs, prev_refs, next_refs = refs[0:3], refs[3:6], refs[6:9]
    w_ref, b_ref, vx_ref, x1_ref = refs[9:]
    i = pl.program_id(1)
    row = lax.broadcasted_iota(jnp.int32, (r, 1), 0)
    outs = []
    for part in range(3):
        cols = slice(part * D_HYENA, (part + 1) * D_HYENA)
        u = u_refs[part][0].astype(F32)
        prev_row = prev_refs[part][0].astype(F32)[_BF16_SUBLANES - 1:_BF16_SUBLANES]
        next_row = next_refs[part][0].astype(F32)[0:1]
        prev_row = jnp.where(i == 0, 0.0, prev_row)
        next_row = jnp.where(i == nt - 1, 0.0, next_row)
        u_prev = jnp.where(row == 0, prev_row, pltpu.roll(u, 1, axis=0))
        u_next = jnp.where(row == r - 1, next_row, pltpu.roll(u, r - 1, axis=0))
        outs.append(u_prev * w_ref[0:1, cols] + u * w_ref[1:2, cols]
                    + u_next * w_ref[2:3, cols] + b_ref[:, cols])
    x1, x2, v = outs
    vx_ref[0] = (v * x2).astype(BF16)
    x1_ref[0] = x1.astype(BF16)


def _shortconv(proj3, conv_w, conv_b):
    b, l, _ = proj3.shape
    c = D_HYENA
    r = 512
    nt = l // r
    hb = _BF16_SUBLANES
    halo = r // hb
    first_block = D_FNET // c
    tiles = [pl.BlockSpec((1, r, c), lambda bi, i, p=p: (bi, i, first_block + p))
             for p in range(3)]
    prevs = [pl.BlockSpec((1, hb, c),
                          lambda bi, i, p=p: (bi, jnp.maximum(i * halo - 1, 0), first_block + p))
             for p in range(3)]
    nexts = [pl.BlockSpec((1, hb, c),
                          lambda bi, i, p=p: (bi, jnp.minimum((i + 1) * halo, l // hb - 1),
                                              first_block + p))
             for p in range(3)]
    return pl.pallas_call(
        functools.partial(_shortconv_kernel, r=r, nt=nt),
        grid=(b, nt),
        in_specs=tiles + prevs + nexts + [_resident((3, 3 * c)), _resident((1, 3 * c))],
        out_specs=[pl.BlockSpec((1, r, c), lambda bi, i: (bi, i, 0))] * 2,
        out_shape=[jax.ShapeDtypeStruct((b, l, c), BF16)] * 2,
        compiler_params=_params("parallel", "parallel"),
        name="shortconv",
    )(*([proj3] * 9), conv_w, conv_b)


def _filter_rows_kernel(bands_ref, w1t_ref, w1c_ref, w1s_ref, b1_ref, w2_ref, b2_ref,
                        w3_ref, b3_ref, fr_ref, w4_ref, deltas_ref, lhs_ref, y_ref, ss_ref,
                        decay_ref, *, l, n2, sb, chunk):
    g = pl.program_id(0)
    n1 = lhs_ref.shape[1]
    half = n1 // 2
    lanes = sb * half
    c = deltas_ref.shape[1]

    def offsets(shape, axis):
        idx = lax.broadcasted_iota(jnp.int32, shape, axis)
        back = idx >= lanes
        idx = jnp.where(back, idx - lanes, idx)
        s = lax.shift_right_logical(idx, half.bit_length() - 1)
        return back, n2 * (idx & (half - 1)) + s

    back, off = offsets((1, 2 * lanes), 1)
    lag = jnp.where(back, l - off - sb * g, off + sb * g).astype(F32)
    ang = bands_ref[...] * ((2.0 * math.pi / l) * lag)
    fr = fr_ref[...]
    pre = (w1t_ref[...] * (lag * (1.0 / (l - 1))) + _dot_hi(w1c_ref[...], jnp.cos(ang))
           - _dot_hi(w1s_ref[...], jnp.sin(ang)) + b1_ref[...])
    h = jnp.sin(fr * pre)
    h = jnp.sin(fr * (_dot_hi(w2_ref[...], h) + b2_ref[...]))
    h = jnp.sin(fr * (_dot_hi(w3_ref[...], h) + b3_ref[...]))

    @pl.when(g == 0)
    def _():
        ss_ref[...] = jnp.zeros_like(ss_ref)
        _, off_col = offsets((lanes, 1), 0)
        off_col = off_col.astype(F32)
        decay_ref[0] = jnp.exp(-(off_col * (1.0 / (l - 1))) * deltas_ref[...])
        decay_ref[1] = jnp.exp(-((l - off_col) * (1.0 / (l - 1))) * deltas_ref[...])

    shift = (sb * g).astype(F32) * (1.0 / (l - 1))
    row0 = lax.broadcasted_iota(jnp.int32, (lanes, 1), 0) == 0
    lhs = lhs_ref[...].astype(BF16)
    for lo in range(0, c, chunk):
        cols = slice(lo, lo + chunk)
        dl = deltas_ref[:, cols]
        k_fwd = (_dot_tn_bf16x3(h[:, :lanes], w4_ref[:, lo:lo + chunk])
                 * (decay_ref[0, :, cols] * jnp.exp(-shift * dl)))
        k_bwd = (_dot_tn_bf16x3(h[:, lanes:], w4_ref[:, c + lo:c + lo + chunk])
                 * (decay_ref[1, :, cols] * jnp.exp(shift * dl)))
        k_bwd = jnp.where(jnp.logical_and(row0, g == 0), 0.0, k_bwd)
        ss_ref[:, cols] += jnp.sum(k_fwd * k_fwd + k_bwd * k_bwd, axis=0, keepdims=True)
        ys = []
        for s in range(sb):
            rows = slice(s * half, (s + 1) * half)
            xs = jnp.concatenate([k_fwd[rows], k_bwd[rows]], axis=0).astype(BF16)
            ys.append(_dot(lhs, xs).astype(BF16))
        y = _swap_major(jnp.stack(ys, axis=0))
        y_ref[0, :, :, :, cols] = y.reshape(2, n1, sb, chunk)


def _filter_spectrum(l, w1, b1, w2, b2, w3, b3, w4, freq):
    n = 2 * l
    n2 = _DFT_N2
    n1 = n // n2
    c = D_HYENA
    sb = _ROW_GROUP
    t = _dft_tables(n1, n2)
    bands = np.linspace(1e-4, FILTER_BANDS - 1, FILTER_BANDS)[:, None]
    deltas = np.abs(np.linspace(math.log(DECAY_TARGET) / FAST_DECAY_PCT,
                                math.log(DECAY_TARGET) / SLOW_DECAY_PCT, D_HYENA))[None, :]
    ins = [_const(bands), w1[0:1].T, w1[1:1 + FILTER_BANDS].T, w1[1 + FILTER_BANDS:].T,
           b1[:, None], w2.T, b2[:, None], w3.T, b3[:, None], freq[:, None], w4,
           _const(deltas), _const(t["rows_fwd"])]
    y, ss = pl.pallas_call(
        functools.partial(_filter_rows_kernel, l=l, n2=n2, sb=sb, chunk=512),
        grid=(n2 // sb,),
        in_specs=[_resident(a.shape) for a in ins],
        out_specs=[pl.BlockSpec((1, 2, n1, sb, c), lambda g: (0, 0, 0, g, 0)),
                   pl.BlockSpec((1, c), lambda g: (0, 0))],
        out_shape=[jax.ShapeDtypeStruct((1, 2, n1, n2, c), BF16),
                   jax.ShapeDtypeStruct((1, c), F32)],
        scratch_shapes=[pltpu.VMEM((2, sb * (n1 // 2), c), F32)],
        compiler_params=_params("arbitrary"),
        name="filter_rows",
    )(*ins)

    k1b = 8
    head, y_spec = _stage2_specs(n2, c, k1b)
    return pl.pallas_call(
        functools.partial(_filter_mid_kernel, k1b=k1b, n2=n2),
        grid=(n1 // k1b, 1),
        in_specs=head + [y_spec, _resident((1, c))],
        out_specs=pl.BlockSpec((2, k1b, n2, c), lambda k, bi: (0, k, 0, 0)),
        out_shape=jax.ShapeDtypeStruct((2, n1, n2, c), BF16),
        compiler_params=_params("parallel", "parallel"),
        name="filter_mid",
    )(_const(t["p"]), _const(t["q"]), _const(t["tw_k1"]), y, ss)


def _conv_out_kernel(pc_ref, qc_ref, tw_ref, z_ref, vx_ref, x1_ref, skip_ref, o_ref):
    _, _, n1, sb, cb = z_ref.shape
    z = _swap_major(z_ref[0].reshape(2 * n1, sb, cb))
    ys = []
    for s in range(sb):
        lhs = (pc_ref[...] * tw_ref[s, 0:1, :] + qc_ref[...] * tw_ref[s, 1:2, :]).astype(BF16)
        ys.append(_dot(lhs, z[s]))
    y = _swap_major(jnp.stack(ys, axis=0)).reshape(o_ref.shape)
    y = y + vx_ref[...].astype(F32) * skip_ref[...]
    o_ref[...] = (y * x1_ref[...].astype(F32)).astype(BF16)


def _longconv(vx, x1, kf, skip):
    b, l, c = vx.shape
    assert b == 2, "batch pair is packed as (re, im)"
    n = 2 * l
    n2 = _DFT_N2
    n1 = n // n2
    half = n1 // 2
    t = _dft_tables(n1, n2)
    y = _rows_fwd(_const(t["rows_fwd_c"]), vx.reshape(1, 2, half, n2, c), "conv_rows")
    k1b = 8
    head, y_spec = _stage2_specs(n2, c, k1b)
    z = pl.pallas_call(
        functools.partial(_conv_mid_kernel, k1b=k1b, n2=n2),
        grid=(n1 // k1b, 1),
        in_specs=head + [_resident((2 * n2, 2 * n2)), y_spec,
                         pl.BlockSpec((2, k1b, n2, c), lambda k, bi: (0, k, 0, 0))],
        out_specs=y_spec,
        out_shape=jax.ShapeDtypeStruct((1, 2, n1, n2, c), BF16),
        compiler_params=_params("parallel", "parallel"),
        name="conv_mid",
    )(_const(t["p"]), _const(t["q"]), _const(t["tw_k1"]), _const(t["inv2"] / n), y, kf)

    sb, cb = _ROW_GROUP, 1024
    seq_spec = pl.BlockSpec((2, half, sb, cb), lambda g, ci: (0, 0, g, ci))
    out = pl.pallas_call(
        _conv_out_kernel,
        grid=(n2 // sb, c // cb),
        in_specs=[_resident((n1, 2 * n1)), _resident((n1, 2 * n1)),
                  pl.BlockSpec((sb, 2, 2 * n1), lambda g, ci: (g, 0, 0)),
                  pl.BlockSpec((1, 2, n1, sb, cb), lambda g, ci: (0, 0, 0, g, ci)),
                  seq_spec, seq_spec,
                  pl.BlockSpec((1, cb), lambda g, ci: (0, ci))],
        out_specs=seq_spec,
        out_shape=jax.ShapeDtypeStruct((2, half, n2, c), BF16),
        compiler_params=_params("parallel", "parallel"),
        name="conv_out",
    )(_const(t["pc"]), _const(t["qc"]), _const(t["tw_n2"]), z,
      vx.reshape(2, half, n2, c), x1.reshape(2, half, n2, c), skip)
    return out.reshape(b * l, c)


def _merge_kernel(mx_ref, yh_ref, ga_ref, gb_ref, x_ref, wf_ref, wh_ref, wo_ref, g2_ref,
                  x1_ref, h2_ref, m_ref, *, chunk):
    d = x_ref.shape[1]
    for lo in range(0, d, chunk):
        cols = slice(lo, lo + chunk)
        ya = _dot(mx_ref[...], wf_ref[:, cols])
        yb = _dot(yh_ref[...], wh_ref[:, cols])
        m_ref[:, cols] = (ga_ref[:, cols].astype(F32) * ya
                          + gb_ref[:, cols].astype(F32) * yb).astype(BF16)
    ss = jnp.zeros((x_ref.shape[0], 1), F32)
    for lo in range(0, d, chunk):
        cols = slice(lo, lo + chunk)
        x1 = x_ref[:, cols] + _dot(m_ref[...], wo_ref[:, cols])
        x1_ref[:, cols] = x1
        ss = ss + jnp.sum(x1 * x1, axis=-1, keepdims=True)
    scale = lax.rsqrt(ss * (1.0 / d) + EPS)
    for lo in range(0, d, chunk):
        cols = slice(lo, lo + chunk)
        h2_ref[:, cols] = (x1_ref[:, cols] * scale * g2_ref[:, cols]).astype(BF16)


def _merge(mixed, yh, proj, x2, wf, wh, wo, g2):
    rows = x2.shape[0]
    tm = 512
    d = D_MODEL
    gate_block = (proj.shape[1] - 2 * d) // d
    return pl.pallas_call(
        functools.partial(_merge_kernel, chunk=256),
        grid=(rows // tm,),
        in_specs=[
            pl.BlockSpec((tm, D_FNET), lambda i: (i, 0)),
            pl.BlockSpec((tm, D_HYENA), lambda i: (i, 0)),
            pl.BlockSpec((tm, d), lambda i: (i, gate_block)),
            pl.BlockSpec((tm, d), lambda i: (i, gate_block + 1)),
            pl.BlockSpec((tm, d), lambda i: (i, 0)),
            _resident((D_FNET, d)), _resident((D_HYENA, d)), _resident((d, d)),
            _resident((1, d)),
        ],
        out_specs=[pl.BlockSpec((tm, d), lambda i: (i, 0))] * 2,
        out_shape=[jax.ShapeDtypeStruct((rows, d), F32),
                   jax.ShapeDtypeStruct((rows, d), BF16)],
        scratch_shapes=[pltpu.VMEM((tm, d), BF16)],
        compiler_params=_params("parallel"),
        name="merge",
    )(mixed, yh, proj, proj, x2, wf, wh, wo, g2)


def _mlp_kernel(h2_ref, x1_ref, w1_ref, w2_ref, gf_ref, o_ref, *, nj):
    j = pl.program_id(1)
    slice_rows = x1_ref.shape[0]

    def step(first):
        t = _dot(h2_ref[...], w1_ref[...])
        t = jnp.square(jnp.maximum(t, 0.0)).astype(BF16)
        part = _dot(t, w2_ref[...])
        if first:
            o_ref[...] = part
        else:
            o_ref[...] += part
        rows = pl.ds(pl.multiple_of(j * slice_rows, slice_rows), slice_rows)
        o_ref[rows, :] += x1_ref[...]

    @pl.when(j == 0)
    def _():
        step(True)

    @pl.when(j > 0)
    def _():
        step(False)

    @pl.when(j == nj - 1)
    def _():
        x = o_ref[...]
        ms = jnp.mean(x * x, axis=-1, keepdims=True)
        o_ref[...] = x * lax.rsqrt(ms + EPS) * gf_ref[...]


def _mlp(h2, x1, w1, w2, gf):
    rows = x1.shape[0]
    tm, tf = 1024, 1024
    d = D_MODEL
    nj = D_FF // tf
    return pl.pallas_call(
        functools.partial(_mlp_kernel, nj=nj),
        grid=(rows // tm, nj),
        in_specs=[
            pl.BlockSpec((tm, d), lambda i, j: (i, 0)),
            pl.BlockSpec((tm // nj, d), lambda i, j: (i * nj + j, 0)),
            pl.BlockSpec((d, tf), lambda i, j: (0, j)),
            pl.BlockSpec((tf, d), lambda i, j: (j, 0)),
            pl.BlockSpec((1, d), lambda i, j: (0, 0)),
        ],
        out_specs=pl.BlockSpec((tm, d), lambda i, j: (i, 0)),
        out_shape=jax.ShapeDtypeStruct((rows, d), F32),
        compiler_params=_params("parallel", "arbitrary"),
        name="mlp",
    )(h2, x1, w1, w2, gf)


def _layer_and_final_norm(x, w, filt):
    b, l, d = x.shape
    x2 = x.reshape(b * l, d)
    proj = _project(x2, w["norm1_g"], w["w_cat"], w["b_gate"])
    mixed = _fnet(proj, b, l)
    vx, x1h = _shortconv(proj.reshape(b, l, proj.shape[-1]), w["conv_w"], w["conv_b"])
    kf = _filter_spectrum(l, *filt)
    yh = _longconv(vx, x1h, kf, w["skip"])
    x1, h2 = _merge(mixed, yh, proj, x2, w["w_fnet_map"], w["w_hyena_out"], w["w_out"],
                    w["norm2_g"])
    out = _mlp(h2, x1, w["w_mlp1"], w["w_mlp2"], w["norm_f_g"])
    return out.reshape(b, l, d)


def kernel(x_prompt, x_sample, norm1_g, w_in, conv_w, conv_b, filt_w1, filt_b1, filt_w2, filt_b2, filt_w3, filt_b3, filt_w4, filt_freq, hyena_skip, w_fnet_map, w_hyena_out, w_gate, b_gate, w_out, norm2_g, w_mlp1, w_mlp2, norm_f_g):
    assert norm1_g.shape[0] == 1, "one layer"
    w = {
        "norm1_g": norm1_g[0][None],
        "w_cat": _cast_concat([w_in[0], w_gate[0]]),
        "b_gate": b_gate[0][None],
        "conv_w": conv_w[0],
        "conv_b": conv_b[0][None],
        "skip": hyena_skip[0][None],
        "w_fnet_map": w_fnet_map[0].astype(BF16),
        "w_hyena_out": w_hyena_out[0].astype(BF16),
        "w_out": w_out[0].astype(BF16),
        "norm2_g": norm2_g[0][None],
        "w_mlp1": w_mlp1[0].astype(BF16),
        "w_mlp2": w_mlp2[0].astype(BF16),
        "norm_f_g": norm_f_g[None],
    }
    filt = (filt_w1[0], filt_b1[0], filt_w2[0], filt_b2[0], filt_w3[0], filt_b3[0],
            filt_w4[0], filt_freq[0])
    return (_layer_and_final_norm(x_prompt, w, filt),
            _layer_and_final_norm(x_sample, w, filt))
```

```python
import functools
import math

import jax
import jax.numpy as jnp
import numpy as np
from jax import lax
from jax.experimental import pallas as pl
from jax.experimental.pallas import tpu as pltpu

D_MODEL = 2048
N_FNET_GROUPS = 4
FNET_GROUP_DIM = 256
D_FNET = N_FNET_GROUPS * FNET_GROUP_DIM
D_HYENA = 1024
FILTER_BANDS = 16
FILTER_HIDDEN = 64
DECAY_TARGET = 1e-2
FAST_DECAY_PCT = 0.3
SLOW_DECAY_PCT = 1.5
D_FF = 4 * D_MODEL
EPS = 1e-6

F32 = jnp.float32
BF16 = jnp.bfloat16

_VMEM_LIMIT_BYTES = 56 * 1024 * 1024
_DFT_N2 = 128
_BF16_SUBLANES = 16
_ROW_GROUP = _BF16_SUBLANES

_dot = functools.partial(jnp.dot, preferred_element_type=F32)
_dot_hi = functools.partial(jnp.dot, preferred_element_type=F32,
                            precision=lax.Precision.HIGHEST)


def _dot_tn_bf16x3(a, b):
    dims = (((0,), (0,)), ((), ()))
    a_hi, b_hi = a.astype(BF16), b.astype(BF16)
    a_lo = (a - a_hi.astype(F32)).astype(BF16)
    b_lo = (b - b_hi.astype(F32)).astype(BF16)
    dg = functools.partial(lax.dot_general, dimension_numbers=dims, preferred_element_type=F32)
    return dg(a_hi, b_hi) + dg(a_lo, b_hi) + dg(a_hi, b_lo)


def _params(*sem):
    return pltpu.CompilerParams(dimension_semantics=sem,
                                vmem_limit_bytes=_VMEM_LIMIT_BYTES)


def _resident(shape):
    zeros = (0,) * len(shape)
    return pl.BlockSpec(shape, lambda *_: zeros, pipeline_mode=pl.Buffered(1))


def _cos_sin(n):
    jk = (np.arange(n)[:, None] * np.arange(n)[None, :]) % n
    ang = 2.0 * np.pi * jk / n
    return np.cos(ang), np.sin(ang)


@functools.lru_cache(maxsize=None)
def _dft_tables(n1, n2):
    n = n1 * n2
    c1, s1 = _cos_sin(n1)
    c2, s2 = _cos_sin(n2)
    fr1, fi1 = c1, -s1
    fr2, fi2 = c2, -s2
    ang = 2.0 * np.pi * ((np.arange(n1)[:, None] * np.arange(n2)[None, :]) % n) / n
    twr, twi = np.cos(ang), -np.sin(ang)
    t = {}
    h = n1 // 2
    t["rows_fwd"] = np.concatenate([fr1, fi1], axis=0)
    t["rows_fwd_c"] = np.block([[fr1[:, :h], -fi1[:, :h]], [fi1[:, :h], fr1[:, :h]]])
    t["p"] = np.block([[fr2, -fi2], [fi2, fr2]])
    t["q"] = np.block([[-fi2, -fr2], [fr2, -fi2]])
    t["tw_k1"] = np.stack([np.concatenate([twr, twr], axis=1),
                           np.concatenate([twi, twi], axis=1)], axis=1)
    t["inv2"] = np.block([[fr2, fi2], [-fi2, fr2]])
    t["pc"] = np.block([[fr1[:h], fi1[:h]], [-fi1[:h], fr1[:h]]])
    t["qc"] = np.block([[-fi1[:h], fr1[:h]], [-fr1[:h], -fi1[:h]]])
    t["tw_n2"] = np.stack([np.concatenate([twr.T, twr.T], axis=1),
                           np.concatenate([twi.T, twi.T], axis=1)], axis=1)
    return t


def _const(a, dtype=F32):
    return jnp.asarray(np.asarray(a, np.float32), dtype)


def _cast_concat_kernel(*refs):
    o_ref = refs[-1]
    lo = 0
    for w_ref in refs[:-1]:
        o_ref[:, lo:lo + w_ref.shape[1]] = w_ref[...].astype(BF16)
        lo += w_ref.shape[1]


def _cast_concat(ws):
    k = ws[0].shape[0]
    n = sum(w.shape[1] for w in ws)
    rb = 128
    return pl.pallas_call(
        _cast_concat_kernel,
        grid=(k // rb,),
        in_specs=[pl.BlockSpec((rb, w.shape[1]), lambda i: (i, 0)) for w in ws],
        out_specs=pl.BlockSpec((rb, n), lambda i: (i, 0)),
        out_shape=jax.ShapeDtypeStruct((k, n), BF16),
        compiler_params=_params("parallel"),
        name="cast_concat",
    )(*ws)


def _project_kernel(x_ref, g_ref, w_ref, b_ref, o_ref, h_ref, *, gate_tile, chunk):
    j = pl.program_id(1)
    tm = x_ref.shape[0]

    def emit(gated, rows):
        for lo in range(0, o_ref.shape[1], chunk):
            acc = _dot(h_ref[rows, :], w_ref[:, lo:lo + chunk])
            if gated:
                acc = jax.nn.sigmoid(acc + b_ref[:, lo:lo + chunk])
            o_ref[rows, lo:lo + chunk] = acc.astype(BF16)

    @pl.when(j == 0)
    def _():
        for r0 in range(0, tm, tm // 2):
            rows = slice(r0, r0 + tm // 2)
            x = x_ref[rows, :]
            ms = jnp.mean(x * x, axis=-1, keepdims=True)
            h_ref[rows, :] = (x * lax.rsqrt(ms + EPS) * g_ref[...]).astype(BF16)
            emit(False, rows)

    @pl.when(jnp.logical_and(j > 0, j < gate_tile))
    def _():
        emit(False, slice(None))

    @pl.when(j >= gate_tile)
    def _():
        emit(True, slice(None))


def _project(x2, g, w_cat, b_gate):
    rows = x2.shape[0]
    n_out = w_cat.shape[1]
    tm, tn = 1024, 2048
    gate_tile = (n_out - b_gate.shape[1]) // tn
    assert gate_tile >= 1, "the first column tile of a row tile is not gated"
    return pl.pallas_call(
        functools.partial(_project_kernel, gate_tile=gate_tile, chunk=tn),
        grid=(rows // tm, n_out // tn),
        in_specs=[
            pl.BlockSpec((tm, D_MODEL), lambda i, j: (i, 0)),
            pl.BlockSpec((1, D_MODEL), lambda i, j: (0, 0)),
            pl.BlockSpec((D_MODEL, tn), lambda i, j: (0, j)),
            pl.BlockSpec((1, tn), lambda i, j: (0, jnp.maximum(j - gate_tile, 0))),
        ],
        out_specs=pl.BlockSpec((tm, tn), lambda i, j: (i, j)),
        out_shape=jax.ShapeDtypeStruct((rows, n_out), BF16),
        scratch_shapes=[pltpu.VMEM((tm, D_MODEL), BF16)],
        compiler_params=_params("parallel", "arbitrary"),
        name="project",
    )(x2, g, w_cat, b_gate)


def _swap_major(x):
    return jnp.transpose(x, (1, 0, 2))


def _rows_fwd_kernel(lhs_ref, x_ref, o_ref):
    _, parts, k, sb, cb = x_ref.shape
    x = x_ref[0].astype(BF16).reshape(parts * k, sb, cb)
    xt = _swap_major(x)
    lhs = lhs_ref[...].astype(BF16)
    y = jnp.stack([_dot(lhs, xt[s]).astype(BF16) for s in range(sb)], axis=0)
    o_ref[0] = _swap_major(y).reshape(o_ref.shape[1:])


def _rows_fwd(lhs, x5, name, c=None):
    b, parts, k, n2, _ = x5.shape
    c = c or x5.shape[-1]
    n1 = lhs.shape[0] // 2
    sb, cb = _ROW_GROUP, 1024
    return pl.pallas_call(
        _rows_fwd_kernel,
        grid=(b, n2 // sb, c // cb),
        in_specs=[_resident(lhs.shape),
                  pl.BlockSpec((1, parts, k, sb, cb), lambda bi, g, ci: (bi, 0, 0, g, ci))],
        out_specs=pl.BlockSpec((1, 2, n1, sb, cb), lambda bi, g, ci: (bi, 0, 0, g, ci)),
        out_shape=jax.ShapeDtypeStruct((b, 2, n1, n2, c), BF16),
        compiler_params=_params("parallel", "parallel", "parallel"),
        name=name,
    )(lhs, x5)


def _stage2_fwd(p_ref, q_ref, tw_ref, y_ref, i):
    g = (p_ref[...] * tw_ref[i, 0:1, :] + q_ref[...] * tw_ref[i, 1:2, :]).astype(BF16)
    ys = jnp.concatenate([y_ref[0, 0, i], y_ref[0, 1, i]], axis=0)
    return _dot(g, ys)


def _conv_mid_kernel(p_ref, q_ref, tw_ref, inv_ref, y_ref, kf_ref, o_ref, *, k1b, n2):
    for i in range(k1b):
        s = _stage2_fwd(p_ref, q_ref, tw_ref, y_ref, i)
        xr, xi = s[:n2], s[n2:]
        kr, ki = kf_ref[0, i].astype(F32), kf_ref[1, i].astype(F32)
        prod = jnp.concatenate([xr * kr - xi * ki, xr * ki + xi * kr], axis=0)
        z = _dot(inv_ref[...].astype(BF16), prod.astype(BF16))
        o_ref[0, 0, i] = z[:n2].astype(BF16)
        o_ref[0, 1, i] = z[n2:].astype(BF16)


def _filter_mid_kernel(p_ref, q_ref, tw_ref, y_ref, ss_ref, o_ref, *, k1b, n2):
    scale = lax.rsqrt(ss_ref[...] + EPS)
    for i in range(k1b):
        s = _stage2_fwd(p_ref, q_ref, tw_ref, y_ref, i)
        o_ref[0, i] = (s[:n2] * scale).astype(o_ref.dtype)
        o_ref[1, i] = (s[n2:] * scale).astype(o_ref.dtype)


def _fnet_mid_kernel(p_ref, q_ref, tw_ref, y_ref, cc_ref, sc_ref, o_ref, *, k1b, n2):
    cb = o_ref.shape[-1]
    cc = cc_ref[...].astype(BF16)
    sc = sc_ref[...].astype(BF16)
    urs, uis = [], []
    for i in range(k1b):
        s = _stage2_fwd(p_ref, q_ref, tw_ref, y_ref, i)
        urs.append(s[:n2].astype(BF16))
        uis.append(s[n2:].astype(BF16))
    ur = jnp.concatenate(urs, axis=0)
    ui = jnp.concatenate(uis, axis=0)
    groups = []
    for lo in range(0, cb, FNET_GROUP_DIM):
        hi = lo + FNET_GROUP_DIM
        groups.append((_dot(ur[:, lo:hi], cc) + _dot(ui[:, lo:hi], sc)).astype(BF16))
    mixed = jnp.concatenate(groups, axis=1).reshape(k1b, n2, cb)
    o_ref[0] = _swap_major(mixed)


def _stage2_specs(n2, c, k1b):
    return [
        _resident((2 * n2, 2 * n2)),
        _resident((2 * n2, 2 * n2)),
        pl.BlockSpec((k1b, 2, 2 * n2), lambda k, b, *_: (k, 0, 0)),
    ], pl.BlockSpec((1, 2, k1b, n2, c), lambda k, b, *_: (b, 0, k, 0, 0))


def _fnet(proj, b, l):
    n2 = _DFT_N2
    n1 = l // n2
    c = D_FNET
    t = _dft_tables(n1, n2)
    lhs = _const(t["rows_fwd"] / math.sqrt(l))
    y = _rows_fwd(lhs, proj.reshape(b, 1, n1, n2, proj.shape[-1]), "fnet_rows", c)
    cc, sc = _cos_sin(FNET_GROUP_DIM)
    norm = 1.0 / math.sqrt(FNET_GROUP_DIM)
    k1b, cb = _ROW_GROUP, 4 * FNET_GROUP_DIM
    head, _ = _stage2_specs(n2, c, k1b)
    out = pl.pallas_call(
        functools.partial(_fnet_mid_kernel, k1b=k1b, n2=n2),
        grid=(n1 // k1b, b, c // cb),
        in_specs=head + [pl.BlockSpec((1, 2, k1b, n2, cb), lambda k, bi, ci: (bi, 0, k, 0, ci)),
                         _resident((FNET_GROUP_DIM, FNET_GROUP_DIM)),
                         _resident((FNET_GROUP_DIM, FNET_GROUP_DIM))],
        out_specs=pl.BlockSpec((1, n2, k1b, cb), lambda k, bi, ci: (bi, 0, k, ci)),
        out_shape=jax.ShapeDtypeStruct((b, n2, n1, c), BF16),
        compiler_params=_params("parallel", "parallel", "parallel"),
        name="fnet_mid",
    )(_const(t["p"]), _const(t["q"]), _const(t["tw_k1"]), y,
      _const(cc * norm), _const(sc * norm))
    return out.reshape(b * l, c)


def _shortconv_kernel(*refs, r, nt):
    u_refs, prev_refs, next_refs = refs[0:3], refs[3:6], refs[6:9]
    w_ref, b_ref, vx_ref, x1_ref = refs[9:]
    i = pl.program_id(1)
    row = lax.broadcasted_iota(jnp.int32, (r, 1), 0)
    outs = []
    for part in range(3):
        cols = slice(part * D_HYENA, (part + 1) * D_HYENA)
        u = u_refs[part][0].astype(F32)
        prev_row = prev_refs[part][0].astype(F32)[_BF16_SUBLANES - 1:_BF16_SUBLANES]
        next_row = next_refs[part][0].astype(F32)[0:1]
        prev_row = jnp.where(i == 0, 0.0, prev_row)
        next_row = jnp.where(i == nt - 1, 0.0, next_row)
        u_prev = jnp.where(row == 0, prev_row, pltpu.roll(u, 1, axis=0))
        u_next = jnp.where(row == r - 1, next_row, pltpu.roll(u, r - 1, axis=0))
        outs.append(u_prev * w_ref[0:1, cols] + u * w_ref[1:2, cols]
                    + u_next * w_ref[2:3, cols] + b_ref[:, cols])
    x1, x2, v = outs
    vx_ref[0] = (v * x2).astype(BF16)
    x1_ref[0] = x1.astype(BF16)


def _shortconv(proj3, conv_w, conv_b):
    b, l, _ = proj3.shape
    c = D_HYENA
    r = 1024
    nt = l // r
    hb = _BF16_SUBLANES
    halo = r // hb
    first_block = D_FNET // c
    tiles = [pl.BlockSpec((1, r, c), lambda bi, i, p=p: (bi, i, first_block + p))
             for p in range(3)]
    prevs = [pl.BlockSpec((1, hb, c),
                          lambda bi, i, p=p: (bi, jnp.maximum(i * halo - 1, 0), first_block + p))
             for p in range(3)]
    nexts = [pl.BlockSpec((1, hb, c),
                          lambda bi, i, p=p: (bi, jnp.minimum((i + 1) * halo, l // hb - 1),
                                              first_block + p))
             for p in range(3)]
    return pl.pallas_call(
        functools.partial(_shortconv_kernel, r=r, nt=nt),
        grid=(b, nt),
        in_specs=tiles + prevs + nexts + [_resident((3, 3 * c)), _resident((1, 3 * c))],
        out_specs=[pl.BlockSpec((1, r, c), lambda bi, i: (bi, i, 0))] * 2,
        out_shape=[jax.ShapeDtypeStruct((b, l, c), BF16)] * 2,
        compiler_params=_params("parallel", "parallel"),
        name="shortconv",
    )(*([proj3] * 9), conv_w, conv_b)


def _filter_rows_kernel(bands_ref, w1t_ref, w1c_ref, w1s_ref, b1_ref, w2_ref, b2_ref,
                        w3_ref, b3_ref, fr_ref, w4_ref, deltas_ref, lhs_ref, y_ref, ss_ref,
                        decay_ref, *, l, n2, sb, chunk):
    g = pl.program_id(0)
    n1 = lhs_ref.shape[1]
    half = n1 // 2
    lanes = sb * half
    c = deltas_ref.shape[1]

    def offsets(shape, axis):
        idx = lax.broadcasted_iota(jnp.int32, shape, axis)
        back = idx >= lanes
        idx = jnp.where(back, idx - lanes, idx)
        s = lax.shift_right_logical(idx, half.bit_length() - 1)
        return back, n2 * (idx & (half - 1)) + s

    back, off = offsets((1, 2 * lanes), 1)
    lag = jnp.where(back, l - off - sb * g, off + sb * g).astype(F32)
    ang = bands_ref[...] * ((2.0 * math.pi / l) * lag)
    fr = fr_ref[...]
    pre = (w1t_ref[...] * (lag * (1.0 / (l - 1))) + _dot_hi(w1c_ref[...], jnp.cos(ang))
           - _dot_hi(w1s_ref[...], jnp.sin(ang)) + b1_ref[...])
    h = jnp.sin(fr * pre)
    h = jnp.sin(fr * (_dot_hi(w2_ref[...], h) + b2_ref[...]))
    h = jnp.sin(fr * (_dot_hi(w3_ref[...], h) + b3_ref[...]))

    @pl.when(g == 0)
    def _():
        ss_ref[...] = jnp.zeros_like(ss_ref)
        _, off_col = offsets((lanes, 1), 0)
        off_col = off_col.astype(F32)
        decay_ref[0] = jnp.exp(-(off_col * (1.0 / (l - 1))) * deltas_ref[...])
        decay_ref[1] = jnp.exp(-((l - off_col) * (1.0 / (l - 1))) * deltas_ref[...])

    shift = (sb * g).astype(F32) * (1.0 / (l - 1))
    row0 = lax.broadcasted_iota(jnp.int32, (lanes, 1), 0) == 0
    lhs = lhs_ref[...].astype(BF16)
    for lo in range(0, c, chunk):
        cols = slice(lo, lo + chunk)
        dl = deltas_ref[:, cols]
        k_fwd = (_dot_tn_bf16x3(h[:, :lanes], w4_ref[:, lo:lo + chunk])
                 * (decay_ref[0, :, cols] * jnp.exp(-shift * dl)))
        k_bwd = (_dot_tn_bf16x3(h[:, lanes:], w4_ref[:, c + lo:c + lo + chunk])
                 * (decay_ref[1, :, cols] * jnp.exp(shift * dl)))
        k_bwd = jnp.where(jnp.logical_and(row0, g == 0), 0.0, k_bwd)
        ss_ref[:, cols] += jnp.sum(k_fwd * k_fwd + k_bwd * k_bwd, axis=0, keepdims=True)
        ys = []
        for s in range(sb):
            rows = slice(s * half, (s + 1) * half)
            xs = jnp.concatenate([k_fwd[rows], k_bwd[rows]], axis=0).astype(BF16)
            ys.append(_dot(lhs, xs).astype(BF16))
        y = _swap_major(jnp.stack(ys, axis=0))
        y_ref[0, :, :, :, cols] = y.reshape(2, n1, sb, chunk)


def _filter_spectrum(l, w1, b1, w2, b2, w3, b3, w4, freq):
    n = 2 * l
    n2 = _DFT_N2
    n1 = n // n2
    c = D_HYENA
    sb = _ROW_GROUP
    t = _dft_tables(n1, n2)
    bands = np.linspace(1e-4, FILTER_BANDS - 1, FILTER_BANDS)[:, None]
    deltas = np.abs(np.linspace(math.log(DECAY_TARGET) / FAST_DECAY_PCT,
                                math.log(DECAY_TARGET) / SLOW_DECAY_PCT, D_HYENA))[None, :]
    ins = [_const(bands), w1[0:1].T, w1[1:1 + FILTER_BANDS].T, w1[1 + FILTER_BANDS:].T,
           b1[:, None], w2.T, b2[:, None], w3.T, b3[:, None], freq[:, None], w4,
           _const(deltas), _const(t["rows_fwd"])]
    y, ss = pl.pallas_call(
        functools.partial(_filter_rows_kernel, l=l, n2=n2, sb=sb, chunk=512),
        grid=(n2 // sb,),
        in_specs=[_resident(a.shape) for a in ins],
        out_specs=[pl.BlockSpec((1, 2, n1, sb, c), lambda g: (0, 0, 0, g, 0)),
                   pl.BlockSpec((1, c), lambda g: (0, 0))],
        out_shape=[jax.ShapeDtypeStruct((1, 2, n1, n2, c), BF16),
                   jax.ShapeDtypeStruct((1, c), F32)],
        scratch_shapes=[pltpu.VMEM((2, sb * (n1 // 2), c), F32)],
        compiler_params=_params("arbitrary"),
        name="filter_rows",
    )(*ins)

    k1b = 8
    head, y_spec = _stage2_specs(n2, c, k1b)
    return pl.pallas_call(
        functools.partial(_filter_mid_kernel, k1b=k1b, n2=n2),
        grid=(n1 // k1b, 1),
        in_specs=head + [y_spec, _resident((1, c))],
        out_specs=pl.BlockSpec((2, k1b, n2, c), lambda k, bi: (0, k, 0, 0)),
        out_shape=jax.ShapeDtypeStruct((2, n1, n2, c), BF16),
        compiler_params=_params("parallel", "parallel"),
        name="filter_mid",
    )(_const(t["p"]), _const(t["q"]), _const(t["tw_k1"]), y, ss)


def _conv_out_kernel(pc_ref, qc_ref, tw_ref, z_ref, vx_ref, x1_ref, skip_ref, o_ref):
    _, _, n1, sb, cb = z_ref.shape
    z = _swap_major(z_ref[0].reshape(2 * n1, sb, cb))
    ys = []
    for s in range(sb):
        lhs = (pc_ref[...] * tw_ref[s, 0:1, :] + qc_ref[...] * tw_ref[s, 1:2, :]).astype(BF16)
        ys.append(_dot(lhs, z[s]))
    y = _swap_major(jnp.stack(ys, axis=0)).reshape(o_ref.shape)
    y = y + vx_ref[...].astype(F32) * skip_ref[...]
    o_ref[...] = (y * x1_ref[...].astype(F32)).astype(BF16)


def _longconv(vx, x1, kf, skip):
    b, l, c = vx.shape
    assert b == 2, "batch pair is packed as (re, im)"
    n = 2 * l
    n2 = _DFT_N2
    n1 = n // n2
    half = n1 // 2
    t = _dft_tables(n1, n2)
    y = _rows_fwd(_const(t["rows_fwd_c"]), vx.reshape(1, 2, half, n2, c), "conv_rows")
    k1b = 8
    head, y_spec = _stage2_specs(n2, c, k1b)
    z = pl.pallas_call(
        functools.partial(_conv_mid_kernel, k1b=k1b, n2=n2),
        grid=(n1 // k1b, 1),
        in_specs=head + [_resident((2 * n2, 2 * n2)), y_spec,
                         pl.BlockSpec((2, k1b, n2, c), lambda k, bi: (0, k, 0, 0))],
        out_specs=y_spec,
        out_shape=jax.ShapeDtypeStruct((1, 2, n1, n2, c), BF16),
        compiler_params=_params("parallel", "parallel"),
        name="conv_mid",
    )(_const(t["p"]), _const(t["q"]), _const(t["tw_k1"]), _const(t["inv2"] / n), y, kf)

    sb, cb = _ROW_GROUP, 1024
    seq_spec = pl.BlockSpec((2, half, sb, cb), lambda g, ci: (0, 0, g, ci))
    out = pl.pallas_call(
        _conv_out_kernel,
        grid=(n2 // sb, c // cb),
        in_specs=[_resident((n1, 2 * n1)), _resident((n1, 2 * n1)),
                  pl.BlockSpec((sb, 2, 2 * n1), lambda g, ci: (g, 0, 0)),
                  pl.BlockSpec((1, 2, n1, sb, cb), lambda g, ci: (0, 0, 0, g, ci)),
                  seq_spec, seq_spec,
                  pl.BlockSpec((1, cb), lambda g, ci: (0, ci))],
        out_specs=seq_spec,
        out_shape=jax.ShapeDtypeStruct((2, half, n2, c), BF16),
        compiler_params=_params("parallel", "parallel"),
        name="conv_out",
    )(_const(t["pc"]), _const(t["qc"]), _const(t["tw_n2"]), z,
      vx.reshape(2, half, n2, c), x1.reshape(2, half, n2, c), skip)
    return out.reshape(b * l, c)


def _merge_kernel(mx_ref, yh_ref, ga_ref, gb_ref, x_ref, wf_ref, wh_ref, wo_ref, g2_ref,
                  x1_ref, h2_ref, m_ref, *, chunk):
    d = x_ref.shape[1]
    for lo in range(0, d, chunk):
        cols = slice(lo, lo + chunk)
        ya = _dot(mx_ref[...], wf_ref[:, cols])
        yb = _dot(yh_ref[...], wh_ref[:, cols])
        m_ref[:, cols] = (ga_ref[:, cols].astype(F32) * ya
                          + gb_ref[:, cols].astype(F32) * yb).astype(BF16)
    ss = jnp.zeros((x_ref.shape[0], 1), F32)
    for lo in range(0, d, chunk):
        cols = slice(lo, lo + chunk)
        x1 = x_ref[:, cols] + _dot(m_ref[...], wo_ref[:, cols])
        x1_ref[:, cols] = x1
        ss = ss + jnp.sum(x1 * x1, axis=-1, keepdims=True)
    scale = lax.rsqrt(ss * (1.0 / d) + EPS)
    for lo in range(0, d, chunk):
        cols = slice(lo, lo + chunk)
        h2_ref[:, cols] = (x1_ref[:, cols] * scale * g2_ref[:, cols]).astype(BF16)


def _merge(mixed, yh, proj, x2, wf, wh, wo, g2):
    rows = x2.shape[0]
    tm = 512
    d = D_MODEL
    gate_block = (proj.shape[1] - 2 * d) // d
    return pl.pallas_call(
        functools.partial(_merge_kernel, chunk=256),
        grid=(rows // tm,),
        in_specs=[
            pl.BlockSpec((tm, D_FNET), lambda i: (i, 0)),
            pl.BlockSpec((tm, D_HYENA), lambda i: (i, 0)),
            pl.BlockSpec((tm, d), lambda i: (i, gate_block)),
            pl.BlockSpec((tm, d), lambda i: (i, gate_block + 1)),
            pl.BlockSpec((tm, d), lambda i: (i, 0)),
            _resident((D_FNET, d)), _resident((D_HYENA, d)), _resident((d, d)),
            _resident((1, d)),
        ],
        out_specs=[pl.BlockSpec((tm, d), lambda i: (i, 0))] * 2,
        out_shape=[jax.ShapeDtypeStruct((rows, d), F32),
                   jax.ShapeDtypeStruct((rows, d), BF16)],
        scratch_shapes=[pltpu.VMEM((tm, d), BF16)],
        compiler_params=_params("parallel"),
        name="merge",
    )(mixed, yh, proj, proj, x2, wf, wh, wo, g2)


def _mlp_kernel(h2_ref, x1_ref, w1_ref, w2_ref, gf_ref, o_ref, *, nj):
    j = pl.program_id(1)
    slice_rows = x1_ref.shape[0]

    def step(first):
        t = _dot(h2_ref[...], w1_ref[...])
        t = jnp.square(jnp.maximum(t, 0.0)).astype(BF16)
        part = _dot(t, w2_ref[...])
        if first:
            o_ref[...] = part
        else:
            o_ref[...] += part
        rows = pl.ds(pl.multiple_of(j * slice_rows, slice_rows), slice_rows)
        o_ref[rows, :] += x1_ref[...]

    @pl.when(j == 0)
    def _():
        step(True)

    @pl.when(j > 0)
    def _():
        step(False)

    @pl.when(j == nj - 1)
    def _():
        x = o_ref[...]
        ms = jnp.mean(x * x, axis=-1, keepdims=True)
        o_ref[...] = x * lax.rsqrt(ms + EPS) * gf_ref[...]


def _mlp(h2, x1, w1, w2, gf):
    rows = x1.shape[0]
    tm, tf = 1024, 1024
    d = D_MODEL
    nj = D_FF // tf
    return pl.pallas_call(
        functools.partial(_mlp_kernel, nj=nj),
        grid=(rows // tm, nj),
        in_specs=[
            pl.BlockSpec((tm, d), lambda i, j: (i, 0)),
            pl.BlockSpec((tm // nj, d), lambda i, j: (i * nj + j, 0)),
            pl.BlockSpec((d, tf), lambda i, j: (0, j)),
            pl.BlockSpec((tf, d), lambda i, j: (j, 0)),
            pl.BlockSpec((1, d), lambda i, j: (0, 0)),
        ],
        out_specs=pl.BlockSpec((tm, d), lambda i, j: (i, 0)),
        out_shape=jax.ShapeDtypeStruct((rows, d), F32),
        compiler_params=_params("parallel", "arbitrary"),
        name="mlp",
    )(h2, x1, w1, w2, gf)


def _layer_and_final_norm(x, w, filt):
    b, l, d = x.shape
    x2 = x.reshape(b * l, d)
    proj = _project(x2, w["norm1_g"], w["w_cat"], w["b_gate"])
    mixed = _fnet(proj, b, l)
    vx, x1h = _shortconv(proj.reshape(b, l, proj.shape[-1]), w["conv_w"], w["conv_b"])
    kf = _filter_spectrum(l, *filt)
    yh = _longconv(vx, x1h, kf, w["skip"])
    x1, h2 = _merge(mixed, yh, proj, x2, w["w_fnet_map"], w["w_hyena_out"], w["w_out"],
                    w["norm2_g"])
    out = _mlp(h2, x1, w["w_mlp1"], w["w_mlp2"], w["norm_f_g"])
    return out.reshape(b, l, d)


def kernel(x_prompt, x_sample, norm1_g, w_in, conv_w, conv_b, filt_w1, filt_b1, filt_w2, filt_b2, filt_w3, filt_b3, filt_w4, filt_freq, hyena_skip, w_fnet_map, w_hyena_out, w_gate, b_gate, w_out, norm2_g, w_mlp1, w_mlp2, norm_f_g):
    assert norm1_g.shape[0] == 1, "one layer"
    w = {
        "norm1_g": norm1_g[0][None],
        "w_cat": _cast_concat([w_in[0], w_gate[0]]),
        "b_gate": b_gate[0][None],
        "conv_w": conv_w[0],
        "conv_b": conv_b[0][None],
        "skip": hyena_skip[0][None],
        "w_fnet_map": w_fnet_map[0].astype(BF16),
        "w_hyena_out": w_hyena_out[0].astype(BF16),
        "w_out": w_out[0].astype(BF16),
        "norm2_g": norm2_g[0][None],
        "w_mlp1": w_mlp1[0].astype(BF16),
        "w_mlp2": w_mlp2[0].astype(BF16),
        "norm_f_g": norm_f_g[None],
    }
    filt = (filt_w1[0], filt_b1[0], filt_w2[0], filt_b2[0], filt_w3[0], filt_b3[0],
            filt_w4[0], filt_freq[0])
    return (_layer_and_final_norm(x_prompt, w, filt),
            _layer_and_final_norm(x_sample, w, filt))
```

```python
import functools
import math

import jax
import jax.numpy as jnp
import numpy as np
from jax import lax
from jax.experimental import pallas as pl
from jax.experimental.pallas import tpu as pltpu

D_MODEL = 2048
N_FNET_GROUPS = 4
FNET_GROUP_DIM = 256
D_FNET = N_FNET_GROUPS * FNET_GROUP_DIM
D_HYENA = 1024
FILTER_BANDS = 16
FILTER_HIDDEN = 64
DECAY_TARGET = 1e-2
FAST_DECAY_PCT = 0.3
SLOW_DECAY_PCT = 1.5
D_FF = 4 * D_MODEL
EPS = 1e-6

F32 = jnp.float32
BF16 = jnp.bfloat16

_VMEM_LIMIT_BYTES = 56 * 1024 * 1024
_DFT_N2 = 128
_BF16_SUBLANES = 16
_ROW_GROUP = _BF16_SUBLANES

_dot = functools.partial(jnp.dot, preferred_element_type=F32)
_dot_hi = functools.partial(jnp.dot, preferred_element_type=F32,
                            precision=lax.Precision.HIGHEST)


def _dot_tn_bf16x3(a, b):
    dims = (((0,), (0,)), ((), ()))
    a_hi, b_hi = a.astype(BF16), b.astype(BF16)
    a_lo = (a - a_hi.astype(F32)).astype(BF16)
    b_lo = (b - b_hi.astype(F32)).astype(BF16)
    dg = functools.partial(lax.dot_general, dimension_numbers=dims, preferred_element_type=F32)
    return dg(a_hi, b_hi) + dg(a_lo, b_hi) + dg(a_hi, b_lo)


def _params(*sem):
    return pltpu.CompilerParams(dimension_semantics=sem,
                                vmem_limit_bytes=_VMEM_LIMIT_BYTES)


def _resident(shape):
    zeros = (0,) * len(shape)
    return pl.BlockSpec(shape, lambda *_: zeros, pipeline_mode=pl.Buffered(1))


def _cos_sin(n):
    jk = (np.arange(n)[:, None] * np.arange(n)[None, :]) % n
    ang = 2.0 * np.pi * jk / n
    return np.cos(ang), np.sin(ang)


@functools.lru_cache(maxsize=None)
def _dft_tables(n1, n2):
    n = n1 * n2
    c1, s1 = _cos_sin(n1)
    c2, s2 = _cos_sin(n2)
    fr1, fi1 = c1, -s1
    fr2, fi2 = c2, -s2
    ang = 2.0 * np.pi * ((np.arange(n1)[:, None] * np.arange(n2)[None, :]) % n) / n
    twr, twi = np.cos(ang), -np.sin(ang)
    t = {}
    h = n1 // 2
    t["rows_fwd"] = np.concatenate([fr1, fi1], axis=0)
    t["rows_fwd_c"] = np.block([[fr1[:, :h], -fi1[:, :h]], [fi1[:, :h], fr1[:, :h]]])
    t["p"] = np.block([[fr2, -fi2], [fi2, fr2]])
    t["q"] = np.block([[-fi2, -fr2], [fr2, -fi2]])
    t["tw_k1"] = np.stack([np.concatenate([twr, twr], axis=1),
                           np.concatenate([twi, twi], axis=1)], axis=1)
    t["inv2"] = np.block([[fr2, fi2], [-fi2, fr2]])
    t["pc"] = np.block([[fr1[:h], fi1[:h]], [-fi1[:h], fr1[:h]]])
    t["qc"] = np.block([[-fi1[:h], fr1[:h]], [-fr1[:h], -fi1[:h]]])
    t["tw_n2"] = np.stack([np.concatenate([twr.T, twr.T], axis=1),
                           np.concatenate([twi.T, twi.T], axis=1)], axis=1)
    return t


def _const(a, dtype=F32):
    return jnp.asarray(np.asarray(a, np.float32), dtype)


def _cast_concat_kernel(*refs):
    o_ref = refs[-1]
    lo = 0
    for w_ref in refs[:-1]:
        o_ref[:, lo:lo + w_ref.shape[1]] = w_ref[...].astype(BF16)
        lo += w_ref.shape[1]


def _cast_concat(ws):
    k = ws[0].shape[0]
    n = sum(w.shape[1] for w in ws)
    rb = 128
    return pl.pallas_call(
        _cast_concat_kernel,
        grid=(k // rb,),
        in_specs=[pl.BlockSpec((rb, w.shape[1]), lambda i: (i, 0)) for w in ws],
        out_specs=pl.BlockSpec((rb, n), lambda i: (i, 0)),
        out_shape=jax.ShapeDtypeStruct((k, n), BF16),
        compiler_params=_params("parallel"),
        name="cast_concat",
    )(*ws)


def _project_kernel(x_ref, g_ref, w_ref, b_ref, o_ref, h_ref, *, gate_tile, chunk):
    j = pl.program_id(1)
    tm = x_ref.shape[0]

    def emit(gated, rows):
        for lo in range(0, o_ref.shape[1], chunk):
            acc = _dot(h_ref[rows, :], w_ref[:, lo:lo + chunk])
            if gated:
                acc = jax.nn.sigmoid(acc + b_ref[:, lo:lo + chunk])
            o_ref[rows, lo:lo + chunk] = acc.astype(BF16)

    @pl.when(j == 0)
    def _():
        for r0 in range(0, tm, tm // 2):
            rows = slice(r0, r0 + tm // 2)
            x = x_ref[rows, :]
            ms = jnp.mean(x * x, axis=-1, keepdims=True)
            h_ref[rows, :] = (x * lax.rsqrt(ms + EPS) * g_ref[...]).astype(BF16)
            emit(False, rows)

    @pl.when(jnp.logical_and(j > 0, j < gate_tile))
    def _():
        emit(False, slice(None))

    @pl.when(j >= gate_tile)
    def _():
        emit(True, slice(None))


def _project(x2, g, w_cat, b_gate):
    rows = x2.shape[0]
    n_out = w_cat.shape[1]
    tm, tn = 1024, 2048
    gate_tile = (n_out - b_gate.shape[1]) // tn
    assert gate_tile >= 1, "the first column tile of a row tile is not gated"
    return pl.pallas_call(
        functools.partial(_project_kernel, gate_tile=gate_tile, chunk=tn),
        grid=(rows // tm, n_out // tn),
        in_specs=[
            pl.BlockSpec((tm, D_MODEL), lambda i, j: (i, 0)),
            pl.BlockSpec((1, D_MODEL), lambda i, j: (0, 0)),
            pl.BlockSpec((D_MODEL, tn), lambda i, j: (0, j)),
            pl.BlockSpec((1, tn), lambda i, j: (0, jnp.maximum(j - gate_tile, 0))),
        ],
        out_specs=pl.BlockSpec((tm, tn), lambda i, j: (i, j)),
        out_shape=jax.ShapeDtypeStruct((rows, n_out), BF16),
        scratch_shapes=[pltpu.VMEM((tm, D_MODEL), BF16)],
        compiler_params=_params("parallel", "arbitrary"),
        name="project",
    )(x2, g, w_cat, b_gate)


def _swap_major(x):
    return jnp.transpose(x, (1, 0, 2))


def _rows_fwd_kernel(lhs_ref, x_ref, o_ref):
    _, parts, k, sb, cb = x_ref.shape
    x = x_ref[0].astype(BF16).reshape(parts * k, sb, cb)
    xt = _swap_major(x)
    lhs = lhs_ref[...].astype(BF16)
    y = jnp.stack([_dot(lhs, xt[s]).astype(BF16) for s in range(sb)], axis=0)
    o_ref[0] = _swap_major(y).reshape(o_ref.shape[1:])


def _rows_fwd(lhs, x5, name, c=None):
    b, parts, k, n2, _ = x5.shape
    c = c or x5.shape[-1]
    n1 = lhs.shape[0] // 2
    sb, cb = _ROW_GROUP, 1024
    return pl.pallas_call(
        _rows_fwd_kernel,
        grid=(b, n2 // sb, c // cb),
        in_specs=[_resident(lhs.shape),
                  pl.BlockSpec((1, parts, k, sb, cb), lambda bi, g, ci: (bi, 0, 0, g, ci))],
        out_specs=pl.BlockSpec((1, 2, n1, sb, cb), lambda bi, g, ci: (bi, 0, 0, g, ci)),
        out_shape=jax.ShapeDtypeStruct((b, 2, n1, n2, c), BF16),
        compiler_params=_params("parallel", "parallel", "parallel"),
        name=name,
    )(lhs, x5)


def _stage2_fwd(p_ref, q_ref, tw_ref, y_ref, i):
    g = (p_ref[...] * tw_ref[i, 0:1, :] + q_ref[...] * tw_ref[i, 1:2, :]).astype(BF16)
    ys = jnp.concatenate([y_ref[0, 0, i], y_ref[0, 1, i]], axis=0)
    return _dot(g, ys)


def _conv_mid_kernel(p_ref, q_ref, tw_ref, inv_ref, y_ref, yk_ref, ss_ref, o_ref, *, k1b, n2):
    c = y_ref.shape[-1]
    scale = lax.rsqrt(ss_ref[...] + EPS)
    inv = inv_ref[...].astype(BF16)
    for i in range(k1b):
        g = (p_ref[...] * tw_ref[i, 0:1, :] + q_ref[...] * tw_ref[i, 1:2, :]).astype(BF16)
        stacked = jnp.concatenate(
            [jnp.concatenate([y_ref[0, 0, i], yk_ref[0, 0, i]], axis=1),
             jnp.concatenate([y_ref[0, 1, i], yk_ref[0, 1, i]], axis=1)], axis=0)
        s = _dot(g, stacked)
        xr, xi = s[:n2, :c], s[n2:, :c]
        kr, ki = s[:n2, c:] * scale, s[n2:, c:] * scale
        prod = jnp.concatenate([xr * kr - xi * ki, xr * ki + xi * kr], axis=0)
        z = _dot(inv, prod.astype(BF16))
        o_ref[0, 0, i] = z[:n2].astype(BF16)
        o_ref[0, 1, i] = z[n2:].astype(BF16)


def _fnet_mid_kernel(p_ref, q_ref, tw_ref, y_ref, cc_ref, sc_ref, o_ref, *, k1b, n2):
    cb = o_ref.shape[-1]
    cc = cc_ref[...].astype(BF16)
    sc = sc_ref[...].astype(BF16)
    urs, uis = [], []
    for i in range(k1b):
        s = _stage2_fwd(p_ref, q_ref, tw_ref, y_ref, i)
        urs.append(s[:n2].astype(BF16))
        uis.append(s[n2:].astype(BF16))
    ur = jnp.concatenate(urs, axis=0)
    ui = jnp.concatenate(uis, axis=0)
    groups = []
    for lo in range(0, cb, FNET_GROUP_DIM):
        hi = lo + FNET_GROUP_DIM
        groups.append((_dot(ur[:, lo:hi], cc) + _dot(ui[:, lo:hi], sc)).astype(BF16))
    mixed = jnp.concatenate(groups, axis=1).reshape(k1b, n2, cb)
    o_ref[0] = _swap_major(mixed)


def _stage2_specs(n2, c, k1b):
    return [
        _resident((2 * n2, 2 * n2)),
        _resident((2 * n2, 2 * n2)),
        pl.BlockSpec((k1b, 2, 2 * n2), lambda k, b, *_: (k, 0, 0)),
    ], pl.BlockSpec((1, 2, k1b, n2, c), lambda k, b, *_: (b, 0, k, 0, 0))


def _fnet(proj, b, l):
    n2 = _DFT_N2
    n1 = l // n2
    c = D_FNET
    t = _dft_tables(n1, n2)
    lhs = _const(t["rows_fwd"] / math.sqrt(l))
    y = _rows_fwd(lhs, proj.reshape(b, 1, n1, n2, proj.shape[-1]), "fnet_rows", c)
    cc, sc = _cos_sin(FNET_GROUP_DIM)
    norm = 1.0 / math.sqrt(FNET_GROUP_DIM)
    k1b, cb = _ROW_GROUP, 4 * FNET_GROUP_DIM
    head, _ = _stage2_specs(n2, c, k1b)
    out = pl.pallas_call(
        functools.partial(_fnet_mid_kernel, k1b=k1b, n2=n2),
        grid=(n1 // k1b, b, c // cb),
        in_specs=head + [pl.BlockSpec((1, 2, k1b, n2, cb), lambda k, bi, ci: (bi, 0, k, 0, ci)),
                         _resident((FNET_GROUP_DIM, FNET_GROUP_DIM)),
                         _resident((FNET_GROUP_DIM, FNET_GROUP_DIM))],
        out_specs=pl.BlockSpec((1, n2, k1b, cb), lambda k, bi, ci: (bi, 0, k, ci)),
        out_shape=jax.ShapeDtypeStruct((b, n2, n1, c), BF16),
        compiler_params=_params("parallel", "parallel", "parallel"),
        name="fnet_mid",
    )(_const(t["p"]), _const(t["q"]), _const(t["tw_k1"]), y,
      _const(cc * norm), _const(sc * norm))
    return out.reshape(b * l, c)


def _shortconv_kernel(*refs, r, nt):
    u_refs, prev_refs, next_refs = refs[0:3], refs[3:6], refs[6:9]
    w_ref, b_ref, vx_ref, x1_ref = refs[9:]
    i = pl.program_id(1)
    row = lax.broadcasted_iota(jnp.int32, (r, 1), 0)
    outs = []
    for part in range(3):
        cols = slice(part * D_HYENA, (part + 1) * D_HYENA)
        u = u_refs[part][0].astype(F32)
        prev_row = prev_refs[part][0].astype(F32)[_BF16_SUBLANES - 1:_BF16_SUBLANES]
        next_row = next_refs[part][0].astype(F32)[0:1]
        prev_row = jnp.where(i == 0, 0.0, prev_row)
        next_row = jnp.where(i == nt - 1, 0.0, next_row)
        u_prev = jnp.where(row == 0, prev_row, pltpu.roll(u, 1, axis=0))
        u_next = jnp.where(row == r - 1, next_row, pltpu.roll(u, r - 1, axis=0))
        outs.append(u_prev * w_ref[0:1, cols] + u * w_ref[1:2, cols]
                    + u_next * w_ref[2:3, cols] + b_ref[:, cols])
    x1, x2, v = outs
    vx_ref[0] = (v * x2).astype(BF16)
    x1_ref[0] = x1.astype(BF16)


def _shortconv(proj3, conv_w, conv_b):
    b, l, _ = proj3.shape
    c = D_HYENA
    r = 1024
    nt = l // r
    hb = _BF16_SUBLANES
    halo = r // hb
    first_block = D_FNET // c
    tiles = [pl.BlockSpec((1, r, c), lambda bi, i, p=p: (bi, i, first_block + p))
             for p in range(3)]
    prevs = [pl.BlockSpec((1, hb, c),
                          lambda bi, i, p=p: (bi, jnp.maximum(i * halo - 1, 0), first_block + p))
             for p in range(3)]
    nexts = [pl.BlockSpec((1, hb, c),
                          lambda bi, i, p=p: (bi, jnp.minimum((i + 1) * halo, l // hb - 1),
                                              first_block + p))
             for p in range(3)]
    return pl.pallas_call(
        functools.partial(_shortconv_kernel, r=r, nt=nt),
        grid=(b, nt),
        in_specs=tiles + prevs + nexts + [_resident((3, 3 * c)), _resident((1, 3 * c))],
        out_specs=[pl.BlockSpec((1, r, c), lambda bi, i: (bi, i, 0))] * 2,
        out_shape=[jax.ShapeDtypeStruct((b, l, c), BF16)] * 2,
        compiler_params=_params("parallel", "parallel"),
        name="shortconv",
    )(*([proj3] * 9), conv_w, conv_b)


def _filter_rows_kernel(bands_ref, w1t_ref, w1c_ref, w1s_ref, b1_ref, w2_ref, b2_ref,
                        w3_ref, b3_ref, fr_ref, w4_ref, deltas_ref, lhs_ref, y_ref, ss_ref,
                        decay_ref, *, l, n2, sb, chunk):
    g = pl.program_id(0)
    n1 = lhs_ref.shape[1]
    half = n1 // 2
    lanes = sb * half
    c = deltas_ref.shape[1]

    def offsets(shape, axis):
        idx = lax.broadcasted_iota(jnp.int32, shape, axis)
        back = idx >= lanes
        idx = jnp.where(back, idx - lanes, idx)
        s = lax.shift_right_logical(idx, half.bit_length() - 1)
        return back, n2 * (idx & (half - 1)) + s

    back, off = offsets((1, 2 * lanes), 1)
    lag = jnp.where(back, l - off - sb * g, off + sb * g).astype(F32)
    ang = bands_ref[...] * ((2.0 * math.pi / l) * lag)
    fr = fr_ref[...]
    pre = (w1t_ref[...] * (lag * (1.0 / (l - 1))) + _dot_hi(w1c_ref[...], jnp.cos(ang))
           - _dot_hi(w1s_ref[...], jnp.sin(ang)) + b1_ref[...])
    h = jnp.sin(fr * pre)
    h = jnp.sin(fr * (_dot_hi(w2_ref[...], h) + b2_ref[...]))
    h = jnp.sin(fr * (_dot_hi(w3_ref[...], h) + b3_ref[...]))

    @pl.when(g == 0)
    def _():
        ss_ref[...] = jnp.zeros_like(ss_ref)
        _, off_col = offsets((lanes, 1), 0)
        off_col = off_col.astype(F32)
        decay_ref[0] = jnp.exp(-(off_col * (1.0 / (l - 1))) * deltas_ref[...])
        decay_ref[1] = jnp.exp(-((l - off_col) * (1.0 / (l - 1))) * deltas_ref[...])

    shift = (sb * g).astype(F32) * (1.0 / (l - 1))
    row0 = lax.broadcasted_iota(jnp.int32, (lanes, 1), 0) == 0
    lhs = lhs_ref[...].astype(BF16)
    for lo in range(0, c, chunk):
        cols = slice(lo, lo + chunk)
        dl = deltas_ref[:, cols]
        k_fwd = (_dot_tn_bf16x3(h[:, :lanes], w4_ref[:, lo:lo + chunk])
                 * (decay_ref[0, :, cols] * jnp.exp(-shift * dl)))
        k_bwd = (_dot_tn_bf16x3(h[:, lanes:], w4_ref[:, c + lo:c + lo + chunk])
                 * (decay_ref[1, :, cols] * jnp.exp(shift * dl)))
        k_bwd = jnp.where(jnp.logical_and(row0, g == 0), 0.0, k_bwd)
        ss_ref[:, cols] += jnp.sum(k_fwd * k_fwd + k_bwd * k_bwd, axis=0, keepdims=True)
        ys = []
        for s in range(sb):
            rows = slice(s * half, (s + 1) * half)
            xs = jnp.concatenate([k_fwd[rows], k_bwd[rows]], axis=0).astype(BF16)
            ys.append(_dot(lhs, xs).astype(BF16))
        y = _swap_major(jnp.stack(ys, axis=0))
        y_ref[0, :, :, :, cols] = y.reshape(2, n1, sb, chunk)


def _filter_rows(l, w1, b1, w2, b2, w3, b3, w4, freq):
    n = 2 * l
    n2 = _DFT_N2
    n1 = n // n2
    c = D_HYENA
    sb = _ROW_GROUP
    t = _dft_tables(n1, n2)
    bands = np.linspace(1e-4, FILTER_BANDS - 1, FILTER_BANDS)[:, None]
    deltas = np.abs(np.linspace(math.log(DECAY_TARGET) / FAST_DECAY_PCT,
                                math.log(DECAY_TARGET) / SLOW_DECAY_PCT, D_HYENA))[None, :]
    ins = [_const(bands), w1[0:1].T, w1[1:1 + FILTER_BANDS].T, w1[1 + FILTER_BANDS:].T,
           b1[:, None], w2.T, b2[:, None], w3.T, b3[:, None], freq[:, None], w4,
           _const(deltas), _const(t["rows_fwd"])]
    y, ss = pl.pallas_call(
        functools.partial(_filter_rows_kernel, l=l, n2=n2, sb=sb, chunk=512),
        grid=(n2 // sb,),
        in_specs=[_resident(a.shape) for a in ins],
        out_specs=[pl.BlockSpec((1, 2, n1, sb, c), lambda g: (0, 0, 0, g, 0)),
                   pl.BlockSpec((1, c), lambda g: (0, 0))],
        out_shape=[jax.ShapeDtypeStruct((1, 2, n1, n2, c), BF16),
                   jax.ShapeDtypeStruct((1, c), F32)],
        scratch_shapes=[pltpu.VMEM((2, sb * (n1 // 2), c), F32)],
        compiler_params=_params("arbitrary"),
        name="filter_rows",
    )(*ins)
    return y, ss


def _conv_out_kernel(pc_ref, qc_ref, tw_ref, z_ref, vx_ref, x1_ref, skip_ref, o_ref):
    _, _, n1, sb, cb = z_ref.shape
    z = _swap_major(z_ref[0].reshape(2 * n1, sb, cb))
    ys = []
    for s in range(sb):
        lhs = (pc_ref[...] * tw_ref[s, 0:1, :] + qc_ref[...] * tw_ref[s, 1:2, :]).astype(BF16)
        ys.append(_dot(lhs, z[s]))
    y = _swap_major(jnp.stack(ys, axis=0)).reshape(o_ref.shape)
    y = y + vx_ref[...].astype(F32) * skip_ref[...]
    o_ref[...] = (y * x1_ref[...].astype(F32)).astype(BF16)


def _longconv(vx, x1, filter_rows, filter_ss, skip):
    b, l, c = vx.shape
    assert b == 2, "batch pair is packed as (re, im)"
    n = 2 * l
    n2 = _DFT_N2
    n1 = n // n2
    half = n1 // 2
    t = _dft_tables(n1, n2)
    y = _rows_fwd(_const(t["rows_fwd_c"]), vx.reshape(1, 2, half, n2, c), "conv_rows")
    k1b = 8
    head, y_spec = _stage2_specs(n2, c, k1b)
    z = pl.pallas_call(
        functools.partial(_conv_mid_kernel, k1b=k1b, n2=n2),
        grid=(n1 // k1b, 1),
        in_specs=head + [_resident((2 * n2, 2 * n2)), y_spec, y_spec, _resident((1, c))],
        out_specs=y_spec,
        out_shape=jax.ShapeDtypeStruct((1, 2, n1, n2, c), BF16),
        compiler_params=_params("parallel", "parallel"),
        name="conv_mid",
    )(_const(t["p"]), _const(t["q"]), _const(t["tw_k1"]), _const(t["inv2"] / n), y,
      filter_rows, filter_ss)

    sb, cb = _ROW_GROUP, 1024
    seq_spec = pl.BlockSpec((2, half, sb, cb), lambda g, ci: (0, 0, g, ci))
    out = pl.pallas_call(
        _conv_out_kernel,
        grid=(n2 // sb, c // cb),
        in_specs=[_resident((n1, 2 * n1)), _resident((n1, 2 * n1)),
                  pl.BlockSpec((sb, 2, 2 * n1), lambda g, ci: (g, 0, 0)),
                  pl.BlockSpec((1, 2, n1, sb, cb), lambda g, ci: (0, 0, 0, g, ci)),
                  seq_spec, seq_spec,
                  pl.BlockSpec((1, cb), lambda g, ci: (0, ci))],
        out_specs=seq_spec,
        out_shape=jax.ShapeDtypeStruct((2, half, n2, c), BF16),
        compiler_params=_params("parallel", "parallel"),
        name="conv_out",
    )(_const(t["pc"]), _const(t["qc"]), _const(t["tw_n2"]), z,
      vx.reshape(2, half, n2, c), x1.reshape(2, half, n2, c), skip)
    return out.reshape(b * l, c)


def _merge_kernel(mx_ref, yh_ref, ga_ref, gb_ref, x_ref, wf_ref, wh_ref, wo_ref, g2_ref,
                  x1_ref, h2_ref, m_ref, *, chunk):
    d = x_ref.shape[1]
    for lo in range(0, d, chunk):
        cols = slice(lo, lo + chunk)
        ya = _dot(mx_ref[...], wf_ref[:, cols])
        yb = _dot(yh_ref[...], wh_ref[:, cols])
        m_ref[:, cols] = (ga_ref[:, cols].astype(F32) * ya
                          + gb_ref[:, cols].astype(F32) * yb).astype(BF16)
    ss = jnp.zeros((x_ref.shape[0], 1), F32)
    for lo in range(0, d, chunk):
        cols = slice(lo, lo + chunk)
        x1 = x_ref[:, cols] + _dot(m_ref[...], wo_ref[:, cols])
        x1_ref[:, cols] = x1
        ss = ss + jnp.sum(x1 * x1, axis=-1, keepdims=True)
    scale = lax.rsqrt(ss * (1.0 / d) + EPS)
    for lo in range(0, d, chunk):
        cols = slice(lo, lo + chunk)
        h2_ref[:, cols] = (x1_ref[:, cols] * scale * g2_ref[:, cols]).astype(BF16)


def _merge(mixed, yh, proj, x2, wf, wh, wo, g2):
    rows = x2.shape[0]
    tm = 512
    d = D_MODEL
    gate_block = (proj.shape[1] - 2 * d) // d
    return pl.pallas_call(
        functools.partial(_merge_kernel, chunk=256),
        grid=(rows // tm,),
        in_specs=[
            pl.BlockSpec((tm, D_FNET), lambda i: (i, 0)),
            pl.BlockSpec((tm, D_HYENA), lambda i: (i, 0)),
            pl.BlockSpec((tm, d), lambda i: (i, gate_block)),
            pl.BlockSpec((tm, d), lambda i: (i, gate_block + 1)),
            pl.BlockSpec((tm, d), lambda i: (i, 0)),
            _resident((D_FNET, d)), _resident((D_HYENA, d)), _resident((d, d)),
            _resident((1, d)),
        ],
        out_specs=[pl.BlockSpec((tm, d), lambda i: (i, 0))] * 2,
        out_shape=[jax.ShapeDtypeStruct((rows, d), F32),
                   jax.ShapeDtypeStruct((rows, d), BF16)],
        scratch_shapes=[pltpu.VMEM((tm, d), BF16)],
        compiler_params=_params("parallel"),
        name="merge",
    )(mixed, yh, proj, proj, x2, wf, wh, wo, g2)


def _mlp_kernel(h2_ref, x1_ref, w1_ref, w2_ref, gf_ref, o_ref, *, nj):
    j = pl.program_id(1)
    slice_rows = x1_ref.shape[0]

    def step(first):
        t = _dot(h2_ref[...], w1_ref[...])
        t = jnp.square(jnp.maximum(t, 0.0)).astype(BF16)
        part = _dot(t, w2_ref[...])
        if first:
            o_ref[...] = part
        else:
            o_ref[...] += part
        rows = pl.ds(pl.multiple_of(j * slice_rows, slice_rows), slice_rows)
        o_ref[rows, :] += x1_ref[...]

    @pl.when(j == 0)
    def _():
        step(True)

    @pl.when(j > 0)
    def _():
        step(False)

    @pl.when(j == nj - 1)
    def _():
        x = o_ref[...]
        ms = jnp.mean(x * x, axis=-1, keepdims=True)
        o_ref[...] = x * lax.rsqrt(ms + EPS) * gf_ref[...]


def _mlp(h2, x1, w1, w2, gf):
    rows = x1.shape[0]
    tm, tf = 1024, 1024
    d = D_MODEL
    nj = D_FF // tf
    return pl.pallas_call(
        functools.partial(_mlp_kernel, nj=nj),
        grid=(rows // tm, nj),
        in_specs=[
            pl.BlockSpec((tm, d), lambda i, j: (i, 0)),
            pl.BlockSpec((tm // nj, d), lambda i, j: (i * nj + j, 0)),
            pl.BlockSpec((d, tf), lambda i, j: (0, j)),
            pl.BlockSpec((tf, d), lambda i, j: (j, 0)),
            pl.BlockSpec((1, d), lambda i, j: (0, 0)),
        ],
        out_specs=pl.BlockSpec((tm, d), lambda i, j: (i, 0)),
        out_shape=jax.ShapeDtypeStruct((rows, d), F32),
        compiler_params=_params("parallel", "arbitrary"),
        name="mlp",
    )(h2, x1, w1, w2, gf)


def _layer_and_final_norm(x, w, filt):
    b, l, d = x.shape
    x2 = x.reshape(b * l, d)
    proj = _project(x2, w["norm1_g"], w["w_cat"], w["b_gate"])
    mixed = _fnet(proj, b, l)
    vx, x1h = _shortconv(proj.reshape(b, l, proj.shape[-1]), w["conv_w"], w["conv_b"])
    filter_rows, filter_ss = _filter_rows(l, *filt)
    yh = _longconv(vx, x1h, filter_rows, filter_ss, w["skip"])
    x1, h2 = _merge(mixed, yh, proj, x2, w["w_fnet_map"], w["w_hyena_out"], w["w_out"],
                    w["norm2_g"])
    out = _mlp(h2, x1, w["w_mlp1"], w["w_mlp2"], w["norm_f_g"])
    return out.reshape(b, l, d)


def kernel(x_prompt, x_sample, norm1_g, w_in, conv_w, conv_b, filt_w1, filt_b1, filt_w2, filt_b2, filt_w3, filt_b3, filt_w4, filt_freq, hyena_skip, w_fnet_map, w_hyena_out, w_gate, b_gate, w_out, norm2_g, w_mlp1, w_mlp2, norm_f_g):
    assert norm1_g.shape[0] == 1, "one layer"
    w = {
        "norm1_g": norm1_g[0][None],
        "w_cat": _cast_concat([w_in[0], w_gate[0]]),
        "b_gate": b_gate[0][None],
        "conv_w": conv_w[0],
        "conv_b": conv_b[0][None],
        "skip": hyena_skip[0][None],
        "w_fnet_map": w_fnet_map[0].astype(BF16),
        "w_hyena_out": w_hyena_out[0].astype(BF16),
        "w_out": w_out[0].astype(BF16),
        "norm2_g": norm2_g[0][None],
        "w_mlp1": w_mlp1[0].astype(BF16),
        "w_mlp2": w_mlp2[0].astype(BF16),
        "norm_f_g": norm_f_g[None],
    }
    filt = (filt_w1[0], filt_b1[0], filt_w2[0], filt_b2[0], filt_w3[0], filt_b3[0],
            filt_w4[0], filt_freq[0])
    return (_layer_and_final_norm(x_prompt, w, filt),
            _layer_and_final_norm(x_sample, w, filt))
```

```python
import functools
import math

import jax
import jax.numpy as jnp
import numpy as np
from jax import lax
from jax.experimental import pallas as pl
from jax.experimental.pallas import tpu as pltpu

D_MODEL = 2048
N_FNET_GROUPS = 4
FNET_GROUP_DIM = 256
D_FNET = N_FNET_GROUPS * FNET_GROUP_DIM
D_HYENA = 1024
FILTER_BANDS = 16
FILTER_HIDDEN = 64
DECAY_TARGET = 1e-2
FAST_DECAY_PCT = 0.3
SLOW_DECAY_PCT = 1.5
D_FF = 4 * D_MODEL
EPS = 1e-6

F32 = jnp.float32
BF16 = jnp.bfloat16

_VMEM_LIMIT_BYTES = 56 * 1024 * 1024
_DFT_N2 = 128
_BF16_SUBLANES = 16
_ROW_GROUP = _BF16_SUBLANES

_dot = functools.partial(jnp.dot, preferred_element_type=F32)
_dot_hi = functools.partial(jnp.dot, preferred_element_type=F32,
                            precision=lax.Precision.HIGHEST)


def _dot_tn_bf16x3(a, b):
    dims = (((0,), (0,)), ((), ()))
    a_hi, b_hi = a.astype(BF16), b.astype(BF16)
    a_lo = (a - a_hi.astype(F32)).astype(BF16)
    b_lo = (b - b_hi.astype(F32)).astype(BF16)
    dg = functools.partial(lax.dot_general, dimension_numbers=dims, preferred_element_type=F32)
    return dg(a_hi, b_hi) + dg(a_lo, b_hi) + dg(a_hi, b_lo)


def _params(*sem):
    return pltpu.CompilerParams(dimension_semantics=sem,
                                vmem_limit_bytes=_VMEM_LIMIT_BYTES)


def _resident(shape):
    zeros = (0,) * len(shape)
    return pl.BlockSpec(shape, lambda *_: zeros, pipeline_mode=pl.Buffered(1))


def _cos_sin(n):
    jk = (np.arange(n)[:, None] * np.arange(n)[None, :]) % n
    ang = 2.0 * np.pi * jk / n
    return np.cos(ang), np.sin(ang)


@functools.lru_cache(maxsize=None)
def _dft_tables(n1, n2):
    n = n1 * n2
    c1, s1 = _cos_sin(n1)
    c2, s2 = _cos_sin(n2)
    fr1, fi1 = c1, -s1
    fr2, fi2 = c2, -s2
    ang = 2.0 * np.pi * ((np.arange(n1)[:, None] * np.arange(n2)[None, :]) % n) / n
    twr, twi = np.cos(ang), -np.sin(ang)
    t = {}
    h = n1 // 2
    t["rows_fwd"] = np.concatenate([fr1, fi1], axis=0)
    t["rows_fwd_c"] = np.block([[fr1[:, :h], -fi1[:, :h]], [fi1[:, :h], fr1[:, :h]]])
    t["p"] = np.block([[fr2, -fi2], [fi2, fr2]])
    t["q"] = np.block([[-fi2, -fr2], [fr2, -fi2]])
    t["tw_k1"] = np.stack([np.concatenate([twr, twr], axis=1),
                           np.concatenate([twi, twi], axis=1)], axis=1)
    t["inv2"] = np.block([[fr2, fi2], [-fi2, fr2]])
    t["pc"] = np.block([[fr1[:h], fi1[:h]], [-fi1[:h], fr1[:h]]])
    t["qc"] = np.block([[-fi1[:h], fr1[:h]], [-fr1[:h], -fi1[:h]]])
    t["tw_n2"] = np.stack([np.concatenate([twr.T, twr.T], axis=1),
                           np.concatenate([twi.T, twi.T], axis=1)], axis=1)
    return t


def _const(a, dtype=F32):
    return jnp.asarray(np.asarray(a, np.float32), dtype)


def _cast_concat_kernel(*refs):
    o_ref = refs[-1]
    lo = 0
    for w_ref in refs[:-1]:
        o_ref[:, lo:lo + w_ref.shape[1]] = w_ref[...].astype(BF16)
        lo += w_ref.shape[1]


def _cast_concat(ws):
    k = ws[0].shape[0]
    n = sum(w.shape[1] for w in ws)
    rb = 128
    return pl.pallas_call(
        _cast_concat_kernel,
        grid=(k // rb,),
        in_specs=[pl.BlockSpec((rb, w.shape[1]), lambda i: (i, 0)) for w in ws],
        out_specs=pl.BlockSpec((rb, n), lambda i: (i, 0)),
        out_shape=jax.ShapeDtypeStruct((k, n), BF16),
        compiler_params=_params("parallel"),
        name="cast_concat",
    )(*ws)


def _project_kernel(x_ref, g_ref, w_ref, b_ref, o_ref, h_ref, *, gate_tile, chunk):
    j = pl.program_id(1)
    tm = x_ref.shape[0]

    def emit(gated, rows):
        for lo in range(0, o_ref.shape[1], chunk):
            acc = _dot(h_ref[rows, :], w_ref[:, lo:lo + chunk])
            if gated:
                acc = jax.nn.sigmoid(acc + b_ref[:, lo:lo + chunk])
            o_ref[rows, lo:lo + chunk] = acc.astype(BF16)

    @pl.when(j == 0)
    def _():
        for r0 in range(0, tm, tm // 2):
            rows = slice(r0, r0 + tm // 2)
            x = x_ref[rows, :]
            ms = jnp.mean(x * x, axis=-1, keepdims=True)
            h_ref[rows, :] = (x * lax.rsqrt(ms + EPS) * g_ref[...]).astype(BF16)
            emit(False, rows)

    @pl.when(jnp.logical_and(j > 0, j < gate_tile))
    def _():
        emit(False, slice(None))

    @pl.when(j >= gate_tile)
    def _():
        emit(True, slice(None))


def _project(x2, g, w_cat, b_gate):
    rows = x2.shape[0]
    n_out = w_cat.shape[1]
    tm, tn = 1024, 2048
    gate_tile = (n_out - b_gate.shape[1]) // tn
    assert gate_tile >= 1, "the first column tile of a row tile is not gated"
    return pl.pallas_call(
        functools.partial(_project_kernel, gate_tile=gate_tile, chunk=tn),
        grid=(rows // tm, n_out // tn),
        in_specs=[
            pl.BlockSpec((tm, D_MODEL), lambda i, j: (i, 0)),
            pl.BlockSpec((1, D_MODEL), lambda i, j: (0, 0)),
            pl.BlockSpec((D_MODEL, tn), lambda i, j: (0, j)),
            pl.BlockSpec((1, tn), lambda i, j: (0, jnp.maximum(j - gate_tile, 0))),
        ],
        out_specs=pl.BlockSpec((tm, tn), lambda i, j: (i, j)),
        out_shape=jax.ShapeDtypeStruct((rows, n_out), BF16),
        scratch_shapes=[pltpu.VMEM((tm, D_MODEL), BF16)],
        compiler_params=_params("parallel", "arbitrary"),
        name="project",
    )(x2, g, w_cat, b_gate)


def _swap_major(x):
    return jnp.transpose(x, (1, 0, 2))


def _rows_fwd_kernel(lhs_ref, x_ref, o_ref):
    _, parts, k, sb, cb = x_ref.shape
    x = x_ref[0].astype(BF16).reshape(parts * k, sb, cb)
    xt = _swap_major(x)
    lhs = lhs_ref[...].astype(BF16)
    y = jnp.stack([_dot(lhs, xt[s]).astype(BF16) for s in range(sb)], axis=0)
    o_ref[0] = _swap_major(y).reshape(o_ref.shape[1:])


def _rows_fwd(lhs, x5, name, c=None):
    b, parts, k, n2, _ = x5.shape
    c = c or x5.shape[-1]
    n1 = lhs.shape[0] // 2
    sb, cb = _ROW_GROUP, 1024
    return pl.pallas_call(
        _rows_fwd_kernel,
        grid=(b, n2 // sb, c // cb),
        in_specs=[_resident(lhs.shape),
                  pl.BlockSpec((1, parts, k, sb, cb), lambda bi, g, ci: (bi, 0, 0, g, ci))],
        out_specs=pl.BlockSpec((1, 2, n1, sb, cb), lambda bi, g, ci: (bi, 0, 0, g, ci)),
        out_shape=jax.ShapeDtypeStruct((b, 2, n1, n2, c), BF16),
        compiler_params=_params("parallel", "parallel", "parallel"),
        name=name,
    )(lhs, x5)


def _stage2_fwd(p_ref, q_ref, tw_ref, y_ref, i):
    g = (p_ref[...] * tw_ref[i, 0:1, :] + q_ref[...] * tw_ref[i, 1:2, :]).astype(BF16)
    ys = jnp.concatenate([y_ref[0, 0, i], y_ref[0, 1, i]], axis=0)
    return _dot(g, ys)


def _conv_mid_kernel(p_ref, q_ref, tw_ref, inv_ref, y_ref, yk_ref, ss_ref, o_ref, *, k1b, n2):
    c = y_ref.shape[-1]
    scale = lax.rsqrt(ss_ref[...] + EPS)
    inv = inv_ref[...].astype(BF16)
    for i in range(k1b):
        g = (p_ref[...] * tw_ref[i, 0:1, :] + q_ref[...] * tw_ref[i, 1:2, :]).astype(BF16)
        stacked = jnp.concatenate(
            [jnp.concatenate([y_ref[0, 0, i], yk_ref[0, 0, i]], axis=1),
             jnp.concatenate([y_ref[0, 1, i], yk_ref[0, 1, i]], axis=1)], axis=0)
        s = _dot(g, stacked)
        xr, xi = s[:n2, :c], s[n2:, :c]
        kr, ki = s[:n2, c:] * scale, s[n2:, c:] * scale
        prod = jnp.concatenate([xr * kr - xi * ki, xr * ki + xi * kr], axis=0)
        z = _dot(inv, prod.astype(BF16))
        o_ref[0, 0, i] = z[:n2].astype(BF16)
        o_ref[0, 1, i] = z[n2:].astype(BF16)


def _fnet_mid_kernel(p_ref, q_ref, tw_ref, y_ref, cc_ref, sc_ref, o_ref, *, k1b, n2):
    cb = o_ref.shape[-1]
    cc = cc_ref[...].astype(BF16)
    sc = sc_ref[...].astype(BF16)
    urs, uis = [], []
    for i in range(k1b):
        s = _stage2_fwd(p_ref, q_ref, tw_ref, y_ref, i)
        urs.append(s[:n2].astype(BF16))
        uis.append(s[n2:].astype(BF16))
    ur = jnp.concatenate(urs, axis=0)
    ui = jnp.concatenate(uis, axis=0)
    groups = []
    for lo in range(0, cb, FNET_GROUP_DIM):
        hi = lo + FNET_GROUP_DIM
        groups.append((_dot(ur[:, lo:hi], cc) + _dot(ui[:, lo:hi], sc)).astype(BF16))
    mixed = jnp.concatenate(groups, axis=1).reshape(k1b, n2, cb)
    o_ref[0] = _swap_major(mixed)


def _stage2_specs(n2, c, k1b):
    return [
        _resident((2 * n2, 2 * n2)),
        _resident((2 * n2, 2 * n2)),
        pl.BlockSpec((k1b, 2, 2 * n2), lambda k, b, *_: (k, 0, 0)),
    ], pl.BlockSpec((1, 2, k1b, n2, c), lambda k, b, *_: (b, 0, k, 0, 0))


def _fnet(proj, b, l):
    n2 = _DFT_N2
    n1 = l // n2
    c = D_FNET
    t = _dft_tables(n1, n2)
    lhs = _const(t["rows_fwd"] / math.sqrt(l))
    y = _rows_fwd(lhs, proj.reshape(b, 1, n1, n2, proj.shape[-1]), "fnet_rows", c)
    cc, sc = _cos_sin(FNET_GROUP_DIM)
    norm = 1.0 / math.sqrt(FNET_GROUP_DIM)
    k1b, cb = _ROW_GROUP, 4 * FNET_GROUP_DIM
    head, _ = _stage2_specs(n2, c, k1b)
    out = pl.pallas_call(
        functools.partial(_fnet_mid_kernel, k1b=k1b, n2=n2),
        grid=(n1 // k1b, b, c // cb),
        in_specs=head + [pl.BlockSpec((1, 2, k1b, n2, cb), lambda k, bi, ci: (bi, 0, k, 0, ci)),
                         _resident((FNET_GROUP_DIM, FNET_GROUP_DIM)),
                         _resident((FNET_GROUP_DIM, FNET_GROUP_DIM))],
        out_specs=pl.BlockSpec((1, n2, k1b, cb), lambda k, bi, ci: (bi, 0, k, ci)),
        out_shape=jax.ShapeDtypeStruct((b, n2, n1, c), BF16),
        compiler_params=_params("parallel", "parallel", "parallel"),
        name="fnet_mid",
    )(_const(t["p"]), _const(t["q"]), _const(t["tw_k1"]), y,
      _const(cc * norm), _const(sc * norm))
    return out.reshape(b * l, c)


def _shortconv_kernel(*refs, r, nt):
    u_refs, prev_refs, next_refs = refs[0:3], refs[3:6], refs[6:9]
    w_ref, b_ref, vx_ref, x1_ref = refs[9:]
    i = pl.program_id(1)
    row = lax.broadcasted_iota(jnp.int32, (r, 1), 0)
    outs = []
    for part in range(3):
        cols = slice(part * D_HYENA, (part + 1) * D_HYENA)
        u = u_refs[part][0].astype(F32)
        prev_row = prev_refs[part][0].astype(F32)[_BF16_SUBLANES - 1:_BF16_SUBLANES]
        next_row = next_refs[part][0].astype(F32)[0:1]
        prev_row = jnp.where(i == 0, 0.0, prev_row)
        next_row = jnp.where(i == nt - 1, 0.0, next_row)
        u_prev = jnp.where(row == 0, prev_row, pltpu.roll(u, 1, axis=0))
        u_next = jnp.where(row == r - 1, next_row, pltpu.roll(u, r - 1, axis=0))
        outs.append(u_prev * w_ref[0:1, cols] + u * w_ref[1:2, cols]
                    + u_next * w_ref[2:3, cols] + b_ref[:, cols])
    x1, x2, v = outs
    vx_ref[0] = (v * x2).astype(BF16)
    x1_ref[0] = x1.astype(BF16)


def _shortconv(proj3, conv_w, conv_b):
    b, l, _ = proj3.shape
    c = D_HYENA
    r = 1024
    nt = l // r
    hb = _BF16_SUBLANES
    halo = r // hb
    first_block = D_FNET // c
    tiles = [pl.BlockSpec((1, r, c), lambda bi, i, p=p: (bi, i, first_block + p))
             for p in range(3)]
    prevs = [pl.BlockSpec((1, hb, c),
                          lambda bi, i, p=p: (bi, jnp.maximum(i * halo - 1, 0), first_block + p))
             for p in range(3)]
    nexts = [pl.BlockSpec((1, hb, c),
                          lambda bi, i, p=p: (bi, jnp.minimum((i + 1) * halo, l // hb - 1),
                                              first_block + p))
             for p in range(3)]
    return pl.pallas_call(
        functools.partial(_shortconv_kernel, r=r, nt=nt),
        grid=(b, nt),
        in_specs=tiles + prevs + nexts + [_resident((3, 3 * c)), _resident((1, 3 * c))],
        out_specs=[pl.BlockSpec((1, r, c), lambda bi, i: (bi, i, 0))] * 2,
        out_shape=[jax.ShapeDtypeStruct((b, l, c), BF16)] * 2,
        compiler_params=_params("parallel", "parallel"),
        name="shortconv",
    )(*([proj3] * 9), conv_w, conv_b)


def _filter_rows_kernel(bands_ref, w1t_ref, w1c_ref, w1s_ref, b1_ref, w2_ref, b2_ref,
                        w3_ref, b3_ref, fr_ref, w4_ref, deltas_ref, lhs_ref, y_ref, ss_ref,
                        decay_ref, *, l, n2, sb, chunk):
    g = pl.program_id(0)
    n1 = lhs_ref.shape[1]
    half = n1 // 2
    lanes = sb * half
    c = deltas_ref.shape[1]

    def offsets(shape, axis):
        idx = lax.broadcasted_iota(jnp.int32, shape, axis)
        back = idx >= lanes
        idx = jnp.where(back, idx - lanes, idx)
        s = lax.shift_right_logical(idx, half.bit_length() - 1)
        return back, n2 * (idx & (half - 1)) + s

    back, off = offsets((1, 2 * lanes), 1)
    lag = jnp.where(back, l - off - sb * g, off + sb * g).astype(F32)
    ang = bands_ref[...] * ((2.0 * math.pi / l) * lag)
    fr = fr_ref[...]
    pre = (w1t_ref[...] * (lag * (1.0 / (l - 1))) + _dot_hi(w1c_ref[...], jnp.cos(ang))
           - _dot_hi(w1s_ref[...], jnp.sin(ang)) + b1_ref[...])
    h = jnp.sin(fr * pre)
    h = jnp.sin(fr * (_dot_hi(w2_ref[...], h) + b2_ref[...]))
    h = jnp.sin(fr * (_dot_hi(w3_ref[...], h) + b3_ref[...]))

    @pl.when(g == 0)
    def _():
        ss_ref[...] = jnp.zeros_like(ss_ref)
        _, off_col = offsets((lanes, 1), 0)
        off_col = off_col.astype(F32)
        decay_ref[0] = jnp.exp(-(off_col * (1.0 / (l - 1))) * deltas_ref[...])
        decay_ref[1] = jnp.exp(-((l - off_col) * (1.0 / (l - 1))) * deltas_ref[...])

    shift = (sb * g).astype(F32) * (1.0 / (l - 1))
    row0 = lax.broadcasted_iota(jnp.int32, (lanes, 1), 0) == 0
    lhs = lhs_ref[...].astype(BF16)
    for lo in range(0, c, chunk):
        cols = slice(lo, lo + chunk)
        dl = deltas_ref[:, cols]
        k_fwd = (_dot_tn_bf16x3(h[:, :lanes], w4_ref[:, lo:lo + chunk])
                 * (decay_ref[0, :, cols] * jnp.exp(-shift * dl)))
        k_bwd = (_dot_tn_bf16x3(h[:, lanes:], w4_ref[:, c + lo:c + lo + chunk])
                 * (decay_ref[1, :, cols] * jnp.exp(shift * dl)))
        k_bwd = jnp.where(jnp.logical_and(row0, g == 0), 0.0, k_bwd)
        ss_ref[:, cols] += jnp.sum(k_fwd * k_fwd + k_bwd * k_bwd, axis=0, keepdims=True)
        ys = []
        for s in range(sb):
            rows = slice(s * half, (s + 1) * half)
            xs = jnp.concatenate([k_fwd[rows], k_bwd[rows]], axis=0).astype(BF16)
            ys.append(_dot(lhs, xs).astype(BF16))
        y = _swap_major(jnp.stack(ys, axis=0))
        y_ref[0, :, :, :, cols] = y.reshape(2, n1, sb, chunk)


def _filter_rows(l, w1, b1, w2, b2, w3, b3, w4, freq):
    n = 2 * l
    n2 = _DFT_N2
    n1 = n // n2
    c = D_HYENA
    sb = _ROW_GROUP
    t = _dft_tables(n1, n2)
    bands = np.linspace(1e-4, FILTER_BANDS - 1, FILTER_BANDS)[:, None]
    deltas = np.abs(np.linspace(math.log(DECAY_TARGET) / FAST_DECAY_PCT,
                                math.log(DECAY_TARGET) / SLOW_DECAY_PCT, D_HYENA))[None, :]
    ins = [_const(bands), w1[0:1].T, w1[1:1 + FILTER_BANDS].T, w1[1 + FILTER_BANDS:].T,
           b1[:, None], w2.T, b2[:, None], w3.T, b3[:, None], freq[:, None], w4,
           _const(deltas), _const(t["rows_fwd"])]
    y, ss = pl.pallas_call(
        functools.partial(_filter_rows_kernel, l=l, n2=n2, sb=sb, chunk=512),
        grid=(n2 // sb,),
        in_specs=[_resident(a.shape) for a in ins],
        out_specs=[pl.BlockSpec((1, 2, n1, sb, c), lambda g: (0, 0, 0, g, 0)),
                   pl.BlockSpec((1, c), lambda g: (0, 0))],
        out_shape=[jax.ShapeDtypeStruct((1, 2, n1, n2, c), BF16),
                   jax.ShapeDtypeStruct((1, c), F32)],
        scratch_shapes=[pltpu.VMEM((2, sb * (n1 // 2), c), F32)],
        compiler_params=_params("arbitrary"),
        name="filter_rows",
    )(*ins)
    return y, ss


def _conv_out_kernel(pc_ref, qc_ref, tw_ref, z_ref, vx_ref, x1_ref, skip_ref, o_ref):
    _, _, n1, sb, cb = z_ref.shape
    z = _swap_major(z_ref[0].reshape(2 * n1, sb, cb))
    ys = []
    for s in range(sb):
        lhs = (pc_ref[...] * tw_ref[s, 0:1, :] + qc_ref[...] * tw_ref[s, 1:2, :]).astype(BF16)
        ys.append(_dot(lhs, z[s]))
    y = _swap_major(jnp.stack(ys, axis=0)).reshape(o_ref.shape)
    y = y + vx_ref[...].astype(F32) * skip_ref[...]
    o_ref[...] = (y * x1_ref[...].astype(F32)).astype(BF16)


def _longconv(vx, x1, filter_rows, filter_ss, skip):
    b, l, c = vx.shape
    assert b == 2, "batch pair is packed as (re, im)"
    n = 2 * l
    n2 = _DFT_N2
    n1 = n // n2
    half = n1 // 2
    t = _dft_tables(n1, n2)
    y = _rows_fwd(_const(t["rows_fwd_c"]), vx.reshape(1, 2, half, n2, c), "conv_rows")
    k1b = 8
    head, y_spec = _stage2_specs(n2, c, k1b)
    z = pl.pallas_call(
        functools.partial(_conv_mid_kernel, k1b=k1b, n2=n2),
        grid=(n1 // k1b, 1),
        in_specs=head + [_resident((2 * n2, 2 * n2)), y_spec, y_spec, _resident((1, c))],
        out_specs=y_spec,
        out_shape=jax.ShapeDtypeStruct((1, 2, n1, n2, c), BF16),
        compiler_params=_params("parallel", "parallel"),
        name="conv_mid",
    )(_const(t["p"]), _const(t["q"]), _const(t["tw_k1"]), _const(t["inv2"] / n), y,
      filter_rows, filter_ss)

    sb, cb = _ROW_GROUP, 1024
    seq_spec = pl.BlockSpec((2, half, sb, cb), lambda g, ci: (0, 0, g, ci))
    out = pl.pallas_call(
        _conv_out_kernel,
        grid=(n2 // sb, c // cb),
        in_specs=[_resident((n1, 2 * n1)), _resident((n1, 2 * n1)),
                  pl.BlockSpec((sb, 2, 2 * n1), lambda g, ci: (g, 0, 0)),
                  pl.BlockSpec((1, 2, n1, sb, cb), lambda g, ci: (0, 0, 0, g, ci)),
                  seq_spec, seq_spec,
                  pl.BlockSpec((1, cb), lambda g, ci: (0, ci))],
        out_specs=seq_spec,
        out_shape=jax.ShapeDtypeStruct((2, half, n2, c), BF16),
        compiler_params=_params("parallel", "parallel"),
        name="conv_out",
    )(_const(t["pc"]), _const(t["qc"]), _const(t["tw_n2"]), z,
      vx.reshape(2, half, n2, c), x1.reshape(2, half, n2, c), skip)
    return out.reshape(b * l, c)


def _merge_kernel(mx_ref, yh_ref, ga_ref, gb_ref, x_ref, wf_ref, wh_ref, wo_ref, g2_ref,
                  x1_ref, h2_ref, m_ref, *, chunk):
    d = x_ref.shape[1]
    for lo in range(0, d, chunk):
        cols = slice(lo, lo + chunk)
        ya = _dot(mx_ref[...], wf_ref[:, cols])
        yb = _dot(yh_ref[...], wh_ref[:, cols])
        m_ref[:, cols] = (ga_ref[:, cols].astype(F32) * ya
                          + gb_ref[:, cols].astype(F32) * yb).astype(BF16)
    ss = jnp.zeros((x_ref.shape[0], 1), F32)
    for lo in range(0, d, chunk):
        cols = slice(lo, lo + chunk)
        x1 = x_ref[:, cols] + _dot(m_ref[...], wo_ref[:, cols])
        x1_ref[:, cols] = x1
        ss = ss + jnp.sum(x1 * x1, axis=-1, keepdims=True)
    scale = lax.rsqrt(ss * (1.0 / d) + EPS)
    for lo in range(0, d, chunk):
        cols = slice(lo, lo + chunk)
        h2_ref[:, cols] = (x1_ref[:, cols] * scale * g2_ref[:, cols]).astype(BF16)


def _merge(mixed, yh, proj, x2, wf, wh, wo, g2):
    rows = x2.shape[0]
    tm = 512
    d = D_MODEL
    gate_block = (proj.shape[1] - 2 * d) // d
    return pl.pallas_call(
        functools.partial(_merge_kernel, chunk=256),
        grid=(rows // tm,),
        in_specs=[
            pl.BlockSpec((tm, D_FNET), lambda i: (i, 0)),
            pl.BlockSpec((tm, D_HYENA), lambda i: (i, 0)),
            pl.BlockSpec((tm, d), lambda i: (i, gate_block)),
            pl.BlockSpec((tm, d), lambda i: (i, gate_block + 1)),
            pl.BlockSpec((tm, d), lambda i: (i, 0)),
            _resident((D_FNET, d)), _resident((D_HYENA, d)), _resident((d, d)),
            _resident((1, d)),
        ],
        out_specs=[pl.BlockSpec((tm, d), lambda i: (i, 0))] * 2,
        out_shape=[jax.ShapeDtypeStruct((rows, d), F32),
                   jax.ShapeDtypeStruct((rows, d), BF16)],
        scratch_shapes=[pltpu.VMEM((tm, d), BF16)],
        compiler_params=_params("parallel"),
        name="merge",
    )(mixed, yh, proj, proj, x2, wf, wh, wo, g2)


def _mlp_kernel(h2_ref, x1_ref, w1_ref, w2_ref, gf_ref, o_ref, *, nj):
    j = pl.program_id(1)
    slice_rows = x1_ref.shape[0]

    def step(first):
        t = _dot(h2_ref[...], w1_ref[...])
        t = jnp.square(jnp.maximum(t, 0.0)).astype(BF16)
        part = _dot(t, w2_ref[...].astype(BF16))
        if first:
            o_ref[...] = part
        else:
            o_ref[...] += part
        rows = pl.ds(pl.multiple_of(j * slice_rows, slice_rows), slice_rows)
        o_ref[rows, :] += x1_ref[...]

    @pl.when(j == 0)
    def _():
        step(True)

    @pl.when(j > 0)
    def _():
        step(False)

    @pl.when(j == nj - 1)
    def _():
        x = o_ref[...]
        ms = jnp.mean(x * x, axis=-1, keepdims=True)
        o_ref[...] = x * lax.rsqrt(ms + EPS) * gf_ref[...]


def _mlp(h2, x1, w1, w2, gf):
    rows = x1.shape[0]
    tm, tf = 1024, 1024
    d = D_MODEL
    nj = D_FF // tf
    return pl.pallas_call(
        functools.partial(_mlp_kernel, nj=nj),
        grid=(rows // tm, nj),
        in_specs=[
            pl.BlockSpec((tm, d), lambda i, j: (i, 0)),
            pl.BlockSpec((tm // nj, d), lambda i, j: (i * nj + j, 0)),
            pl.BlockSpec((d, tf), lambda i, j: (0, j)),
            pl.BlockSpec((tf, d), lambda i, j: (j, 0)),
            pl.BlockSpec((1, d), lambda i, j: (0, 0)),
        ],
        out_specs=pl.BlockSpec((tm, d), lambda i, j: (i, 0)),
        out_shape=jax.ShapeDtypeStruct((rows, d), F32),
        compiler_params=_params("parallel", "arbitrary"),
        name="mlp",
    )(h2, x1, w1, w2, gf)


def _layer_and_final_norm(x, w, filt):
    b, l, d = x.shape
    x2 = x.reshape(b * l, d)
    proj = _project(x2, w["norm1_g"], w["w_cat"], w["b_gate"])
    mixed = _fnet(proj, b, l)
    vx, x1h = _shortconv(proj.reshape(b, l, proj.shape[-1]), w["conv_w"], w["conv_b"])
    filter_rows, filter_ss = _filter_rows(l, *filt)
    yh = _longconv(vx, x1h, filter_rows, filter_ss, w["skip"])
    x1, h2 = _merge(mixed, yh, proj, x2, w["w_fnet_map"], w["w_hyena_out"], w["w_out"],
                    w["norm2_g"])
    out = _mlp(h2, x1, w["w_mlp1"], w["w_mlp2"], w["norm_f_g"])
    return out.reshape(b, l, d)


def kernel(x_prompt, x_sample, norm1_g, w_in, conv_w, conv_b, filt_w1, filt_b1, filt_w2, filt_b2, filt_w3, filt_b3, filt_w4, filt_freq, hyena_skip, w_fnet_map, w_hyena_out, w_gate, b_gate, w_out, norm2_g, w_mlp1, w_mlp2, norm_f_g):
    assert norm1_g.shape[0] == 1, "one layer"
    w = {
        "norm1_g": norm1_g[0][None],
        "w_cat": _cast_concat([w_in[0], w_gate[0]]),
        "b_gate": b_gate[0][None],
        "conv_w": conv_w[0],
        "conv_b": conv_b[0][None],
        "skip": hyena_skip[0][None],
        "w_fnet_map": w_fnet_map[0].astype(BF16),
        "w_hyena_out": w_hyena_out[0].astype(BF16),
        "w_out": w_out[0].astype(BF16),
        "norm2_g": norm2_g[0][None],
        "w_mlp1": w_mlp1[0].astype(BF16),
        "w_mlp2": w_mlp2[0],
        "norm_f_g": norm_f_g[None],
    }
    filt = (filt_w1[0], filt_b1[0], filt_w2[0], filt_b2[0], filt_w3[0], filt_b3[0],
            filt_w4[0], filt_freq[0])
    return (_layer_and_final_norm(x_prompt, w, filt),
            _layer_and_final_norm(x_sample, w, filt))
```

```python
import functools
import math

import jax
import jax.numpy as jnp
import numpy as np
from jax import lax
from jax.experimental import pallas as pl
from jax.experimental.pallas import tpu as pltpu

D_MODEL = 2048
N_FNET_GROUPS = 4
FNET_GROUP_DIM = 256
D_FNET = N_FNET_GROUPS * FNET_GROUP_DIM
D_HYENA = 1024
FILTER_BANDS = 16
FILTER_HIDDEN = 64
DECAY_TARGET = 1e-2
FAST_DECAY_PCT = 0.3
SLOW_DECAY_PCT = 1.5
D_FF = 4 * D_MODEL
EPS = 1e-6

F32 = jnp.float32
BF16 = jnp.bfloat16

_VMEM_LIMIT_BYTES = 56 * 1024 * 1024
_DFT_N2 = 128
_BF16_SUBLANES = 16
_ROW_GROUP = _BF16_SUBLANES

_dot = functools.partial(jnp.dot, preferred_element_type=F32)
_dot_hi = functools.partial(jnp.dot, preferred_element_type=F32,
                            precision=lax.Precision.HIGHEST)


def _dot_tn_bf16x3(a, b):
    dims = (((0,), (0,)), ((), ()))
    a_hi, b_hi = a.astype(BF16), b.astype(BF16)
    a_lo = (a - a_hi.astype(F32)).astype(BF16)
    b_lo = (b - b_hi.astype(F32)).astype(BF16)
    dg = functools.partial(lax.dot_general, dimension_numbers=dims, preferred_element_type=F32)
    return dg(a_hi, b_hi) + dg(a_lo, b_hi) + dg(a_hi, b_lo)


def _params(*sem):
    return pltpu.CompilerParams(dimension_semantics=sem,
                                vmem_limit_bytes=_VMEM_LIMIT_BYTES)


def _resident(shape):
    zeros = (0,) * len(shape)
    return pl.BlockSpec(shape, lambda *_: zeros, pipeline_mode=pl.Buffered(1))


def _cos_sin(n):
    jk = (np.arange(n)[:, None] * np.arange(n)[None, :]) % n
    ang = 2.0 * np.pi * jk / n
    return np.cos(ang), np.sin(ang)


@functools.lru_cache(maxsize=None)
def _dft_tables(n1, n2):
    n = n1 * n2
    c1, s1 = _cos_sin(n1)
    c2, s2 = _cos_sin(n2)
    fr1, fi1 = c1, -s1
    fr2, fi2 = c2, -s2
    ang = 2.0 * np.pi * ((np.arange(n1)[:, None] * np.arange(n2)[None, :]) % n) / n
    twr, twi = np.cos(ang), -np.sin(ang)
    t = {}
    h = n1 // 2
    t["rows_fwd"] = np.concatenate([fr1, fi1], axis=0)
    t["rows_fwd_c"] = np.block([[fr1[:, :h], -fi1[:, :h]], [fi1[:, :h], fr1[:, :h]]])
    t["p"] = np.block([[fr2, -fi2], [fi2, fr2]])
    t["q"] = np.block([[-fi2, -fr2], [fr2, -fi2]])
    t["tw_k1"] = np.stack([np.concatenate([twr, twr], axis=1),
                           np.concatenate([twi, twi], axis=1)], axis=1)
    t["inv2"] = np.block([[fr2, fi2], [-fi2, fr2]])
    t["pc"] = np.block([[fr1[:h], fi1[:h]], [-fi1[:h], fr1[:h]]])
    t["qc"] = np.block([[-fi1[:h], fr1[:h]], [-fr1[:h], -fi1[:h]]])
    t["tw_n2"] = np.stack([np.concatenate([twr.T, twr.T], axis=1),
                           np.concatenate([twi.T, twi.T], axis=1)], axis=1)
    return t


def _const(a, dtype=F32):
    return jnp.asarray(np.asarray(a, np.float32), dtype)


def _cast_concat_kernel(*refs):
    o_ref = refs[-1]
    lo = 0
    for w_ref in refs[:-1]:
        o_ref[:, lo:lo + w_ref.shape[1]] = w_ref[...].astype(BF16)
        lo += w_ref.shape[1]


def _cast_concat(ws):
    k = ws[0].shape[0]
    n = sum(w.shape[1] for w in ws)
    rb = 128
    return pl.pallas_call(
        _cast_concat_kernel,
        grid=(k // rb,),
        in_specs=[pl.BlockSpec((rb, w.shape[1]), lambda i: (i, 0)) for w in ws],
        out_specs=pl.BlockSpec((rb, n), lambda i: (i, 0)),
        out_shape=jax.ShapeDtypeStruct((k, n), BF16),
        compiler_params=_params("parallel"),
        name="cast_concat",
    )(*ws)


def _project_kernel(x_ref, g_ref, w_ref, b_ref, o_ref, h_ref, *, gate_tile, chunk):
    j = pl.program_id(1)
    tm = x_ref.shape[0]

    def emit(gated, rows):
        for lo in range(0, o_ref.shape[1], chunk):
            acc = _dot(h_ref[rows, :], w_ref[:, lo:lo + chunk])
            if gated:
                acc = jax.nn.sigmoid(acc + b_ref[:, lo:lo + chunk])
            o_ref[rows, lo:lo + chunk] = acc.astype(BF16)

    @pl.when(j == 0)
    def _():
        for r0 in range(0, tm, tm // 2):
            rows = slice(r0, r0 + tm // 2)
            x = x_ref[rows, :]
            ms = jnp.mean(x * x, axis=-1, keepdims=True)
            h_ref[rows, :] = (x * lax.rsqrt(ms + EPS) * g_ref[...]).astype(BF16)
            emit(False, rows)

    @pl.when(jnp.logical_and(j > 0, j < gate_tile))
    def _():
        emit(False, slice(None))

    @pl.when(j >= gate_tile)
    def _():
        emit(True, slice(None))


def _project(x2, g, w_cat, b_gate):
    rows = x2.shape[0]
    n_out = w_cat.shape[1]
    tm, tn = 1024, 2048
    gate_tile = (n_out - b_gate.shape[1]) // tn
    assert gate_tile >= 1, "the first column tile of a row tile is not gated"
    return pl.pallas_call(
        functools.partial(_project_kernel, gate_tile=gate_tile, chunk=tn),
        grid=(rows // tm, n_out // tn),
        in_specs=[
            pl.BlockSpec((tm, D_MODEL), lambda i, j: (i, 0)),
            pl.BlockSpec((1, D_MODEL), lambda i, j: (0, 0)),
            pl.BlockSpec((D_MODEL, tn), lambda i, j: (0, j)),
            pl.BlockSpec((1, tn), lambda i, j: (0, jnp.maximum(j - gate_tile, 0))),
        ],
        out_specs=pl.BlockSpec((tm, tn), lambda i, j: (i, j)),
        out_shape=jax.ShapeDtypeStruct((rows, n_out), BF16),
        scratch_shapes=[pltpu.VMEM((tm, D_MODEL), BF16)],
        compiler_params=_params("parallel", "arbitrary"),
        name="project",
    )(x2, g, w_cat, b_gate)


def _swap_major(x):
    return jnp.transpose(x, (1, 0, 2))


def _rows_fwd_kernel(lhs_ref, x_ref, o_ref):
    _, parts, k, sb, cb = x_ref.shape
    x = x_ref[0].astype(BF16).reshape(parts * k, sb, cb)
    xt = _swap_major(x)
    lhs = lhs_ref[...].astype(BF16)
    y = jnp.stack([_dot(lhs, xt[s]).astype(BF16) for s in range(sb)], axis=0)
    o_ref[0] = _swap_major(y).reshape(o_ref.shape[1:])


def _rows_fwd(lhs, x5, name, c=None):
    b, parts, k, n2, _ = x5.shape
    c = c or x5.shape[-1]
    n1 = lhs.shape[0] // 2
    sb, cb = _ROW_GROUP, 1024
    return pl.pallas_call(
        _rows_fwd_kernel,
        grid=(b, n2 // sb, c // cb),
        in_specs=[_resident(lhs.shape),
                  pl.BlockSpec((1, parts, k, sb, cb), lambda bi, g, ci: (bi, 0, 0, g, ci))],
        out_specs=pl.BlockSpec((1, 2, n1, sb, cb), lambda bi, g, ci: (bi, 0, 0, g, ci)),
        out_shape=jax.ShapeDtypeStruct((b, 2, n1, n2, c), BF16),
        compiler_params=_params("parallel", "parallel", "parallel"),
        name=name,
    )(lhs, x5)


def _stage2_fwd(p_ref, q_ref, tw_ref, y_ref, i):
    g = (p_ref[...] * tw_ref[i, 0:1, :] + q_ref[...] * tw_ref[i, 1:2, :]).astype(BF16)
    ys = jnp.concatenate([y_ref[0, 0, i], y_ref[0, 1, i]], axis=0)
    return _dot(g, ys)


def _conv_mid_kernel(p_ref, q_ref, tw_ref, inv_ref, y_ref, yk_ref, ss_ref, o_ref, *, k1b, n2):
    c = y_ref.shape[-1]
    scale = lax.rsqrt(ss_ref[...] + EPS)
    inv = inv_ref[...].astype(BF16)
    for i in range(k1b):
        g = (p_ref[...] * tw_ref[i, 0:1, :] + q_ref[...] * tw_ref[i, 1:2, :]).astype(BF16)
        stacked = jnp.concatenate(
            [jnp.concatenate([y_ref[0, 0, i], yk_ref[0, 0, i]], axis=1),
             jnp.concatenate([y_ref[0, 1, i], yk_ref[0, 1, i]], axis=1)], axis=0)
        s = _dot(g, stacked)
        xr, xi = s[:n2, :c], s[n2:, :c]
        kr, ki = s[:n2, c:] * scale, s[n2:, c:] * scale
        prod = jnp.concatenate([xr * kr - xi * ki, xr * ki + xi * kr], axis=0)
        z = _dot(inv, prod.astype(BF16))
        o_ref[0, 0, i] = z[:n2].astype(BF16)
        o_ref[0, 1, i] = z[n2:].astype(BF16)


def _fnet_mid_kernel(p_ref, q_ref, tw_ref, y_ref, cc_ref, sc_ref, o_ref, *, k1b, n2):
    cb = o_ref.shape[-1]
    cc = cc_ref[...].astype(BF16)
    sc = sc_ref[...].astype(BF16)
    urs, uis = [], []
    for i in range(k1b):
        s = _stage2_fwd(p_ref, q_ref, tw_ref, y_ref, i)
        urs.append(s[:n2].astype(BF16))
        uis.append(s[n2:].astype(BF16))
    ur = jnp.concatenate(urs, axis=0)
    ui = jnp.concatenate(uis, axis=0)
    groups = []
    for lo in range(0, cb, FNET_GROUP_DIM):
        hi = lo + FNET_GROUP_DIM
        groups.append((_dot(ur[:, lo:hi], cc) + _dot(ui[:, lo:hi], sc)).astype(BF16))
    mixed = jnp.concatenate(groups, axis=1).reshape(k1b, n2, cb)
    o_ref[0] = _swap_major(mixed)


def _stage2_specs(n2, c, k1b):
    return [
        _resident((2 * n2, 2 * n2)),
        _resident((2 * n2, 2 * n2)),
        pl.BlockSpec((k1b, 2, 2 * n2), lambda k, b, *_: (k, 0, 0)),
    ], pl.BlockSpec((1, 2, k1b, n2, c), lambda k, b, *_: (b, 0, k, 0, 0))


def _fnet(proj, b, l):
    n2 = _DFT_N2
    n1 = l // n2
    c = D_FNET
    t = _dft_tables(n1, n2)
    lhs = _const(t["rows_fwd"] / math.sqrt(l))
    y = _rows_fwd(lhs, proj.reshape(b, 1, n1, n2, proj.shape[-1]), "fnet_rows", c)
    cc, sc = _cos_sin(FNET_GROUP_DIM)
    norm = 1.0 / math.sqrt(FNET_GROUP_DIM)
    k1b, cb = _ROW_GROUP, 4 * FNET_GROUP_DIM
    head, _ = _stage2_specs(n2, c, k1b)
    out = pl.pallas_call(
        functools.partial(_fnet_mid_kernel, k1b=k1b, n2=n2),
        grid=(n1 // k1b, b, c // cb),
        in_specs=head + [pl.BlockSpec((1, 2, k1b, n2, cb), lambda k, bi, ci: (bi, 0, k, 0, ci)),
                         _resident((FNET_GROUP_DIM, FNET_GROUP_DIM)),
                         _resident((FNET_GROUP_DIM, FNET_GROUP_DIM))],
        out_specs=pl.BlockSpec((1, n2, k1b, cb), lambda k, bi, ci: (bi, 0, k, ci)),
        out_shape=jax.ShapeDtypeStruct((b, n2, n1, c), BF16),
        compiler_params=_params("parallel", "parallel", "parallel"),
        name="fnet_mid",
    )(_const(t["p"]), _const(t["q"]), _const(t["tw_k1"]), y,
      _const(cc * norm), _const(sc * norm))
    return out.reshape(b * l, c)


def _shortconv_kernel(*refs, r, nt):
    u_refs, prev_refs, next_refs = refs[0:3], refs[3:6], refs[6:9]
    w_ref, b_ref, vx_ref, x1_ref = refs[9:]
    i = pl.program_id(1)
    row = lax.broadcasted_iota(jnp.int32, (r, 1), 0)
    outs = []
    for part in range(3):
        cols = slice(part * D_HYENA, (part + 1) * D_HYENA)
        u = u_refs[part][0].astype(F32)
        prev_row = prev_refs[part][0].astype(F32)[_BF16_SUBLANES - 1:_BF16_SUBLANES]
        next_row = next_refs[part][0].astype(F32)[0:1]
        prev_row = jnp.where(i == 0, 0.0, prev_row)
        next_row = jnp.where(i == nt - 1, 0.0, next_row)
        u_prev = jnp.where(row == 0, prev_row, pltpu.roll(u, 1, axis=0))
        u_next = jnp.where(row == r - 1, next_row, pltpu.roll(u, r - 1, axis=0))
        outs.append(u_prev * w_ref[0:1, cols] + u * w_ref[1:2, cols]
                    + u_next * w_ref[2:3, cols] + b_ref[:, cols])
    x1, x2, v = outs
    vx_ref[0] = (v * x2).astype(BF16)
    x1_ref[0] = x1.astype(BF16)


def _shortconv(proj3, conv_w, conv_b):
    b, l, _ = proj3.shape
    c = D_HYENA
    r = 1024
    nt = l // r
    hb = _BF16_SUBLANES
    halo = r // hb
    first_block = D_FNET // c
    tiles = [pl.BlockSpec((1, r, c), lambda bi, i, p=p: (bi, i, first_block + p))
             for p in range(3)]
    prevs = [pl.BlockSpec((1, hb, c),
                          lambda bi, i, p=p: (bi, jnp.maximum(i * halo - 1, 0), first_block + p))
             for p in range(3)]
    nexts = [pl.BlockSpec((1, hb, c),
                          lambda bi, i, p=p: (bi, jnp.minimum((i + 1) * halo, l // hb - 1),
                                              first_block + p))
             for p in range(3)]
    return pl.pallas_call(
        functools.partial(_shortconv_kernel, r=r, nt=nt),
        grid=(b, nt),
        in_specs=tiles + prevs + nexts + [_resident((3, 3 * c)), _resident((1, 3 * c))],
        out_specs=[pl.BlockSpec((1, r, c), lambda bi, i: (bi, i, 0))] * 2,
        out_shape=[jax.ShapeDtypeStruct((b, l, c), BF16)] * 2,
        compiler_params=_params("parallel", "parallel"),
        name="shortconv",
    )(*([proj3] * 9), conv_w, conv_b)


def _filter_rows_kernel(bands_ref, w1t_ref, w1c_ref, w1s_ref, b1_ref, w2_ref, b2_ref,
                        w3_ref, b3_ref, fr_ref, w4_ref, deltas_ref, lhs_ref, y_ref, ss_ref,
                        decay_ref, *, l, n2, sb, chunk):
    g = pl.program_id(0)
    n1 = lhs_ref.shape[1]
    half = n1 // 2
    lanes = sb * half
    c = deltas_ref.shape[1]

    def offsets(shape, axis):
        idx = lax.broadcasted_iota(jnp.int32, shape, axis)
        back = idx >= lanes
        idx = jnp.where(back, idx - lanes, idx)
        s = lax.shift_right_logical(idx, half.bit_length() - 1)
        return back, n2 * (idx & (half - 1)) + s

    back, off = offsets((1, 2 * lanes), 1)
    lag = jnp.where(back, l - off - sb * g, off + sb * g).astype(F32)
    ang = bands_ref[...] * ((2.0 * math.pi / l) * lag)
    fr = fr_ref[...]
    pre = (w1t_ref[...] * (lag * (1.0 / (l - 1))) + _dot_hi(w1c_ref[...], jnp.cos(ang))
           - _dot_hi(w1s_ref[...], jnp.sin(ang)) + b1_ref[...])
    h = jnp.sin(fr * pre)
    h = jnp.sin(fr * (_dot_hi(w2_ref[...], h) + b2_ref[...]))
    h = jnp.sin(fr * (_dot_hi(w3_ref[...], h) + b3_ref[...]))

    @pl.when(g == 0)
    def _():
        ss_ref[...] = jnp.zeros_like(ss_ref)
        _, off_col = offsets((lanes, 1), 0)
        off_col = off_col.astype(F32)
        decay_ref[0] = jnp.exp(-(off_col * (1.0 / (l - 1))) * deltas_ref[...])
        decay_ref[1] = jnp.exp(-((l - off_col) * (1.0 / (l - 1))) * deltas_ref[...])

    shift = (sb * g).astype(F32) * (1.0 / (l - 1))
    row0 = lax.broadcasted_iota(jnp.int32, (lanes, 1), 0) == 0
    lhs = lhs_ref[...].astype(BF16)
    for lo in range(0, c, chunk):
        cols = slice(lo, lo + chunk)
        dl = deltas_ref[:, cols]
        k_fwd = (_dot_tn_bf16x3(h[:, :lanes], w4_ref[:, lo:lo + chunk])
                 * (decay_ref[0, :, cols] * jnp.exp(-shift * dl)))
        k_bwd = (_dot_tn_bf16x3(h[:, lanes:], w4_ref[:, c + lo:c + lo + chunk])
                 * (decay_ref[1, :, cols] * jnp.exp(shift * dl)))
        k_bwd = jnp.where(jnp.logical_and(row0, g == 0), 0.0, k_bwd)
        ss_ref[:, cols] += jnp.sum(k_fwd * k_fwd + k_bwd * k_bwd, axis=0, keepdims=True)
        ys = []
        for s in range(sb):
            rows = slice(s * half, (s + 1) * half)
            xs = jnp.concatenate([k_fwd[rows], k_bwd[rows]], axis=0).astype(BF16)
            ys.append(_dot(lhs, xs).astype(BF16))
        y = _swap_major(jnp.stack(ys, axis=0))
        y_ref[0, :, :, :, cols] = y.reshape(2, n1, sb, chunk)


def _filter_rows(l, w1, b1, w2, b2, w3, b3, w4, freq):
    n = 2 * l
    n2 = _DFT_N2
    n1 = n // n2
    c = D_HYENA
    sb = _ROW_GROUP
    t = _dft_tables(n1, n2)
    bands = np.linspace(1e-4, FILTER_BANDS - 1, FILTER_BANDS)[:, None]
    deltas = np.abs(np.linspace(math.log(DECAY_TARGET) / FAST_DECAY_PCT,
                                math.log(DECAY_TARGET) / SLOW_DECAY_PCT, D_HYENA))[None, :]
    ins = [_const(bands), w1[0:1].T, w1[1:1 + FILTER_BANDS].T, w1[1 + FILTER_BANDS:].T,
           b1[:, None], w2.T, b2[:, None], w3.T, b3[:, None], freq[:, None], w4,
           _const(deltas), _const(t["rows_fwd"])]
    y, ss = pl.pallas_call(
        functools.partial(_filter_rows_kernel, l=l, n2=n2, sb=sb, chunk=512),
        grid=(n2 // sb,),
        in_specs=[_resident(a.shape) for a in ins],
        out_specs=[pl.BlockSpec((1, 2, n1, sb, c), lambda g: (0, 0, 0, g, 0)),
                   pl.BlockSpec((1, c), lambda g: (0, 0))],
        out_shape=[jax.ShapeDtypeStruct((1, 2, n1, n2, c), BF16),
                   jax.ShapeDtypeStruct((1, c), F32)],
        scratch_shapes=[pltpu.VMEM((2, sb * (n1 // 2), c), F32)],
        compiler_params=_params("arbitrary"),
        name="filter_rows",
    )(*ins)
    return y, ss


def _conv_out_kernel(pc_ref, qc_ref, tw_ref, z_ref, vx_ref, x1_ref, skip_ref, o_ref):
    _, _, n1, sb, cb = z_ref.shape
    z = _swap_major(z_ref[0].reshape(2 * n1, sb, cb))
    ys = []
    for s in range(sb):
        lhs = (pc_ref[...] * tw_ref[s, 0:1, :] + qc_ref[...] * tw_ref[s, 1:2, :]).astype(BF16)
        ys.append(_dot(lhs, z[s]))
    y = _swap_major(jnp.stack(ys, axis=0)).reshape(o_ref.shape)
    y = y + vx_ref[...].astype(F32) * skip_ref[...]
    o_ref[...] = (y * x1_ref[...].astype(F32)).astype(BF16)


def _longconv(vx, x1, filter_rows, filter_ss, skip):
    b, l, c = vx.shape
    assert b == 2, "batch pair is packed as (re, im)"
    n = 2 * l
    n2 = _DFT_N2
    n1 = n // n2
    half = n1 // 2
    t = _dft_tables(n1, n2)
    y = _rows_fwd(_const(t["rows_fwd_c"]), vx.reshape(1, 2, half, n2, c), "conv_rows")
    k1b = 8
    head, y_spec = _stage2_specs(n2, c, k1b)
    z = pl.pallas_call(
        functools.partial(_conv_mid_kernel, k1b=k1b, n2=n2),
        grid=(n1 // k1b, 1),
        in_specs=head + [_resident((2 * n2, 2 * n2)), y_spec, y_spec, _resident((1, c))],
        out_specs=y_spec,
        out_shape=jax.ShapeDtypeStruct((1, 2, n1, n2, c), BF16),
        compiler_params=_params("parallel", "parallel"),
        name="conv_mid",
    )(_const(t["p"]), _const(t["q"]), _const(t["tw_k1"]), _const(t["inv2"] / n), y,
      filter_rows, filter_ss)

    sb, cb = _ROW_GROUP, 1024
    seq_spec = pl.BlockSpec((2, half, sb, cb), lambda g, ci: (0, 0, g, ci))
    out = pl.pallas_call(
        _conv_out_kernel,
        grid=(n2 // sb, c // cb),
        in_specs=[_resident((n1, 2 * n1)), _resident((n1, 2 * n1)),
                  pl.BlockSpec((sb, 2, 2 * n1), lambda g, ci: (g, 0, 0)),
                  pl.BlockSpec((1, 2, n1, sb, cb), lambda g, ci: (0, 0, 0, g, ci)),
                  seq_spec, seq_spec,
                  pl.BlockSpec((1, cb), lambda g, ci: (0, ci))],
        out_specs=seq_spec,
        out_shape=jax.ShapeDtypeStruct((2, half, n2, c), BF16),
        compiler_params=_params("parallel", "parallel"),
        name="conv_out",
    )(_const(t["pc"]), _const(t["qc"]), _const(t["tw_n2"]), z,
      vx.reshape(2, half, n2, c), x1.reshape(2, half, n2, c), skip)
    return out.reshape(b * l, c)


def _merge_kernel(mx_ref, yh_ref, ga_ref, gb_ref, x_ref, wf_hbm, wh_hbm, wo_hbm, g2_ref,
                  x1_ref, h2_ref, m_ref, wf_ref, wh_ref, wo_ref, stage_ref, sem, *, chunk):
    d = x_ref.shape[1]

    @pl.when(pl.program_id(0) == 0)
    def _():
        rb = stage_ref.shape[1]
        pieces = [(src, dst, r0) for src, dst in ((wf_hbm, wf_ref), (wh_hbm, wh_ref),
                                                  (wo_hbm, wo_ref))
                  for r0 in range(0, src.shape[0], rb)]

        def copy(k):
            src, _, r0 = pieces[k]
            return pltpu.make_async_copy(src.at[pl.ds(r0, rb), :], stage_ref.at[k % 2],
                                         sem.at[k % 2])

        copy(0).start()
        for k, (_, dst, r0) in enumerate(pieces):
            if k + 1 < len(pieces):
                copy(k + 1).start()
            copy(k).wait()
            dst[r0:r0 + rb, :] = stage_ref[k % 2].astype(BF16)

    for lo in range(0, d, chunk):
        cols = slice(lo, lo + chunk)
        ya = _dot(mx_ref[...], wf_ref[:, cols])
        yb = _dot(yh_ref[...], wh_ref[:, cols])
        m_ref[:, cols] = (ga_ref[:, cols].astype(F32) * ya
                          + gb_ref[:, cols].astype(F32) * yb).astype(BF16)
    ss = jnp.zeros((x_ref.shape[0], 1), F32)
    for lo in range(0, d, chunk):
        cols = slice(lo, lo + chunk)
        x1 = x_ref[:, cols] + _dot(m_ref[...], wo_ref[:, cols])
        x1_ref[:, cols] = x1
        ss = ss + jnp.sum(x1 * x1, axis=-1, keepdims=True)
    scale = lax.rsqrt(ss * (1.0 / d) + EPS)
    for lo in range(0, d, chunk):
        cols = slice(lo, lo + chunk)
        h2_ref[:, cols] = (x1_ref[:, cols] * scale * g2_ref[:, cols]).astype(BF16)


def _merge(mixed, yh, proj, x2, wf, wh, wo, g2):
    rows = x2.shape[0]
    tm = 512
    d = D_MODEL
    gate_block = (proj.shape[1] - 2 * d) // d
    return pl.pallas_call(
        functools.partial(_merge_kernel, chunk=256),
        grid=(rows // tm,),
        in_specs=[
            pl.BlockSpec((tm, D_FNET), lambda i: (i, 0)),
            pl.BlockSpec((tm, D_HYENA), lambda i: (i, 0)),
            pl.BlockSpec((tm, d), lambda i: (i, gate_block)),
            pl.BlockSpec((tm, d), lambda i: (i, gate_block + 1)),
            pl.BlockSpec((tm, d), lambda i: (i, 0)),
            pl.BlockSpec(memory_space=pl.ANY), pl.BlockSpec(memory_space=pl.ANY),
            pl.BlockSpec(memory_space=pl.ANY),
            _resident((1, d)),
        ],
        out_specs=[pl.BlockSpec((tm, d), lambda i: (i, 0))] * 2,
        out_shape=[jax.ShapeDtypeStruct((rows, d), F32),
                   jax.ShapeDtypeStruct((rows, d), BF16)],
        scratch_shapes=[pltpu.VMEM((tm, d), BF16),
                        pltpu.VMEM((D_FNET, d), BF16), pltpu.VMEM((D_HYENA, d), BF16),
                        pltpu.VMEM((d, d), BF16),
                        pltpu.VMEM((2, 128, d), F32), pltpu.SemaphoreType.DMA((2,))],
        compiler_params=_params("arbitrary"),
        name="merge",
    )(mixed, yh, proj, proj, x2, wf, wh, wo, g2)


def _mlp_kernel(h2_ref, x1_ref, w1_ref, w2_ref, gf_ref, o_ref, *, nj):
    j = pl.program_id(1)
    slice_rows = x1_ref.shape[0]

    def step(first):
        t = _dot(h2_ref[...], w1_ref[...])
        t = jnp.square(jnp.maximum(t, 0.0)).astype(BF16)
        part = _dot(t, w2_ref[...].astype(BF16))
        if first:
            o_ref[...] = part
        else:
            o_ref[...] += part
        rows = pl.ds(pl.multiple_of(j * slice_rows, slice_rows), slice_rows)
        o_ref[rows, :] += x1_ref[...]

    @pl.when(j == 0)
    def _():
        step(True)

    @pl.when(j > 0)
    def _():
        step(False)

    @pl.when(j == nj - 1)
    def _():
        x = o_ref[...]
        ms = jnp.mean(x * x, axis=-1, keepdims=True)
        o_ref[...] = x * lax.rsqrt(ms + EPS) * gf_ref[...]


def _mlp(h2, x1, w1, w2, gf):
    rows = x1.shape[0]
    tm, tf = 1024, 1024
    d = D_MODEL
    nj = D_FF // tf
    return pl.pallas_call(
        functools.partial(_mlp_kernel, nj=nj),
        grid=(rows // tm, nj),
        in_specs=[
            pl.BlockSpec((tm, d), lambda i, j: (i, 0)),
            pl.BlockSpec((tm // nj, d), lambda i, j: (i * nj + j, 0)),
            pl.BlockSpec((d, tf), lambda i, j: (0, j)),
            pl.BlockSpec((tf, d), lambda i, j: (j, 0)),
            pl.BlockSpec((1, d), lambda i, j: (0, 0)),
        ],
        out_specs=pl.BlockSpec((tm, d), lambda i, j: (i, 0)),
        out_shape=jax.ShapeDtypeStruct((rows, d), F32),
        compiler_params=_params("parallel", "arbitrary"),
        name="mlp",
    )(h2, x1, w1, w2, gf)


def _layer_and_final_norm(x, w, filt):
    b, l, d = x.shape
    x2 = x.reshape(b * l, d)
    proj = _project(x2, w["norm1_g"], w["w_cat"], w["b_gate"])
    mixed = _fnet(proj, b, l)
    vx, x1h = _shortconv(proj.reshape(b, l, proj.shape[-1]), w["conv_w"], w["conv_b"])
    filter_rows, filter_ss = _filter_rows(l, *filt)
    yh = _longconv(vx, x1h, filter_rows, filter_ss, w["skip"])
    x1, h2 = _merge(mixed, yh, proj, x2, w["w_fnet_map"], w["w_hyena_out"], w["w_out"],
                    w["norm2_g"])
    out = _mlp(h2, x1, w["w_mlp1"], w["w_mlp2"], w["norm_f_g"])
    return out.reshape(b, l, d)


def kernel(x_prompt, x_sample, norm1_g, w_in, conv_w, conv_b, filt_w1, filt_b1, filt_w2, filt_b2, filt_w3, filt_b3, filt_w4, filt_freq, hyena_skip, w_fnet_map, w_hyena_out, w_gate, b_gate, w_out, norm2_g, w_mlp1, w_mlp2, norm_f_g):
    assert norm1_g.shape[0] == 1, "one layer"
    w = {
        "norm1_g": norm1_g[0][None],
        "w_cat": _cast_concat([w_in[0], w_gate[0]]),
        "b_gate": b_gate[0][None],
        "conv_w": conv_w[0],
        "conv_b": conv_b[0][None],
        "skip": hyena_skip[0][None],
        "w_fnet_map": w_fnet_map[0],
        "w_hyena_out": w_hyena_out[0],
        "w_out": w_out[0],
        "norm2_g": norm2_g[0][None],
        "w_mlp1": w_mlp1[0].astype(BF16),
        "w_mlp2": w_mlp2[0],
        "norm_f_g": norm_f_g[None],
    }
    filt = (filt_w1[0], filt_b1[0], filt_w2[0], filt_b2[0], filt_w3[0], filt_b3[0],
            filt_w4[0], filt_freq[0])
    return (_layer_and_final_norm(x_prompt, w, filt),
            _layer_and_final_norm(x_sample, w, filt))
```

```python
import functools
import math

import jax
import jax.numpy as jnp
import numpy as np
from jax import lax
from jax.experimental import pallas as pl
from jax.experimental.pallas import tpu as pltpu

D_MODEL = 2048
N_FNET_GROUPS = 4
FNET_GROUP_DIM = 256
D_FNET = N_FNET_GROUPS * FNET_GROUP_DIM
D_HYENA = 1024
FILTER_BANDS = 16
FILTER_HIDDEN = 64
DECAY_TARGET = 1e-2
FAST_DECAY_PCT = 0.3
SLOW_DECAY_PCT = 1.5
D_FF = 4 * D_MODEL
EPS = 1e-6

F32 = jnp.float32
BF16 = jnp.bfloat16

_VMEM_LIMIT_BYTES = 56 * 1024 * 1024
_DFT_N2 = 128
_BF16_SUBLANES = 16
_ROW_GROUP = _BF16_SUBLANES

_dot = functools.partial(jnp.dot, preferred_element_type=F32)
_dot_hi = functools.partial(jnp.dot, preferred_element_type=F32,
                            precision=lax.Precision.HIGHEST)


def _dot_tn_bf16x3(a, b):
    dims = (((0,), (0,)), ((), ()))
    a_hi, b_hi = a.astype(BF16), b.astype(BF16)
    a_lo = (a - a_hi.astype(F32)).astype(BF16)
    b_lo = (b - b_hi.astype(F32)).astype(BF16)
    dg = functools.partial(lax.dot_general, dimension_numbers=dims, preferred_element_type=F32)
    return dg(a_hi, b_hi) + dg(a_lo, b_hi) + dg(a_hi, b_lo)


def _params(*sem):
    return pltpu.CompilerParams(dimension_semantics=sem,
                                vmem_limit_bytes=_VMEM_LIMIT_BYTES)


def _resident(shape):
    zeros = (0,) * len(shape)
    return pl.BlockSpec(shape, lambda *_: zeros, pipeline_mode=pl.Buffered(1))


def _cos_sin(n):
    jk = (np.arange(n)[:, None] * np.arange(n)[None, :]) % n
    ang = 2.0 * np.pi * jk / n
    return np.cos(ang), np.sin(ang)


@functools.lru_cache(maxsize=None)
def _dft_tables(n1, n2):
    n = n1 * n2
    c1, s1 = _cos_sin(n1)
    c2, s2 = _cos_sin(n2)
    fr1, fi1 = c1, -s1
    fr2, fi2 = c2, -s2
    ang = 2.0 * np.pi * ((np.arange(n1)[:, None] * np.arange(n2)[None, :]) % n) / n
    twr, twi = np.cos(ang), -np.sin(ang)
    t = {}
    h = n1 // 2
    t["rows_fwd"] = np.concatenate([fr1, fi1], axis=0)
    t["rows_fwd_c"] = np.block([[fr1[:, :h], -fi1[:, :h]], [fi1[:, :h], fr1[:, :h]]])
    t["p"] = np.block([[fr2, -fi2], [fi2, fr2]])
    t["q"] = np.block([[-fi2, -fr2], [fr2, -fi2]])
    t["tw_k1"] = np.stack([np.concatenate([twr, twr], axis=1),
                           np.concatenate([twi, twi], axis=1)], axis=1)
    t["inv2"] = np.block([[fr2, fi2], [-fi2, fr2]])
    t["pc"] = np.block([[fr1[:h], fi1[:h]], [-fi1[:h], fr1[:h]]])
    t["qc"] = np.block([[-fi1[:h], fr1[:h]], [-fr1[:h], -fi1[:h]]])
    t["tw_n2"] = np.stack([np.concatenate([twr.T, twr.T], axis=1),
                           np.concatenate([twi.T, twi.T], axis=1)], axis=1)
    return t


def _const(a, dtype=F32):
    return jnp.asarray(np.asarray(a, np.float32), dtype)


def _cast_concat_kernel(*refs):
    o_ref = refs[-1]
    lo = 0
    for w_ref in refs[:-1]:
        o_ref[:, lo:lo + w_ref.shape[1]] = w_ref[...].astype(BF16)
        lo += w_ref.shape[1]


def _cast_concat(ws):
    k = ws[0].shape[0]
    n = sum(w.shape[1] for w in ws)
    rb = 128
    return pl.pallas_call(
        _cast_concat_kernel,
        grid=(k // rb,),
        in_specs=[pl.BlockSpec((rb, w.shape[1]), lambda i: (i, 0)) for w in ws],
        out_specs=pl.BlockSpec((rb, n), lambda i: (i, 0)),
        out_shape=jax.ShapeDtypeStruct((k, n), BF16),
        compiler_params=_params("parallel"),
        name="cast_concat",
    )(*ws)


def _project_kernel(x_ref, g_ref, w_ref, b_ref, o_ref, h_ref, *, gate_tile, chunk):
    j = pl.program_id(1)
    tm = x_ref.shape[0]

    def emit(gated, rows):
        for lo in range(0, o_ref.shape[1], chunk):
            acc = _dot(h_ref[rows, :], w_ref[:, lo:lo + chunk])
            if gated:
                acc = jax.nn.sigmoid(acc + b_ref[:, lo:lo + chunk])
            o_ref[rows, lo:lo + chunk] = acc.astype(BF16)

    @pl.when(j == 0)
    def _():
        for r0 in range(0, tm, tm // 2):
            rows = slice(r0, r0 + tm // 2)
            x = x_ref[rows, :]
            ms = jnp.mean(x * x, axis=-1, keepdims=True)
            h_ref[rows, :] = (x * lax.rsqrt(ms + EPS) * g_ref[...]).astype(BF16)
            emit(False, rows)

    @pl.when(jnp.logical_and(j > 0, j < gate_tile))
    def _():
        emit(False, slice(None))

    @pl.when(j >= gate_tile)
    def _():
        emit(True, slice(None))


def _project(x2, g, w_cat, b_gate):
    rows = x2.shape[0]
    n_out = w_cat.shape[1]
    tm, tn = 1024, 2048
    gate_tile = (n_out - b_gate.shape[1]) // tn
    assert gate_tile >= 1, "the first column tile of a row tile is not gated"
    return pl.pallas_call(
        functools.partial(_project_kernel, gate_tile=gate_tile, chunk=tn),
        grid=(rows // tm, n_out // tn),
        in_specs=[
            pl.BlockSpec((tm, D_MODEL), lambda i, j: (i, 0)),
            pl.BlockSpec((1, D_MODEL), lambda i, j: (0, 0)),
            pl.BlockSpec((D_MODEL, tn), lambda i, j: (0, j)),
            pl.BlockSpec((1, tn), lambda i, j: (0, jnp.maximum(j - gate_tile, 0))),
        ],
        out_specs=pl.BlockSpec((tm, tn), lambda i, j: (i, j)),
        out_shape=jax.ShapeDtypeStruct((rows, n_out), BF16),
        scratch_shapes=[pltpu.VMEM((tm, D_MODEL), BF16)],
        compiler_params=_params("parallel", "arbitrary"),
        name="project",
    )(x2, g, w_cat, b_gate)


def _swap_major(x):
    return jnp.transpose(x, (1, 0, 2))


def _rows_fwd_kernel(lhs_ref, x_ref, o_ref):
    _, parts, k, sb, cb = x_ref.shape
    x = x_ref[0].astype(BF16).reshape(parts * k, sb, cb)
    xt = _swap_major(x)
    lhs = lhs_ref[...].astype(BF16)
    y = jnp.stack([_dot(lhs, xt[s]).astype(BF16) for s in range(sb)], axis=0)
    o_ref[0] = _swap_major(y).reshape(o_ref.shape[1:])


def _rows_fwd(lhs, x5, name, c=None):
    b, parts, k, n2, _ = x5.shape
    c = c or x5.shape[-1]
    n1 = lhs.shape[0] // 2
    sb, cb = _ROW_GROUP, 1024
    return pl.pallas_call(
        _rows_fwd_kernel,
        grid=(b, n2 // sb, c // cb),
        in_specs=[_resident(lhs.shape),
                  pl.BlockSpec((1, parts, k, sb, cb), lambda bi, g, ci: (bi, 0, 0, g, ci))],
        out_specs=pl.BlockSpec((1, 2, n1, sb, cb), lambda bi, g, ci: (bi, 0, 0, g, ci)),
        out_shape=jax.ShapeDtypeStruct((b, 2, n1, n2, c), BF16),
        compiler_params=_params("parallel", "parallel", "parallel"),
        name=name,
    )(lhs, x5)


def _stage2_fwd(p_ref, q_ref, tw_ref, y_ref, i):
    g = (p_ref[...] * tw_ref[i, 0:1, :] + q_ref[...] * tw_ref[i, 1:2, :]).astype(BF16)
    ys = jnp.concatenate([y_ref[0, 0, i], y_ref[0, 1, i]], axis=0)
    return _dot(g, ys)


def _conv_mid_kernel(p_ref, q_ref, tw_ref, inv_ref, y_ref, yk_ref, ss_ref, o_ref, *, k1b, n2):
    c = y_ref.shape[-1]
    scale = lax.rsqrt(ss_ref[...] + EPS)
    inv = inv_ref[...].astype(BF16)
    for i in range(k1b):
        g = (p_ref[...] * tw_ref[i, 0:1, :] + q_ref[...] * tw_ref[i, 1:2, :]).astype(BF16)
        stacked = jnp.concatenate(
            [jnp.concatenate([y_ref[0, 0, i], yk_ref[0, 0, i]], axis=1),
             jnp.concatenate([y_ref[0, 1, i], yk_ref[0, 1, i]], axis=1)], axis=0)
        s = _dot(g, stacked)
        xr, xi = s[:n2, :c], s[n2:, :c]
        kr, ki = s[:n2, c:] * scale, s[n2:, c:] * scale
        prod = jnp.concatenate([xr * kr - xi * ki, xr * ki + xi * kr], axis=0)
        z = _dot(inv, prod.astype(BF16))
        o_ref[0, 0, i] = z[:n2].astype(BF16)
        o_ref[0, 1, i] = z[n2:].astype(BF16)


def _fnet_mid_kernel(p_ref, q_ref, tw_ref, y_ref, cc_ref, sc_ref, o_ref, *, k1b, n2):
    cb = o_ref.shape[-1]
    cc = cc_ref[...].astype(BF16)
    sc = sc_ref[...].astype(BF16)
    urs, uis = [], []
    for i in range(k1b):
        s = _stage2_fwd(p_ref, q_ref, tw_ref, y_ref, i)
        urs.append(s[:n2].astype(BF16))
        uis.append(s[n2:].astype(BF16))
    ur = jnp.concatenate(urs, axis=0)
    ui = jnp.concatenate(uis, axis=0)
    groups = []
    for lo in range(0, cb, FNET_GROUP_DIM):
        hi = lo + FNET_GROUP_DIM
        groups.append((_dot(ur[:, lo:hi], cc) + _dot(ui[:, lo:hi], sc)).astype(BF16))
    mixed = jnp.concatenate(groups, axis=1).reshape(k1b, n2, cb)
    o_ref[0] = _swap_major(mixed)


def _stage2_specs(n2, c, k1b):
    return [
        _resident((2 * n2, 2 * n2)),
        _resident((2 * n2, 2 * n2)),
        pl.BlockSpec((k1b, 2, 2 * n2), lambda k, b, *_: (k, 0, 0)),
    ], pl.BlockSpec((1, 2, k1b, n2, c), lambda k, b, *_: (b, 0, k, 0, 0))


def _fnet(proj, b, l):
    n2 = _DFT_N2
    n1 = l // n2
    c = D_FNET
    t = _dft_tables(n1, n2)
    lhs = _const(t["rows_fwd"] / math.sqrt(l))
    y = _rows_fwd(lhs, proj.reshape(b, 1, n1, n2, proj.shape[-1]), "fnet_rows", c)
    cc, sc = _cos_sin(FNET_GROUP_DIM)
    norm = 1.0 / math.sqrt(FNET_GROUP_DIM)
    k1b, cb = _ROW_GROUP, 4 * FNET_GROUP_DIM
    head, _ = _stage2_specs(n2, c, k1b)
    out = pl.pallas_call(
        functools.partial(_fnet_mid_kernel, k1b=k1b, n2=n2),
        grid=(n1 // k1b, b, c // cb),
        in_specs=head + [pl.BlockSpec((1, 2, k1b, n2, cb), lambda k, bi, ci: (bi, 0, k, 0, ci)),
                         _resident((FNET_GROUP_DIM, FNET_GROUP_DIM)),
                         _resident((FNET_GROUP_DIM, FNET_GROUP_DIM))],
        out_specs=pl.BlockSpec((1, n2, k1b, cb), lambda k, bi, ci: (bi, 0, k, ci)),
        out_shape=jax.ShapeDtypeStruct((b, n2, n1, c), BF16),
        compiler_params=_params("parallel", "parallel", "parallel"),
        name="fnet_mid",
    )(_const(t["p"]), _const(t["q"]), _const(t["tw_k1"]), y,
      _const(cc * norm), _const(sc * norm))
    return out.reshape(b * l, c)


def _shortconv_kernel(*refs, r, nt):
    u_refs, prev_refs, next_refs = refs[0:3], refs[3:6], refs[6:9]
    w_ref, b_ref, vx_ref, x1_ref = refs[9:]
    i = pl.program_id(1)
    row = lax.broadcasted_iota(jnp.int32, (r, 1), 0)
    outs = []
    for part in range(3):
        cols = slice(part * D_HYENA, (part + 1) * D_HYENA)
        u = u_refs[part][0].astype(F32)
        prev_row = prev_refs[part][0].astype(F32)[_BF16_SUBLANES - 1:_BF16_SUBLANES]
        next_row = next_refs[part][0].astype(F32)[0:1]
        prev_row = jnp.where(i == 0, 0.0, prev_row)
        next_row = jnp.where(i == nt - 1, 0.0, next_row)
        u_prev = jnp.where(row == 0, prev_row, pltpu.roll(u, 1, axis=0))
        u_next = jnp.where(row == r - 1, next_row, pltpu.roll(u, r - 1, axis=0))
        outs.append(u_prev * w_ref[0:1, cols] + u * w_ref[1:2, cols]
                    + u_next * w_ref[2:3, cols] + b_ref[:, cols])
    x1, x2, v = outs
    vx_ref[0] = (v * x2).astype(BF16)
    x1_ref[0] = x1.astype(BF16)


def _shortconv(proj3, conv_w, conv_b):
    b, l, _ = proj3.shape
    c = D_HYENA
    r = 1024
    nt = l // r
    hb = _BF16_SUBLANES
    halo = r // hb
    first_block = D_FNET // c
    tiles = [pl.BlockSpec((1, r, c), lambda bi, i, p=p: (bi, i, first_block + p))
             for p in range(3)]
    prevs = [pl.BlockSpec((1, hb, c),
                          lambda bi, i, p=p: (bi, jnp.maximum(i * halo - 1, 0), first_block + p))
             for p in range(3)]
    nexts = [pl.BlockSpec((1, hb, c),
                          lambda bi, i, p=p: (bi, jnp.minimum((i + 1) * halo, l // hb - 1),
                                              first_block + p))
             for p in range(3)]
    return pl.pallas_call(
        functools.partial(_shortconv_kernel, r=r, nt=nt),
        grid=(b, nt),
        in_specs=tiles + prevs + nexts + [_resident((3, 3 * c)), _resident((1, 3 * c))],
        out_specs=[pl.BlockSpec((1, r, c), lambda bi, i: (bi, i, 0))] * 2,
        out_shape=[jax.ShapeDtypeStruct((b, l, c), BF16)] * 2,
        compiler_params=_params("parallel", "parallel"),
        name="shortconv",
    )(*([proj3] * 9), conv_w, conv_b)


def _filter_rows_kernel(bands_ref, w1t_ref, w1c_ref, w1s_ref, b1_ref, w2_ref, b2_ref,
                        w3_ref, b3_ref, fr_ref, w4_ref, deltas_ref, lhs_ref, y_ref, ss_ref,
                        decay_ref, *, l, n2, sb, chunk):
    g = pl.program_id(0)
    n1 = lhs_ref.shape[1]
    half = n1 // 2
    lanes = sb * half
    c = deltas_ref.shape[1]

    def offsets(shape, axis):
        idx = lax.broadcasted_iota(jnp.int32, shape, axis)
        back = idx >= lanes
        idx = jnp.where(back, idx - lanes, idx)
        s = lax.shift_right_logical(idx, half.bit_length() - 1)
        return back, n2 * (idx & (half - 1)) + s

    back, off = offsets((1, 2 * lanes), 1)
    lag = jnp.where(back, l - off - sb * g, off + sb * g).astype(F32)
    ang = bands_ref[...] * ((2.0 * math.pi / l) * lag)
    fr = fr_ref[...]
    pre = (w1t_ref[...] * (lag * (1.0 / (l - 1))) + _dot_hi(w1c_ref[...], jnp.cos(ang))
           - _dot_hi(w1s_ref[...], jnp.sin(ang)) + b1_ref[...])
    h = jnp.sin(fr * pre)
    h = jnp.sin(fr * (_dot_hi(w2_ref[...], h) + b2_ref[...]))
    h = jnp.sin(fr * (_dot_hi(w3_ref[...], h) + b3_ref[...]))

    @pl.when(g == 0)
    def _():
        ss_ref[...] = jnp.zeros_like(ss_ref)
        _, off_col = offsets((lanes, 1), 0)
        off_col = off_col.astype(F32)
        decay_ref[0] = jnp.exp(-(off_col * (1.0 / (l - 1))) * deltas_ref[...])
        decay_ref[1] = jnp.exp(-((l - off_col) * (1.0 / (l - 1))) * deltas_ref[...])

    shift = (sb * g).astype(F32) * (1.0 / (l - 1))
    row0 = lax.broadcasted_iota(jnp.int32, (lanes, 1), 0) == 0
    lhs = lhs_ref[...].astype(BF16)
    for lo in range(0, c, chunk):
        cols = slice(lo, lo + chunk)
        dl = deltas_ref[:, cols]
        k_fwd = (_dot_tn_bf16x3(h[:, :lanes], w4_ref[:, lo:lo + chunk])
                 * (decay_ref[0, :, cols] * jnp.exp(-shift * dl)))
        k_bwd = (_dot_tn_bf16x3(h[:, lanes:], w4_ref[:, c + lo:c + lo + chunk])
                 * (decay_ref[1, :, cols] * jnp.exp(shift * dl)))
        k_bwd = jnp.where(jnp.logical_and(row0, g == 0), 0.0, k_bwd)
        ss_ref[:, cols] += jnp.sum(k_fwd * k_fwd + k_bwd * k_bwd, axis=0, keepdims=True)
        ys = []
        for s in range(sb):
            rows = slice(s * half, (s + 1) * half)
            xs = jnp.concatenate([k_fwd[rows], k_bwd[rows]], axis=0).astype(BF16)
            ys.append(_dot(lhs, xs).astype(BF16))
        y = _swap_major(jnp.stack(ys, axis=0))
        y_ref[0, :, :, :, cols] = y.reshape(2, n1, sb, chunk)


def _filter_rows(l, w1, b1, w2, b2, w3, b3, w4, freq):
    n = 2 * l
    n2 = _DFT_N2
    n1 = n // n2
    c = D_HYENA
    sb = _ROW_GROUP
    t = _dft_tables(n1, n2)
    bands = np.linspace(1e-4, FILTER_BANDS - 1, FILTER_BANDS)[:, None]
    deltas = np.abs(np.linspace(math.log(DECAY_TARGET) / FAST_DECAY_PCT,
                                math.log(DECAY_TARGET) / SLOW_DECAY_PCT, D_HYENA))[None, :]
    ins = [_const(bands), w1[0:1].T, w1[1:1 + FILTER_BANDS].T, w1[1 + FILTER_BANDS:].T,
           b1[:, None], w2.T, b2[:, None], w3.T, b3[:, None], freq[:, None], w4,
           _const(deltas), _const(t["rows_fwd"])]
    y, ss = pl.pallas_call(
        functools.partial(_filter_rows_kernel, l=l, n2=n2, sb=sb, chunk=512),
        grid=(n2 // sb,),
        in_specs=[_resident(a.shape) for a in ins],
        out_specs=[pl.BlockSpec((1, 2, n1, sb, c), lambda g: (0, 0, 0, g, 0)),
                   pl.BlockSpec((1, c), lambda g: (0, 0))],
        out_shape=[jax.ShapeDtypeStruct((1, 2, n1, n2, c), BF16),
                   jax.ShapeDtypeStruct((1, c), F32)],
        scratch_shapes=[pltpu.VMEM((2, sb * (n1 // 2), c), F32)],
        compiler_params=_params("arbitrary"),
        name="filter_rows",
    )(*ins)
    return y, ss


def _conv_out_kernel(pc_ref, qc_ref, tw_ref, z_ref, vx_ref, x1_ref, skip_ref, o_ref):
    _, _, n1, sb, cb = z_ref.shape
    z = _swap_major(z_ref[0].reshape(2 * n1, sb, cb))
    ys = []
    for s in range(sb):
        lhs = (pc_ref[...] * tw_ref[s, 0:1, :] + qc_ref[...] * tw_ref[s, 1:2, :]).astype(BF16)
        ys.append(_dot(lhs, z[s]))
    y = _swap_major(jnp.stack(ys, axis=0)).reshape(o_ref.shape)
    y = y + vx_ref[...].astype(F32) * skip_ref[...]
    o_ref[...] = (y * x1_ref[...].astype(F32)).astype(BF16)


def _longconv(vx, x1, filter_rows, filter_ss, skip):
    b, l, c = vx.shape
    assert b == 2, "batch pair is packed as (re, im)"
    n = 2 * l
    n2 = _DFT_N2
    n1 = n // n2
    half = n1 // 2
    t = _dft_tables(n1, n2)
    y = _rows_fwd(_const(t["rows_fwd_c"]), vx.reshape(1, 2, half, n2, c), "conv_rows")
    k1b = 8
    head, y_spec = _stage2_specs(n2, c, k1b)
    z = pl.pallas_call(
        functools.partial(_conv_mid_kernel, k1b=k1b, n2=n2),
        grid=(n1 // k1b, 1),
        in_specs=head + [_resident((2 * n2, 2 * n2)), y_spec, y_spec, _resident((1, c))],
        out_specs=y_spec,
        out_shape=jax.ShapeDtypeStruct((1, 2, n1, n2, c), BF16),
        compiler_params=_params("parallel", "parallel"),
        name="conv_mid",
    )(_const(t["p"]), _const(t["q"]), _const(t["tw_k1"]), _const(t["inv2"] / n), y,
      filter_rows, filter_ss)

    sb, cb = _ROW_GROUP, 1024
    seq_spec = pl.BlockSpec((2, half, sb, cb), lambda g, ci: (0, 0, g, ci))
    out = pl.pallas_call(
        _conv_out_kernel,
        grid=(n2 // sb, c // cb),
        in_specs=[_resident((n1, 2 * n1)), _resident((n1, 2 * n1)),
                  pl.BlockSpec((sb, 2, 2 * n1), lambda g, ci: (g, 0, 0)),
                  pl.BlockSpec((1, 2, n1, sb, cb), lambda g, ci: (0, 0, 0, g, ci)),
                  seq_spec, seq_spec,
                  pl.BlockSpec((1, cb), lambda g, ci: (0, ci))],
        out_specs=seq_spec,
        out_shape=jax.ShapeDtypeStruct((2, half, n2, c), BF16),
        compiler_params=_params("parallel", "parallel"),
        name="conv_out",
    )(_const(t["pc"]), _const(t["qc"]), _const(t["tw_n2"]), z,
      vx.reshape(2, half, n2, c), x1.reshape(2, half, n2, c), skip)
    return out.reshape(b * l, c)


def _merge_kernel(mx_ref, yh_ref, ga_ref, gb_ref, x_ref, wf_ref, wh_ref, wo_ref, g2_ref,
                  x1_ref, h2_ref, m_ref, *, chunk):
    d = x_ref.shape[1]
    for lo in range(0, d, chunk):
        cols = slice(lo, lo + chunk)
        ya = _dot(mx_ref[...], wf_ref[:, cols])
        yb = _dot(yh_ref[...], wh_ref[:, cols])
        m_ref[:, cols] = (ga_ref[:, cols].astype(F32) * ya
                          + gb_ref[:, cols].astype(F32) * yb).astype(BF16)
    ss = jnp.zeros((x_ref.shape[0], 1), F32)
    for lo in range(0, d, chunk):
        cols = slice(lo, lo + chunk)
        x1 = x_ref[:, cols] + _dot(m_ref[...], wo_ref[:, cols])
        x1_ref[:, cols] = x1
        ss = ss + jnp.sum(x1 * x1, axis=-1, keepdims=True)
    scale = lax.rsqrt(ss * (1.0 / d) + EPS)
    for lo in range(0, d, chunk):
        cols = slice(lo, lo + chunk)
        h2_ref[:, cols] = (x1_ref[:, cols] * scale * g2_ref[:, cols]).astype(BF16)


def _merge(mixed, yh, proj, x2, wf, wh, wo, g2):
    rows = x2.shape[0]
    tm = 512
    d = D_MODEL
    gate_block = (proj.shape[1] - 2 * d) // d
    return pl.pallas_call(
        functools.partial(_merge_kernel, chunk=256),
        grid=(rows // tm,),
        in_specs=[
            pl.BlockSpec((tm, D_FNET), lambda i: (i, 0)),
            pl.BlockSpec((tm, D_HYENA), lambda i: (i, 0)),
            pl.BlockSpec((tm, d), lambda i: (i, gate_block)),
            pl.BlockSpec((tm, d), lambda i: (i, gate_block + 1)),
            pl.BlockSpec((tm, d), lambda i: (i, 0)),
            _resident((D_FNET, d)), _resident((D_HYENA, d)), _resident((d, d)),
            _resident((1, d)),
        ],
        out_specs=[pl.BlockSpec((tm, d), lambda i: (i, 0))] * 2,
        out_shape=[jax.ShapeDtypeStruct((rows, d), F32),
                   jax.ShapeDtypeStruct((rows, d), BF16)],
        scratch_shapes=[pltpu.VMEM((tm, d), BF16)],
        compiler_params=_params("parallel"),
        name="merge",
    )(mixed, yh, proj, proj, x2, wf, wh, wo, g2)


def _mlp_kernel(h2_ref, x1_ref, w1_ref, w2_ref, gf_ref, o_ref, *, nj):
    j = pl.program_id(1)
    slice_rows = x1_ref.shape[0]

    def step(first):
        t = _dot(h2_ref[...], w1_ref[...])
        t = jnp.square(jnp.maximum(t, 0.0)).astype(BF16)
        part = _dot(t, w2_ref[...].astype(BF16))
        if first:
            o_ref[...] = part
        else:
            o_ref[...] += part
        rows = pl.ds(pl.multiple_of(j * slice_rows, slice_rows), slice_rows)
        o_ref[rows, :] += x1_ref[...]

    @pl.when(j == 0)
    def _():
        step(True)

    @pl.when(j > 0)
    def _():
        step(False)

    @pl.when(j == nj - 1)
    def _():
        x = o_ref[...]
        ms = jnp.mean(x * x, axis=-1, keepdims=True)
        o_ref[...] = x * lax.rsqrt(ms + EPS) * gf_ref[...]


def _mlp(h2, x1, w1, w2, gf):
    rows = x1.shape[0]
    tm, tf = 1024, 1024
    d = D_MODEL
    nj = D_FF // tf
    return pl.pallas_call(
        functools.partial(_mlp_kernel, nj=nj),
        grid=(rows // tm, nj),
        in_specs=[
            pl.BlockSpec((tm, d), lambda i, j: (i, 0)),
            pl.BlockSpec((tm // nj, d), lambda i, j: (i * nj + j, 0)),
            pl.BlockSpec((d, tf), lambda i, j: (0, j)),
            pl.BlockSpec((tf, d), lambda i, j: (j, 0)),
            pl.BlockSpec((1, d), lambda i, j: (0, 0)),
        ],
        out_specs=pl.BlockSpec((tm, d), lambda i, j: (i, 0)),
        out_shape=jax.ShapeDtypeStruct((rows, d), F32),
        compiler_params=pltpu.CompilerParams(
            dimension_semantics=("parallel", "arbitrary"),
            vmem_limit_bytes=_VMEM_LIMIT_BYTES,
            allow_input_fusion=[False, False, True, False, False]),
        name="mlp",
    )(h2, x1, w1, w2, gf)


def _layer_and_final_norm(x, w, filt):
    b, l, d = x.shape
    x2 = x.reshape(b * l, d)
    proj = _project(x2, w["norm1_g"], w["w_cat"], w["b_gate"])
    mixed = _fnet(proj, b, l)
    vx, x1h = _shortconv(proj.reshape(b, l, proj.shape[-1]), w["conv_w"], w["conv_b"])
    filter_rows, filter_ss = _filter_rows(l, *filt)
    yh = _longconv(vx, x1h, filter_rows, filter_ss, w["skip"])
    x1, h2 = _merge(mixed, yh, proj, x2, w["w_fnet_map"], w["w_hyena_out"], w["w_out"],
                    w["norm2_g"])
    out = _mlp(h2, x1, w["w_mlp1"], w["w_mlp2"], w["norm_f_g"])
    return out.reshape(b, l, d)


def kernel(x_prompt, x_sample, norm1_g, w_in, conv_w, conv_b, filt_w1, filt_b1, filt_w2, filt_b2, filt_w3, filt_b3, filt_w4, filt_freq, hyena_skip, w_fnet_map, w_hyena_out, w_gate, b_gate, w_out, norm2_g, w_mlp1, w_mlp2, norm_f_g):
    assert norm1_g.shape[0] == 1, "one layer"
    w = {
        "norm1_g": norm1_g[0][None],
        "w_cat": _cast_concat([w_in[0], w_gate[0]]),
        "b_gate": b_gate[0][None],
        "conv_w": conv_w[0],
        "conv_b": conv_b[0][None],
        "skip": hyena_skip[0][None],
        "w_fnet_map": w_fnet_map[0].astype(BF16),
        "w_hyena_out": w_hyena_out[0].astype(BF16),
        "w_out": w_out[0].astype(BF16),
        "norm2_g": norm2_g[0][None],
        "w_mlp1": w_mlp1[0].astype(BF16),
        "w_mlp2": w_mlp2[0],
        "norm_f_g": norm_f_g[None],
    }
    filt = (filt_w1[0], filt_b1[0], filt_w2[0], filt_b2[0], filt_w3[0], filt_b3[0],
            filt_w4[0], filt_freq[0])
    return (_layer_and_final_norm(x_prompt, w, filt),
            _layer_and_final_norm(x_sample, w, filt))
```
